```python
import jax, jax.numpy as jnp
from jax import lax
import numpy as np

D_MODEL = 2048
BATCH = 1
SEQ = 16384
DEPTH = 1

D_MIX = 2 * D_MODEL
D_ATTN = D_MODEL // 1
D_SSM = D_MIX - D_ATTN

N_HEADS = 16
HEAD_DIM = D_ATTN // N_HEADS
N_KV_HEADS = 4
Q_PER_KV = N_HEADS // N_KV_HEADS
KV_DIM = N_KV_HEADS * HEAD_DIM
CMP_LEN = 32
CMP_STRIDE = 16
CMP_HIDDEN = 256
SEL_BLOCK = 64
N_SEL = 16
WINDOW = 512
Q_BLOCK = 128

SSM_HEAD_DIM = 64
SSM_HEADS = D_SSM // SSM_HEAD_DIM
SSM_GROUPS = 4
SSM_STATE = 128
SSM_CONV = 4
SSM_CHUNK = 256
D_CONV_CH = D_SSM + 2 * SSM_GROUPS * SSM_STATE

D_FF = 5632
FFN_CONV = 3

IN_SIZES = (D_SSM, D_CONV_CH, SSM_HEADS, D_ATTN, KV_DIM, KV_DIM, KV_DIM, KV_DIM, KV_DIM, KV_DIM, 3 * N_HEADS)
D_IN = D_SSM + D_CONV_CH + SSM_HEADS + D_ATTN + 6 * KV_DIM + 3 * N_HEADS

NORM_EPS = 1e-6
NEG_INF = -1e30
FORCE_SCORE = 1e4

kernel_name = 'hymba_ssd_nsa_convffn_block'


def rms_norm(x, w):
    xf = x.astype(jnp.float32)
    y = xf * lax.rsqrt(jnp.mean(xf * xf, axis=-1, keepdims=True) + NORM_EPS)
    return (y * w.astype(jnp.float32)).astype(x.dtype)


def masked_softmax(s, mask):
    p = jax.nn.softmax(jnp.where(mask, s, NEG_INF), axis=-1)
    return jnp.where(mask, p, 0.0)


def causal_depthwise_conv(u, w, b):
    k, c = w.shape
    y = lax.conv_general_dilated(u, w[:, None, :].astype(u.dtype), window_strides=(1,),
                                 padding=[(k - 1, 0)], dimension_numbers=('NWC', 'WIO', 'NWC'),
                                 feature_group_count=c)
    return y + b.astype(u.dtype)


def alibi_slopes(n):
    return 2.0 ** (-8.0 * jnp.arange(1, n + 1, dtype=jnp.float32) / n)


def ssd_chunked(xs, dt, a_log, d_skip, b_mat, c_mat):
    bsz, L = xs.shape[:2]
    G, R, P, N, Q = SSM_GROUPS, SSM_HEADS // SSM_GROUPS, SSM_HEAD_DIM, SSM_STATE, SSM_CHUNK
    n_chunks = -(-L // Q)
    pad = n_chunks * Q - L
    pad_l = lambda u: jnp.pad(u, [(0, 0), (0, pad)] + [(0, 0)] * (u.ndim - 2))
    f32 = jnp.float32
    x = pad_l(xs).reshape(bsz, n_chunks, Q, G, R, P).astype(f32)
    dtc = pad_l(dt).reshape(bsz, n_chunks, Q, G, R)
    bc = pad_l(b_mat).reshape(bsz, n_chunks, Q, G, N).astype(f32)
    cc = pad_l(c_mat).reshape(bsz, n_chunks, Q, G, N).astype(f32)
    A = -jnp.exp(a_log.astype(f32)).reshape(G, R)
    a = (dtc * A).transpose(0, 1, 3, 4, 2)
    a_cum = jnp.cumsum(a, axis=-1)
    causal = jnp.tril(jnp.ones((Q, Q), dtype=bool))
    decay_in = jnp.exp(jnp.where(causal, a_cum[..., :, None] - a_cum[..., None, :], NEG_INF))
    xdt = x * dtc[..., None]
    cb = jnp.einsum('bclgn,bcsgn->bcgls', cc, bc)
    y_diag = jnp.einsum('bcgrls,bcsgrp->bclgrp', cb[:, :, :, None] * decay_in, xdt)
    decay_out = jnp.exp(a_cum[..., -1:] - a_cum).transpose(0, 1, 4, 2, 3)
    states = jnp.einsum('bclgn,bclgrp->bcgrpn', bc, xdt * decay_out[..., None])
    chunk_decay = jnp.exp(a_cum[..., -1])

    def step(h, inp):
        s_c, d_c = inp
        return h * d_c[..., None, None] + s_c, h

    h0 = jnp.zeros((bsz, G, R, P, N), f32)
    _, prev = lax.scan(step, h0, (jnp.moveaxis(states, 1, 0), jnp.moveaxis(chunk_decay, 1, 0)))
    prev = jnp.moveaxis(prev, 0, 1)
    decay_to = jnp.exp(a_cum).transpose(0, 1, 4, 2, 3)
    y_off = jnp.einsum('bclgn,bcgrpn->bclgrp', cc, prev) * decay_to[..., None]
    y = y_diag + y_off + x * d_skip.astype(f32).reshape(G, R)[:, :, None]
    return y.reshape(bsz, n_chunks * Q, G * R * P)[:, :L].astype(xs.dtype)


def compress_blocks(kv, pos, w1, w2):
    L = kv.shape[2]
    n_cmp = (L - CMP_LEN) // CMP_STRIDE + 1
    idx = jnp.arange(n_cmp)[:, None] * CMP_STRIDE + jnp.arange(CMP_LEN)[None, :]
    blk = kv[:, :, idx] + pos.astype(kv.dtype)
    flat = blk.reshape(blk.shape[:3] + (CMP_LEN * HEAD_DIM,))
    return jax.nn.gelu(flat @ w1) @ w2


def nsa_attention(q, kc_raw, vc_raw, ks, vs, kw, vw, gate_logits, pos_k, w1_k, w2_k, pos_v, w1_v, w2_v):
    f32 = jnp.float32
    bsz, L = q.shape[:2]
    G, R = N_KV_HEADS, Q_PER_KV
    qh = q.reshape(bsz, L, G, R, HEAD_DIM).transpose(0, 2, 3, 1, 4)
    to_g = lambda u: u.transpose(0, 2, 1, 3)
    kc = compress_blocks(to_g(kc_raw), pos_k, w1_k, w2_k)
    vc = compress_blocks(to_g(vc_raw), pos_v, w1_v, w2_v)
    n_cmp = kc.shape[2]
    n_blk = L // SEL_BLOCK
    n_sel = min(N_SEL, n_blk)
    ksb = to_g(ks).reshape(bsz, G, n_blk, SEL_BLOCK, HEAD_DIM)
    vsb = to_g(vs).reshape(bsz, G, n_blk, SEL_BLOCK, HEAD_DIM)
    win_pad = ((0, 0), (0, 0), (WINDOW, 0), (0, 0))
    kw_pad = jnp.pad(to_g(kw), win_pad)
    vw_pad = jnp.pad(to_g(vw), win_pad)
    slopes = alibi_slopes(N_HEADS).reshape(G, R)[None, :, :, None, None]
    scale = HEAD_DIM ** -0.5
    cmp_end = jnp.arange(n_cmp) * CMP_STRIDE + CMP_LEN - 1
    a_r, b_r = SEL_BLOCK // CMP_STRIDE, CMP_LEN // CMP_STRIDE
    jb = jnp.arange(n_blk)
    mn = (jnp.arange(a_r)[:, None] + jnp.arange(b_r)[None, :]).reshape(-1)
    imp_idx = a_r * (jb[:, None] + 1) - 1 - mn[None, :] + (b_r - 1)
    bi = jnp.arange(bsz)[:, None, None, None]
    gi = jnp.arange(G)[None, :, None, None]
    offs_sel = jnp.arange(SEL_BLOCK)
    offs_win = jnp.arange(Q_BLOCK + WINDOW) - WINDOW

    def scores(s, dist):
        return s.astype(f32) * scale - slopes * dist.astype(f32)

    def query_block(qb):
        t0 = qb * Q_BLOCK
        t = t0 + jnp.arange(Q_BLOCK)
        qblk = lax.dynamic_slice_in_dim(qh, t0, Q_BLOCK, axis=3)
        dist_c = t[:, None] - cmp_end[None, :]
        p_c = masked_softmax(scores(jnp.einsum('bgrqd,bgkd->bgrqk', qblk, kc), dist_c), dist_c >= 0)
        o_c = jnp.einsum('bgrqk,bgkd->bgrqd', p_c.astype(vc.dtype), vc)
        p_pad = jnp.pad(p_c.sum(axis=2), ((0, 0), (0, 0), (0, 0), (b_r - 1, b_r - 1)))
        imp = p_pad[..., imp_idx].sum(-1)
        jt = (t // SEL_BLOCK)[:, None]
        forced = (jb == 0) | (jb == jt) | (jb == jt - 1)
        imp = jnp.where(forced, FORCE_SCORE, jnp.where(jb <= jt, imp, -1.0))
        _, sel = lax.top_k(imp, n_sel)
        k_sel = ksb[bi, gi, sel].reshape(bsz, G, Q_BLOCK, n_sel * SEL_BLOCK, HEAD_DIM)
        v_sel = vsb[bi, gi, sel].reshape(bsz, G, Q_BLOCK, n_sel * SEL_BLOCK, HEAD_DIM)
        pos_sel = (sel[..., None] * SEL_BLOCK + offs_sel).reshape(bsz, G, Q_BLOCK, n_sel * SEL_BLOCK)
        dist_s = (t[:, None] - pos_sel)[:, :, None]
        p_s = masked_softmax(scores(jnp.einsum('bgrqd,bgqkd->bgrqk', qblk, k_sel), dist_s), dist_s >= 0)
        o_s = jnp.einsum('bgrqk,bgqkd->bgrqd', p_s.astype(v_sel.dtype), v_sel)
        k_win = lax.dynamic_slice_in_dim(kw_pad, t0, Q_BLOCK + WINDOW, axis=2)
        v_win = lax.dynamic_slice_in_dim(vw_pad, t0, Q_BLOCK + WINDOW, axis=2)
        pos_w = t0 + offs_win
        dist_w = t[:, None] - pos_w[None, :]
        valid_w = (dist_w >= 0) & (dist_w < WINDOW) & (pos_w[None, :] >= 0)
        p_w = masked_softmax(scores(jnp.einsum('bgrqd,bgkd->bgrqk', qblk, k_win), dist_w), valid_w)
        o_w = jnp.einsum('bgrqk,bgkd->bgrqd', p_w.astype(v_win.dtype), v_win)
        return o_c, o_s, o_w

    o_c, o_s, o_w = lax.map(query_block, jnp.arange(L // Q_BLOCK))
    merge = lambda o: o.transpose(1, 2, 3, 0, 4, 5).reshape(bsz, G, R, L, HEAD_DIM).astype(f32)
    g = jax.nn.sigmoid(gate_logits.astype(f32)).reshape(bsz, L, 3, G, R).transpose(2, 0, 3, 4, 1)[..., None]
    o = g[0] * merge(o_c) + g[1] * merge(o_s) + g[2] * merge(o_w)
    return o.transpose(0, 3, 1, 2, 4).reshape(bsz, L, N_HEADS * HEAD_DIM).astype(q.dtype)


def setup_inputs(seed: int = 0) -> dict:
    key = jax.random.key(seed)
    ks = jax.random.split(key, 24)
    nrm = lambda k, shape, s: jax.random.normal(k, shape, jnp.float32) * s
    gain = lambda k, shape: 1.0 + 0.02 * jax.random.normal(k, shape, jnp.float32)
    dt0 = jnp.exp(jax.random.uniform(ks[5], (DEPTH, SSM_HEADS), jnp.float32, np.log(1e-3), np.log(1e-1)))
    return {
        'x': nrm(ks[0], (BATCH, SEQ, D_MODEL), 1.0),
        'mix_norm_w': gain(ks[1], (DEPTH, D_MODEL)),
        'w_in': nrm(ks[2], (DEPTH, D_MODEL, D_IN), D_MODEL ** -0.5),
        'ssm_conv_w': nrm(ks[3], (DEPTH, SSM_CONV, D_CONV_CH), SSM_CONV ** -0.5),
        'ssm_conv_b': nrm(ks[4], (DEPTH, D_CONV_CH), 0.01),
        'ssm_dt_bias': dt0 + jnp.log(-jnp.expm1(-dt0)),
        'ssm_a_log': jnp.log(jax.random.uniform(ks[6], (DEPTH, SSM_HEADS), jnp.float32, 1.0, 16.0)),
        'ssm_d': 1.0 + 0.1 * jax.random.normal(ks[7], (DEPTH, SSM_HEADS), jnp.float32),
        'ssm_norm_w': gain(ks[8], (DEPTH, D_SSM)),
        'cmp_pos_k': nrm(ks[9], (DEPTH, CMP_LEN, HEAD_DIM), 0.1),
        'cmp_w1_k': nrm(ks[10], (DEPTH, CMP_LEN * HEAD_DIM, CMP_HIDDEN), (CMP_LEN * HEAD_DIM) ** -0.5),
        'cmp_w2_k': nrm(ks[11], (DEPTH, CMP_HIDDEN, HEAD_DIM), CMP_HIDDEN ** -0.5),
        'cmp_pos_v': nrm(ks[12], (DEPTH, CMP_LEN, HEAD_DIM), 0.1),
        'cmp_w1_v': nrm(ks[13], (DEPTH, CMP_LEN * HEAD_DIM, CMP_HIDDEN), (CMP_LEN * HEAD_DIM) ** -0.5),
        'cmp_w2_v': nrm(ks[14], (DEPTH, CMP_HIDDEN, HEAD_DIM), CMP_HIDDEN ** -0.5),
        'attn_norm_w': gain(ks[15], (DEPTH, D_ATTN)),
        'w_out': nrm(ks[16], (DEPTH, D_MIX, D_MODEL), D_MIX ** -0.5),
        'ffn_norm_w': gain(ks[17], (DEPTH, D_MODEL)),
        'w_up': nrm(ks[18], (DEPTH, D_MODEL, 2 * D_FF), D_MODEL ** -0.5),
        'ffn_conv_w': nrm(ks[19], (DEPTH, FFN_CONV, 2 * D_FF), FFN_CONV ** -0.5),
        'ffn_conv_b': nrm(ks[20], (DEPTH, 2 * D_FF), 0.01),
        'w_down': nrm(ks[21], (DEPTH, D_FF, D_MODEL), D_FF ** -0.5),
        'final_norm_w': gain(ks[22], (D_MODEL,)),
    }


def reference(x, mix_norm_w, w_in, ssm_conv_w, ssm_conv_b, ssm_dt_bias, ssm_a_log, ssm_d, ssm_norm_w,
              cmp_pos_k, cmp_w1_k, cmp_w2_k, cmp_pos_v, cmp_w1_v, cmp_w2_v, attn_norm_w, w_out,
              ffn_norm_w, w_up, ffn_conv_w, ffn_conv_b, w_down, final_norm_w):
    bsz, L, _ = x.shape
    split_at = [int(v) for v in np.cumsum(IN_SIZES)[:-1]]
    kv_shape = lambda u: u.reshape(bsz, L, N_KV_HEADS, HEAD_DIM)
    for l in range(DEPTH):
        h = rms_norm(x, mix_norm_w[l])
        proj = h @ w_in[l]
        z, xbc, dt_raw, q, kc, vc, ksl, vsl, kwn, vwn, gl = jnp.split(proj, split_at, axis=-1)
        xbc = jax.nn.silu(causal_depthwise_conv(xbc, ssm_conv_w[l], ssm_conv_b[l]))
        xs, bm, cm = jnp.split(xbc, [D_SSM, D_SSM + SSM_GROUPS * SSM_STATE], axis=-1)
        dt = jax.nn.softplus(dt_raw.astype(jnp.float32) + ssm_dt_bias[l].astype(jnp.float32))
        y_ssm = ssd_chunked(xs.reshape(bsz, L, SSM_HEADS, SSM_HEAD_DIM), dt, ssm_a_log[l], ssm_d[l],
                            bm.reshape(bsz, L, SSM_GROUPS, SSM_STATE), cm.reshape(bsz, L, SSM_GROUPS, SSM_STATE))
        y_ssm = rms_norm(y_ssm * jax.nn.silu(z), ssm_norm_w[l])
        y_attn = nsa_attention(q.reshape(bsz, L, N_HEADS, HEAD_DIM), kv_shape(kc), kv_shape(vc),
                               kv_shape(ksl), kv_shape(vsl), kv_shape(kwn), kv_shape(vwn),
                               gl.reshape(bsz, L, 3, N_HEADS), cmp_pos_k[l], cmp_w1_k[l], cmp_w2_k[l],
                               cmp_pos_v[l], cmp_w1_v[l], cmp_w2_v[l])
        y_attn = rms_norm(y_attn, attn_norm_w[l])
        x = x + jnp.concatenate([y_attn, y_ssm], axis=-1) @ w_out[l]
        h = rms_norm(x, ffn_norm_w[l])
        u = causal_depthwise_conv(h @ w_up[l], ffn_conv_w[l], ffn_conv_b[l])
        gate, val = jnp.split(u, 2, axis=-1)
        x = x + (jax.nn.silu(gate) * val) @ w_down[l]
    return rms_norm(x, final_norm_w)
```

```python
import functools

import numpy as np
import jax
import jax.numpy as jnp
from jax import lax
from jax.experimental import pallas as pl
from jax.experimental.pallas import tpu as pltpu

F32 = jnp.float32
BF16 = jnp.bfloat16

D_MODEL = 2048
D_ATTN = 2048
D_SSM = 2048
N_HEADS = 16
HEAD_DIM = 128
N_KV_HEADS = 4
Q_PER_KV = 4
KV_DIM = 512
CMP_LEN = 32
CMP_STRIDE = 16
CMP_HIDDEN = 256
SEL_BLOCK = 64
N_SEL = 16
WINDOW = 512
Q_BLOCK = 128
SSM_HEAD_DIM = 64
SSM_HEADS = 32
SSM_GROUPS = 4
SSM_STATE = 128
SSM_CONV = 4
SSM_CHUNK = 256
D_FF = 5632
FFN_CONV = 3
NORM_EPS = 1e-6
NEG_INF = -1e30
FORCE_SCORE = 1e4

P_COLS = 8192
KV_COLS = 2048
TAIL = 128
PROJ_TN = 1024
MIB = 1024 * 1024


def _params(sem, vmem_mib):
    return pltpu.CompilerParams(dimension_semantics=sem, vmem_limit_bytes=vmem_mib * MIB)


def _rms(x, w):
    return x * lax.rsqrt(jnp.mean(x * x, axis=-1, keepdims=True) + NORM_EPS) * w


def _dot(a, b):
    return jnp.dot(a, b, preferred_element_type=F32)


def _dot_nt(a, b):
    return lax.dot_general(a, b, (((1,), (1,)), ((), ())), preferred_element_type=F32)


def _split3(x):
    hi = x.astype(BF16)
    r1 = x - hi.astype(F32)
    mid = r1.astype(BF16)
    lo = (r1 - mid.astype(F32)).astype(BF16)
    return hi, mid, lo


def _dot3(x, m):
    hi, mid, lo = _split3(x)
    return _dot(hi, m) + _dot(mid, m) + _dot(lo, m)


def _dot3_left(m, x):
    hi, mid, lo = _split3(x)
    return _dot(m, hi) + _dot(m, mid) + _dot(m, lo)


def _silu(x):
    return x * (1.0 / (1.0 + jnp.exp(-x)))


def _inproj_kernel(x_ref, nw_ref, w_ref, wt_ref, wtt_ref, p_ref, kv_ref, tail_ref, tailt_ref, xn_ref, *, n_p):
    j = pl.program_id(1)

    @pl.when(j == 0)
    def _():
        xn = _rms(x_ref[...], nw_ref[...]).astype(BF16)
        xn_ref[...] = xn
        tail_ref[...] = _dot(xn, wt_ref[...])
        tailt_ref[...] = _dot_nt(wtt_ref[...], xn)

    r = _dot(xn_ref[...], w_ref[...])

    @pl.when(j < n_p)
    def _():
        p_ref[...] = r

    @pl.when(j >= n_p)
    def _():
        for c in range(PROJ_TN // HEAD_DIM):
            kv_ref[c] = r[:, c * HEAD_DIM:(c + 1) * HEAD_DIM].astype(BF16)


def _in_projection(x2, norm_w, w_main, w_tail, w_tail_t, tm):
    L = x2.shape[0]
    n_p = P_COLS // PROJ_TN
    n_kv = KV_COLS // PROJ_TN
    hp = PROJ_TN // HEAD_DIM
    return pl.pallas_call(
        functools.partial(_inproj_kernel, n_p=n_p),
        grid=(L // tm, n_p + n_kv),
        in_specs=[
            pl.BlockSpec((tm, D_MODEL), lambda i, j: (i, 0)),
            pl.BlockSpec((1, D_MODEL), lambda i, j: (0, 0)),
            pl.BlockSpec((D_MODEL, PROJ_TN), lambda i, j: (0, j)),
            pl.BlockSpec((D_MODEL, TAIL), lambda i, j: (0, 0)),
            pl.BlockSpec((TAIL, D_MODEL), lambda i, j: (0, 0)),
        ],
        out_specs=[
            pl.BlockSpec((tm, PROJ_TN), lambda i, j: (i, jnp.minimum(j, n_p - 1))),
            pl.BlockSpec((hp, tm, HEAD_DIM), lambda i, j: (jnp.maximum(j - n_p, 0), i, 0)),
            pl.BlockSpec((tm, TAIL), lambda i, j: (i, 0)),
            pl.BlockSpec((TAIL, tm), lambda i, j: (0, i)),
        ],
        out_shape=[
            jax.ShapeDtypeStruct((L, P_COLS), F32),
            jax.ShapeDtypeStruct((KV_COLS // HEAD_DIM, L, HEAD_DIM), BF16),
            jax.ShapeDtypeStruct((L, TAIL), F32),
            jax.ShapeDtypeStruct((TAIL, L), F32),
        ],
        scratch_shapes=[pltpu.VMEM((tm, D_MODEL), BF16)],
        compiler_params=_params(("arbitrary", "arbitrary"), 56),
        name="in_projection",
    )(x2, norm_w, w_main, w_tail, w_tail_t)


def _compress_kernel(x_ref, pos_ref, w1_ref, w2_ref, o_ref, *, n_rows):
    half = CMP_LEN // 2
    acc0 = jnp.zeros((n_rows, CMP_HIDDEN), F32)
    acc1 = jnp.zeros((n_rows, CMP_HIDDEN), F32)
    for i in range(half):
        xi = x_ref[pl.ds(i, n_rows, stride=CMP_STRIDE), :]
        a0 = (xi + pos_ref[i:i + 1, :]).astype(BF16)
        a1 = (xi + pos_ref[half + i:half + i + 1, :]).astype(BF16)
        acc0 = acc0 + _dot(a0, w1_ref[i * HEAD_DIM:(i + 1) * HEAD_DIM, :])
        acc1 = acc1 + _dot(a1, w1_ref[(half + i) * HEAD_DIM:(half + i + 1) * HEAD_DIM, :])
    hid = acc0 + pltpu.roll(acc1, n_rows - 1, 0)
    o_ref[...] = _dot(jax.nn.gelu(hid).astype(BF16), w2_ref[...])


def _compress(p, pos, w1, w2):
    L = p.shape[0]
    n_rows = L // CMP_STRIDE
    col0 = (P_COLS - 2 * KV_DIM) // HEAD_DIM
    return pl.pallas_call(
        functools.partial(_compress_kernel, n_rows=n_rows),
        grid=(2, N_KV_HEADS),
        in_specs=[
            pl.BlockSpec((L, HEAD_DIM), lambda s, g: (0, col0 + N_KV_HEADS * s + g)),
            pl.BlockSpec((None, CMP_LEN, HEAD_DIM), lambda s, g: (s, 0, 0)),
            pl.BlockSpec((None, CMP_LEN * HEAD_DIM, CMP_HIDDEN), lambda s, g: (s, 0, 0)),
            pl.BlockSpec((None, CMP_HIDDEN, HEAD_DIM), lambda s, g: (s, 0, 0)),
        ],
        out_specs=pl.BlockSpec((None, None, n_rows, HEAD_DIM), lambda s, g: (s, g, 0, 0)),
        out_shape=jax.ShapeDtypeStruct((2, N_KV_HEADS, n_rows, HEAD_DIM), F32),
        compiler_params=_params(("arbitrary", "arbitrary"), 48),
        name="kv_compress",
    )(p, pos, w1, w2)


def _ssd_kernel(xs_ref, bc_ref, xs_halo_ref, bc_halo_ref, z_ref, tail_ref, tailt_ref, cw_ref, cb_ref,
                dtb_row_ref, dtb_col_ref, alog_row_ref, alog_col_ref, dskip_ref, expand_ref, nw_ref,
                o_ref, state_ref, cx_ref, cbc_ref):
    c = pl.program_id(0)
    Q = SSM_CHUNK
    GW = D_SSM // SSM_GROUPS
    halo = 8

    @pl.when(c == 0)
    def _():
        state_ref[...] = jnp.zeros_like(state_ref)
        cx_ref[0:halo, :] = jnp.zeros((halo, D_SSM), F32)
        cbc_ref[0:halo, :] = jnp.zeros((halo, 2 * SSM_GROUPS * SSM_STATE), F32)

    @pl.when(c > 0)
    def _():
        cx_ref[0:halo, :] = xs_halo_ref[...]
        cbc_ref[0:halo, :] = bc_halo_ref[...]

    cx_ref[halo:halo + Q, :] = xs_ref[...]
    cbc_ref[halo:halo + Q, :] = bc_ref[...]

    def conv_silu(ref, w, b):
        acc = b
        for k in range(SSM_CONV):
            acc = acc + ref[pl.ds(halo - (SSM_CONV - 1) + k, Q), :] * w[k:k + 1, :]
        return _silu(acc)

    cw = cw_ref[...]
    cb = cb_ref[...]
    xs = conv_silu(cx_ref, cw[:, :D_SSM], cb[:, :D_SSM])
    bcm = conv_silu(cbc_ref, cw[:, D_SSM:], cb[:, D_SSM:])

    def softplus(v):
        return jnp.maximum(v, 0.0) + jnp.log1p(jnp.exp(-jnp.abs(v)))

    dt = softplus(tail_ref[...] + dtb_row_ref[...])
    a = dt * (-jnp.exp(alog_row_ref[...]))
    rows = lax.broadcasted_iota(jnp.int32, (Q, Q), 0)
    cols = lax.broadcasted_iota(jnp.int32, (Q, Q), 1)
    causal = cols <= rows
    tri = jnp.where(causal, 1.0, 0.0).astype(BF16)
    a_cum = _dot3_left(tri, a)
    dt_t = softplus(tailt_ref[0:SSM_HEADS, :] + dtb_col_ref[...])
    a_t = dt_t * (-jnp.exp(alog_col_ref[...]))
    tri_t = jnp.where(rows <= cols, 1.0, 0.0).astype(BF16)
    a_cum_t = _dot3(a_t, tri_t)

    expand = expand_ref[...]
    dt_e = _dot3(dt, expand)
    ac_e = _dot3(a_cum, expand)
    xdt = xs * dt_e
    decay_to = jnp.exp(ac_e)
    a_last = ac_e[Q - 1:Q, :]
    xdd = xdt * jnp.exp(a_last - ac_e)
    chunk_decay = jnp.exp(a_last)

    lane = lax.broadcasted_iota(jnp.int32, (Q, 2 * SSM_HEAD_DIM), 1)
    first_head = lane < SSM_HEAD_DIM
    y_groups = []
    for g in range(SSM_GROUPS):
        bg = bcm[:, g * SSM_STATE:(g + 1) * SSM_STATE]
        cg = bcm[:, (SSM_GROUPS + g) * SSM_STATE:(SSM_GROUPS + g + 1) * SSM_STATE].astype(BF16)
        gmat = _dot_nt(cg, bg.astype(BF16))
        st = state_ref[g]
        y_off = _dot(cg, st.astype(BF16)) * decay_to[:, g * GW:(g + 1) * GW]
        new = _dot(bg.T.astype(BF16), xdd[:, g * GW:(g + 1) * GW].astype(BF16))
        state_ref[g] = st * chunk_decay[:, g * GW:(g + 1) * GW] + new
        pairs = []
        for pr in range(GW // (2 * SSM_HEAD_DIM)):
            h0 = g * (GW // SSM_HEAD_DIM) + 2 * pr
            xp = xdt[:, h0 * SSM_HEAD_DIM:(h0 + 2) * SSM_HEAD_DIM].astype(BF16)
            ys = []
            for h in (h0, h0 + 1):
                diff = a_cum[:, h:h + 1] - a_cum_t[h:h + 1, :]
                m = (gmat * jnp.exp(jnp.where(causal, diff, NEG_INF))).astype(BF16)
                ys.append(_dot(m, xp))
            pairs.append(jnp.where(first_head, ys[0], ys[1]))
        y_groups.append(jnp.concatenate(pairs, axis=1) + y_off)
    y = jnp.concatenate(y_groups, axis=1) + xs * dskip_ref[...]
    y = y * _silu(z_ref[...])
    o_ref[...] = _rms(y, nw_ref[...]).astype(BF16)


def _ssd(p, tail, tail_t, conv_w, conv_b, dtb_row, dtb_col, alog_row, alog_col, dskip_e, expand, norm_w):
    L = p.shape[0]
    Q = SSM_CHUNK
    bc_w = 2 * SSM_GROUPS * SSM_STATE
    xs_blk = D_SSM // D_SSM
    bc_blk = (2 * D_SSM) // bc_w
    hb = Q // 8
    full = lambda shape: pl.BlockSpec(shape, lambda c: (0,) * len(shape))
    return pl.pallas_call(
        _ssd_kernel,
        grid=(L // Q,),
        in_specs=[
            pl.BlockSpec((Q, D_SSM), lambda c: (c, xs_blk)),
            pl.BlockSpec((Q, bc_w), lambda c: (c, bc_blk)),
            pl.BlockSpec((8, D_SSM), lambda c: (jnp.maximum(c * hb - 1, 0), xs_blk)),
            pl.BlockSpec((8, bc_w), lambda c: (jnp.maximum(c * hb - 1, 0), bc_blk)),
            pl.BlockSpec((Q, D_SSM), lambda c: (c, 0)),
            pl.BlockSpec((Q, TAIL), lambda c: (c, 0)),
            pl.BlockSpec((TAIL, Q), lambda c: (0, c)),
            full(conv_w.shape), full(conv_b.shape), full(dtb_row.shape), full(dtb_col.shape),
            full(alog_row.shape), full(alog_col.shape), full(dskip_e.shape), full(expand.shape),
            full(norm_w.shape),
        ],
        out_specs=pl.BlockSpec((Q, D_SSM), lambda c: (c, 0)),
        out_shape=jax.ShapeDtypeStruct((L, D_SSM), BF16),
        scratch_shapes=[
            pltpu.VMEM((SSM_GROUPS, SSM_STATE, D_SSM // SSM_GROUPS), F32),
            pltpu.VMEM((Q + 8, D_SSM), F32),
            pltpu.VMEM((Q + 8, bc_w), F32),
        ],
        compiler_params=_params(("arbitrary",), 48),
        name="ssd_scan",
    )(p, p, p, p, p, tail, tail_t, conv_w, conv_b, dtb_row, dtb_col, alog_row, alog_col, dskip_e, expand, norm_w)


def _stack_heads(q):
    return jnp.concatenate([q[:, r * HEAD_DIM:(r + 1) * HEAD_DIM] for r in range(Q_PER_KV)], axis=0).astype(BF16)


def _softmax_rows(s, valid):
    s = jnp.where(valid, s, NEG_INF)
    m = jnp.max(s, axis=-1, keepdims=True)
    p = jnp.where(valid, jnp.exp(s - m), 0.0)
    l = jnp.sum(p, axis=-1, keepdims=True)
    return p * jnp.where(l > 0.0, 1.0 / l, 0.0)


def _cmp_win_kernel(slopes_ref, q_ref, kc_ref, vc_ref, kw_ref, vw_ref, imp_w_ref, oc_ref, ow_ref, sel_ref, *,
                    n_cmp_rows, n_blk, n_sel, seq_len):
    g = pl.program_id(0)
    qb = pl.program_id(1)
    T = Q_BLOCK
    t0 = qb * T
    scale = HEAD_DIM ** -0.5
    q4 = _stack_heads(q_ref[...])
    t_col = t0 + lax.broadcasted_iota(jnp.int32, (T, 1), 0)

    kc = kc_ref[...].astype(BF16)
    vc = vc_ref[...].astype(BF16)
    s_all = _dot_nt(q4, kc)
    cmp_end = lax.broadcasted_iota(jnp.int32, (1, n_cmp_rows), 1) * CMP_STRIDE + (CMP_LEN - 1)
    dist = t_col - cmp_end
    valid = dist >= 0
    distf = dist.astype(F32)
    psum = jnp.zeros((T, n_cmp_rows), F32)
    for r in range(Q_PER_KV):
        s = s_all[r * T:(r + 1) * T] * scale - slopes_ref[g * Q_PER_KV + r] * distf
        p = _softmax_rows(s, valid)
        psum = psum + p
        oc_ref[:, r * HEAD_DIM:(r + 1) * HEAD_DIM] = _dot(p.astype(BF16), vc)

    imp = _dot3(psum, imp_w_ref[...])
    jb = lax.broadcasted_iota(jnp.int32, (1, n_blk), 1)
    jt = t_col // SEL_BLOCK
    forced = (jb == 0) | (jb == jt) | (jb == jt - 1)
    work = jnp.where(forced, FORCE_SCORE, jnp.where(jb <= jt, imp, -1.0))
    jbf = jb.astype(F32)
    sel = jnp.zeros((T, n_blk), F32)
    for _ in range(n_sel):
        m = jnp.max(work, axis=-1, keepdims=True)
        first = jnp.min(jnp.where(work == m, jbf, float(n_blk)), axis=-1, keepdims=True)
        hit = jbf == first
        sel = jnp.where(hit, 1.0, sel)
        work = jnp.where(hit, NEG_INF, work)
    sel_ref[...] = sel.astype(BF16)

    wlen = WINDOW + T
    start = pl.multiple_of(jnp.maximum(t0 - WINDOW, 0), T)
    kwin = kw_ref[pl.ds(start, wlen), :]
    vwin = vw_ref[pl.ds(start, wlen), :]
    s_all = _dot_nt(q4, kwin)
    dist = t_col - (start + lax.broadcasted_iota(jnp.int32, (1, wlen), 1))
    valid = (dist >= 0) & (dist < WINDOW)
    distf = dist.astype(F32)
    for r in range(Q_PER_KV):
        s = s_all[r * T:(r + 1) * T] * scale - slopes_ref[g * Q_PER_KV + r] * distf
        p = _softmax_rows(s, valid)
        ow_ref[:, r * HEAD_DIM:(r + 1) * HEAD_DIM] = _dot(p.astype(BF16), vwin)


def _cmp_win_attention(slopes, p, cmp, kv, imp_w, n_sel):
    L = p.shape[0]
    n_cmp_rows = L // CMP_STRIDE
    n_blk = L // SEL_BLOCK
    q_blk0 = (P_COLS - 2 * KV_DIM - D_ATTN) // KV_DIM
    kern = functools.partial(_cmp_win_kernel, n_cmp_rows=n_cmp_rows, n_blk=n_blk, n_sel=n_sel, seq_len=L)
    return pl.pallas_call(
        kern,
        grid=(N_KV_HEADS, L // Q_BLOCK),
        in_specs=[
            pl.BlockSpec(memory_space=pltpu.SMEM),
            pl.BlockSpec((Q_BLOCK, KV_DIM), lambda g, i: (i, q_blk0 + g)),
            pl.BlockSpec((None, None, n_cmp_rows, HEAD_DIM), lambda g, i: (0, g, 0, 0)),
            pl.BlockSpec((None, None, n_cmp_rows, HEAD_DIM), lambda g, i: (1, g, 0, 0)),
            pl.BlockSpec((None, L, HEAD_DIM), lambda g, i: (2 * N_KV_HEADS + g, 0, 0)),
            pl.BlockSpec((None, L, HEAD_DIM), lambda g, i: (3 * N_KV_HEADS + g, 0, 0)),
            pl.BlockSpec((n_cmp_rows, n_blk), lambda g, i: (0, 0)),
        ],
        out_specs=[
            pl.BlockSpec((Q_BLOCK, KV_DIM), lambda g, i: (i, g)),
            pl.BlockSpec((Q_BLOCK, KV_DIM), lambda g, i: (i, g)),
            pl.BlockSpec((None, Q_BLOCK, n_blk), lambda g, i: (g, i, 0)),
        ],
        out_shape=[
            jax.ShapeDtypeStruct((L, D_ATTN), F32),
            jax.ShapeDtypeStruct((L, D_ATTN), F32),
            jax.ShapeDtypeStruct((N_KV_HEADS, L, n_blk), BF16),
        ],
        compiler_params=_params(("arbitrary", "arbitrary"), 48),
        name="cmp_win_attention",
    )(slopes, p, cmp, cmp, kv, kv, imp_w)


SEL_TK = 512


def _sel_kernel(slopes_ref, q_ref, ks_ref, vs_ref, sel_ref, o_ref, m_ref, l_ref, acc_ref, *, n_blk):
    g = pl.program_id(0)
    qb = pl.program_id(1)
    T = Q_BLOCK
    R = Q_PER_KV
    TK = SEL_TK
    t0 = qb * T
    scale = HEAD_DIM ** -0.5
    q4 = _stack_heads(q_ref[...])
    t_col = t0 + lax.broadcasted_iota(jnp.int32, (T, 1), 0)
    sel = sel_ref[...]
    rel = (lax.broadcasted_iota(jnp.int32, (n_blk, TK), 0)
           - lax.broadcasted_iota(jnp.int32, (n_blk, TK), 1) // SEL_BLOCK)
    key_iota = lax.broadcasted_iota(jnp.int32, (1, TK), 1)

    m_ref[...] = jnp.full(m_ref.shape, NEG_INF, F32)
    l_ref[...] = jnp.zeros(l_ref.shape, F32)
    acc_ref[...] = jnp.zeros(acc_ref.shape, F32)

    def tile(kt, carry):
        k0 = pl.multiple_of(kt * TK, TK)
        ktile = ks_ref[pl.ds(k0, TK), :]
        vtile = vs_ref[pl.ds(k0, TK), :]
        s_all = _dot_nt(q4, ktile)
        spread = jnp.where(rel == kt * (TK // SEL_BLOCK), 1.0, 0.0).astype(BF16)
        chosen = _dot(sel, spread)
        dist = t_col - (k0 + key_iota)
        valid = (dist >= 0) & (chosen > 0.5)
        distf = dist.astype(F32)
        for r in range(R):
            rs = slice(r * T, (r + 1) * T)
            s = s_all[rs] * scale - slopes_ref[g * R + r] * distf
            s = jnp.where(valid, s, NEG_INF)
            m_old = m_ref[rs]
            m_new = jnp.maximum(m_old, jnp.max(s, axis=-1, keepdims=True))
            alpha = jnp.exp(m_old - m_new)
            p = jnp.where(valid, jnp.exp(s - m_new), 0.0)
            l_ref[rs] = alpha * l_ref[rs] + jnp.sum(p, axis=-1, keepdims=True)
            acc_ref[rs] = alpha * acc_ref[rs] + _dot(p.astype(BF16), vtile)
            m_ref[rs] = m_new
        return carry

    n_tiles = (t0 + T - 1) // TK + 1
    lax.fori_loop(0, n_tiles, tile, 0)
    for r in range(R):
        rs = slice(r * T, (r + 1) * T)
        l = l_ref[rs]
        o_ref[:, r * HEAD_DIM:(r + 1) * HEAD_DIM] = acc_ref[rs] * jnp.where(l > 0.0, 1.0 / l, 0.0)


def _sel_attention(slopes, p, kv, sel):
    L = p.shape[0]
    n_blk = L // SEL_BLOCK
    q_blk0 = (P_COLS - 2 * KV_DIM - D_ATTN) // KV_DIM
    rows = Q_PER_KV * Q_BLOCK
    return pl.pallas_call(
        functools.partial(_sel_kernel, n_blk=n_blk),
        grid=(N_KV_HEADS, L // Q_BLOCK),
        in_specs=[
            pl.BlockSpec(memory_space=pltpu.SMEM),
            pl.BlockSpec((Q_BLOCK, KV_DIM), lambda g, i: (i, q_blk0 + g)),
            pl.BlockSpec((None, L, HEAD_DIM), lambda g, i: (g, 0, 0)),
            pl.BlockSpec((None, L, HEAD_DIM), lambda g, i: (N_KV_HEADS + g, 0, 0)),
            pl.BlockSpec((None, Q_BLOCK, n_blk), lambda g, i: (g, i, 0)),
        ],
        out_specs=pl.BlockSpec((Q_BLOCK, KV_DIM), lambda g, i: (i, g)),
        out_shape=jax.ShapeDtypeStruct((L, D_ATTN), F32),
        scratch_shapes=[
            pltpu.VMEM((rows, 1), F32),
            pltpu.VMEM((rows, 1), F32),
            pltpu.VMEM((rows, HEAD_DIM), F32),
        ],
        compiler_params=_params(("arbitrary", "arbitrary"), 48),
        name="selected_attention",
    )(slopes, p, kv, kv, sel)


def _combine_kernel(oc_ref, os_ref, ow_ref, tail_ref, nw_ref, o_ref):
    gates = 1.0 / (1.0 + jnp.exp(-tail_ref[...]))
    parts = []
    for h in range(N_HEADS):
        cs = slice(h * HEAD_DIM, (h + 1) * HEAD_DIM)
        c0 = SSM_HEADS + h
        parts.append(gates[:, c0:c0 + 1] * oc_ref[:, cs]
                     + gates[:, c0 + N_HEADS:c0 + N_HEADS + 1] * os_ref[:, cs]
                     + gates[:, c0 + 2 * N_HEADS:c0 + 2 * N_HEADS + 1] * ow_ref[:, cs])
    o = jnp.concatenate(parts, axis=1)
    o_ref[...] = _rms(o, nw_ref[...]).astype(BF16)


def _combine(oc, osel, ow, tail, norm_w, tm):
    L = oc.shape[0]
    row = pl.BlockSpec((tm, D_ATTN), lambda i: (i, 0))
    return pl.pallas_call(
        _combine_kernel,
        grid=(L // tm,),
        in_specs=[row, row, row, pl.BlockSpec((tm, TAIL), lambda i: (i, 0)),
                  pl.BlockSpec((1, D_ATTN), lambda i: (0, 0))],
        out_specs=row,
        out_shape=jax.ShapeDtypeStruct((L, D_ATTN), BF16),
        compiler_params=_params(("arbitrary",), 48),
        name="gate_combine",
    )(oc, osel, ow, tail, norm_w)


def _outproj_kernel(x_ref, ya_ref, ys_ref, wa_ref, ws_ref, o_ref):
    o_ref[...] = x_ref[...] + _dot(ya_ref[...], wa_ref[...]) + _dot(ys_ref[...], ws_ref[...])


def _out_projection(x2, ya, ys, w_out, tm, tn):
    L = x2.shape[0]
    n_j = D_MODEL // tn
    return pl.pallas_call(
        _outproj_kernel,
        grid=(L // tm, n_j),
        in_specs=[
            pl.BlockSpec((tm, tn), lambda i, j: (i, j)),
            pl.BlockSpec((tm, D_ATTN), lambda i, j: (i, 0)),
            pl.BlockSpec((tm, D_SSM), lambda i, j: (i, 0)),
            pl.BlockSpec((D_ATTN, tn), lambda i, j: (0, j)),
            pl.BlockSpec((D_SSM, tn), lambda i, j: (1, j)),
        ],
        out_specs=pl.BlockSpec((tm, tn), lambda i, j: (i, j)),
        out_shape=jax.ShapeDtypeStruct((L, D_MODEL), F32),
        compiler_params=_params(("arbitrary", "arbitrary"), 48),
        name="out_projection",
    )(x2, ya, ys, w_out, w_out)


FFN_HALO = 16


def _ffn_kernel(x_ref, halo_ref, nw_ref, wg_ref, wv_ref, cwg_ref, cwv_ref, cbg_ref, cbv_ref, wd_ref, fw_ref,
                o_ref, hn_ref, ug_ref, uv_ref, acc_ref, *, tm):
    i = pl.program_id(0)
    j = pl.program_id(1)
    H = FFN_HALO

    @pl.when(j == 0)
    def _():
        hn_ref[H:H + tm, :] = _rms(x_ref[...], nw_ref[...]).astype(BF16)
        acc_ref[...] = jnp.zeros_like(acc_ref)

    @pl.when((j == 0) & (i == 0))
    def _():
        hn_ref[0:H, :] = jnp.zeros((H, D_MODEL), BF16)

    @pl.when((j == 0) & (i > 0))
    def _():
        hn_ref[0:H, :] = _rms(halo_ref[...], nw_ref[...]).astype(BF16)

    hn = hn_ref[...]
    ug_ref[...] = _dot(hn, wg_ref[...])
    uv_ref[...] = _dot(hn, wv_ref[...])

    def conv(u_ref, w_ref, b_ref):
        acc = b_ref[...]
        for k in range(FFN_CONV):
            acc = acc + u_ref[pl.ds(H - (FFN_CONV - 1) + k, tm), :] * w_ref[k:k + 1, :]
        return acc

    act = _silu(conv(ug_ref, cwg_ref, cbg_ref)) * conv(uv_ref, cwv_ref, cbv_ref)
    acc_ref[...] += _dot(act.astype(BF16), wd_ref[...])

    @pl.when(j == pl.num_programs(1) - 1)
    def _():
        o_ref[...] = _rms(x_ref[...] + acc_ref[...], fw_ref[...])


def _conv_ffn(x1, norm_w, w_up, conv_w, conv_b, w_down, final_w, tm, tf):
    L = x1.shape[0]
    n_f = D_FF // tf
    hb = tm // FFN_HALO
    return pl.pallas_call(
        functools.partial(_ffn_kernel, tm=tm),
        grid=(L // tm, n_f),
        in_specs=[
            pl.BlockSpec((tm, D_MODEL), lambda i, j: (i, 0)),
            pl.BlockSpec((FFN_HALO, D_MODEL), lambda i, j: (jnp.maximum(i * hb - 1, 0), 0)),
            pl.BlockSpec((1, D_MODEL), lambda i, j: (0, 0)),
            pl.BlockSpec((D_MODEL, tf), lambda i, j: (0, j)),
            pl.BlockSpec((D_MODEL, tf), lambda i, j: (0, n_f + j)),
            pl.BlockSpec((FFN_CONV, tf), lambda i, j: (0, j)),
            pl.BlockSpec((FFN_CONV, tf), lambda i, j: (0, n_f + j)),
            pl.BlockSpec((1, tf), lambda i, j: (0, j)),
            pl.BlockSpec((1, tf), lambda i, j: (0, n_f + j)),
            pl.BlockSpec((tf, D_MODEL), lambda i, j: (j, 0)),
            pl.BlockSpec((1, D_MODEL), lambda i, j: (0, 0)),
        ],
        out_specs=pl.BlockSpec((tm, D_MODEL), lambda i, j: (i, 0)),
        out_shape=jax.ShapeDtypeStruct((L, D_MODEL), F32),
        scratch_shapes=[
            pltpu.VMEM((tm + FFN_HALO, D_MODEL), BF16),
            pltpu.VMEM((tm + FFN_HALO, tf), F32),
            pltpu.VMEM((tm + FFN_HALO, tf), F32),
            pltpu.VMEM((tm, D_MODEL), F32),
        ],
        compiler_params=_params(("arbitrary", "arbitrary"), 56),
        name="conv_ffn",
    )(x1, x1, norm_w, w_up, w_up, conv_w, conv_w, conv_b, conv_b, w_down, final_w)


def _importance_matrix(n_cmp_rows, n_blk):
    a_r, b_r = SEL_BLOCK // CMP_STRIDE, CMP_LEN // CMP_STRIDE
    w = np.zeros((n_cmp_rows, n_blk), np.float32)
    for j in range(n_blk):
        for m in range(a_r):
            for n in range(b_r):
                idx = a_r * (j + 1) - 1 - (m + n)
                if 0 <= idx < n_cmp_rows:
                    w[idx, j] += 1.0
    return w


def _head_expand():
    e = np.zeros((TAIL, D_SSM), np.float32)
    for h in range(SSM_HEADS):
        e[h, h * SSM_HEAD_DIM:(h + 1) * SSM_HEAD_DIM] = 1.0
    return e


def _layer(x2, mix_norm_w, w_in, ssm_conv_w, ssm_conv_b, ssm_dt_bias, ssm_a_log, ssm_d, ssm_norm_w,
           cmp_pos, cmp_w1, cmp_w2, attn_norm_w, w_out, ffn_norm_w, w_up, ffn_conv_w, ffn_conv_b, w_down,
           final_norm_w, apply_final):
    L = x2.shape[0]
    row = lambda v: v.reshape(1, -1).astype(F32)

    bounds = np.cumsum([0, D_SSM, D_SSM + 2 * SSM_GROUPS * SSM_STATE, SSM_HEADS, D_ATTN] + [KV_DIM] * 6
                       + [3 * N_HEADS])
    seg = lambda k: w_in[:, bounds[k]:bounds[k + 1]]
    z_w, xbc_w, dt_w, q_w, kc_w, vc_w, ks_w, vs_w, kw_w, vw_w, gl_w = [seg(k) for k in range(11)]
    w_main = jnp.concatenate([z_w, xbc_w, q_w, kc_w, vc_w, ks_w, vs_w, kw_w, vw_w], axis=1).astype(BF16)
    w_tail = jnp.concatenate([dt_w, gl_w, jnp.zeros((D_MODEL, TAIL - SSM_HEADS - 3 * N_HEADS), F32)], axis=1)
    w_tail = w_tail.astype(BF16)
    pad_row = lambda v: jnp.concatenate([v.astype(F32), jnp.zeros((TAIL - SSM_HEADS,), F32)]).reshape(1, TAIL)

    p, kv, tail, tail_t = _in_projection(x2, row(mix_norm_w), w_main, w_tail, w_tail.T, tm=min(1024, L))
    cmp = _compress(p, cmp_pos, cmp_w1.astype(BF16), cmp_w2.astype(BF16))

    y_ssm = _ssd(p, tail, tail_t, ssm_conv_w.astype(F32), row(ssm_conv_b), pad_row(ssm_dt_bias),
                 ssm_dt_bias.astype(F32).reshape(SSM_HEADS, 1), pad_row(ssm_a_log),
                 ssm_a_log.astype(F32).reshape(SSM_HEADS, 1),
                 row(jnp.repeat(ssm_d.astype(F32), SSM_HEAD_DIM)), jnp.asarray(_head_expand(), BF16),
                 row(ssm_norm_w))

    n_blk = L // SEL_BLOCK
    slopes = 2.0 ** (-8.0 * jnp.arange(1, N_HEADS + 1, dtype=F32) / N_HEADS)
    imp_w = jnp.asarray(_importance_matrix(L // CMP_STRIDE, n_blk), BF16)
    o_c, o_w, sel = _cmp_win_attention(slopes, p, cmp, kv, imp_w, min(N_SEL, n_blk))
    o_s = _sel_attention(slopes, p, kv, sel)
    y_attn = _combine(o_c, o_s, o_w, tail, row(attn_norm_w), tm=256)

    x1 = _out_projection(x2, y_attn, y_ssm, w_out.astype(BF16), tm=512, tn=1024)
    return _conv_ffn(x1, row(ffn_norm_w), w_up.astype(BF16), ffn_conv_w.astype(F32), row(ffn_conv_b),
                     w_down.astype(BF16), row(final_norm_w), tm=512, tf=512)


def kernel(x, mix_norm_w, w_in, ssm_conv_w, ssm_conv_b, ssm_dt_bias, ssm_a_log, ssm_d, ssm_norm_w, cmp_pos_k,
           cmp_w1_k, cmp_w2_k, cmp_pos_v, cmp_w1_v, cmp_w2_v, attn_norm_w, w_out, ffn_norm_w, w_up, ffn_conv_w,
           ffn_conv_b, w_down, final_norm_w):
    bsz, L, _ = x.shape
    assert bsz == 1 and mix_norm_w.shape[0] == 1, "single sequence, single layer"
    assert L % 1024 == 0 and L >= WINDOW + Q_BLOCK
    out = _layer(
        x[0], mix_norm_w[0], w_in[0], ssm_conv_w[0], ssm_conv_b[0], ssm_dt_bias[0], ssm_a_log[0], ssm_d[0],
        ssm_norm_w[0], jnp.stack([cmp_pos_k[0], cmp_pos_v[0]]), jnp.stack([cmp_w1_k[0], cmp_w1_v[0]]),
        jnp.stack([cmp_w2_k[0], cmp_w2_v[0]]), attn_norm_w[0], w_out[0], ffn_norm_w[0], w_up[0], ffn_conv_w[0],
        ffn_conv_b[0], w_down[0], final_norm_w, True)
    return out[None]
```

```python
import functools

import numpy as np
import jax
import jax.numpy as jnp
from jax import lax
from jax.experimental import pallas as pl
from jax.experimental.pallas import tpu as pltpu

F32 = jnp.float32
BF16 = jnp.bfloat16

D_MODEL = 2048
D_ATTN = 2048
D_SSM = 2048
N_HEADS = 16
HEAD_DIM = 128
N_KV_HEADS = 4
Q_PER_KV = 4
KV_DIM = 512
CMP_LEN = 32
CMP_STRIDE = 16
CMP_HIDDEN = 256
SEL_BLOCK = 64
N_SEL = 16
WINDOW = 512
Q_BLOCK = 128
SSM_HEAD_DIM = 64
SSM_HEADS = 32
SSM_GROUPS = 4
SSM_STATE = 128
SSM_CONV = 4
SSM_CHUNK = 256
D_FF = 5632
FFN_CONV = 3
NORM_EPS = 1e-6
NEG_INF = -1e30
FORCE_SCORE = 1e4

P_COLS = 8192
KV_COLS = 2048
TAIL = 128
PROJ_TN = 1024
MIB = 1024 * 1024


def _params(sem, vmem_mib):
    return pltpu.CompilerParams(dimension_semantics=sem, vmem_limit_bytes=vmem_mib * MIB)


def _rms(x, w):
    return x * lax.rsqrt(jnp.mean(x * x, axis=-1, keepdims=True) + NORM_EPS) * w


def _dot(a, b):
    return jnp.dot(a, b, preferred_element_type=F32)


def _dot_nt(a, b):
    return lax.dot_general(a, b, (((1,), (1,)), ((), ())), preferred_element_type=F32)


def _split3(x):
    hi = x.astype(BF16)
    r1 = x - hi.astype(F32)
    mid = r1.astype(BF16)
    lo = (r1 - mid.astype(F32)).astype(BF16)
    return hi, mid, lo


def _dot3(x, m):
    hi, mid, lo = _split3(x)
    return _dot(hi, m) + _dot(mid, m) + _dot(lo, m)


def _dot3_left(m, x):
    hi, mid, lo = _split3(x)
    return _dot(m, hi) + _dot(m, mid) + _dot(m, lo)


def _silu(x):
    return x * (1.0 / (1.0 + jnp.exp(-x)))


def _inproj_kernel(x_ref, nw_ref, w_ref, wt_ref, wtt_ref, p_ref, kv_ref, tail_ref, tailt_ref, xn_ref, *, n_p):
    j = pl.program_id(1)

    @pl.when(j == 0)
    def _():
        xn = _rms(x_ref[...], nw_ref[...]).astype(BF16)
        xn_ref[...] = xn
        tail_ref[...] = _dot(xn, wt_ref[...])
        tailt_ref[...] = _dot_nt(wtt_ref[...], xn)

    r = _dot(xn_ref[...], w_ref[...])

    @pl.when(j < n_p)
    def _():
        p_ref[...] = r

    @pl.when(j >= n_p)
    def _():
        for c in range(PROJ_TN // HEAD_DIM):
            kv_ref[c] = r[:, c * HEAD_DIM:(c + 1) * HEAD_DIM].astype(BF16)


def _in_projection(x2, norm_w, w_main, w_tail, w_tail_t, tm):
    L = x2.shape[0]
    n_p = P_COLS // PROJ_TN
    n_kv = KV_COLS // PROJ_TN
    hp = PROJ_TN // HEAD_DIM
    return pl.pallas_call(
        functools.partial(_inproj_kernel, n_p=n_p),
        grid=(L // tm, n_p + n_kv),
        in_specs=[
            pl.BlockSpec((tm, D_MODEL), lambda i, j: (i, 0)),
            pl.BlockSpec((1, D_MODEL), lambda i, j: (0, 0)),
            pl.BlockSpec((D_MODEL, PROJ_TN), lambda i, j: (0, j)),
            pl.BlockSpec((D_MODEL, TAIL), lambda i, j: (0, 0)),
            pl.BlockSpec((TAIL, D_MODEL), lambda i, j: (0, 0)),
        ],
        out_specs=[
            pl.BlockSpec((tm, PROJ_TN), lambda i, j: (i, jnp.minimum(j, n_p - 1))),
            pl.BlockSpec((hp, tm, HEAD_DIM), lambda i, j: (jnp.maximum(j - n_p, 0), i, 0)),
            pl.BlockSpec((tm, TAIL), lambda i, j: (i, 0)),
            pl.BlockSpec((TAIL, tm), lambda i, j: (0, i)),
        ],
        out_shape=[
            jax.ShapeDtypeStruct((L, P_COLS), F32),
            jax.ShapeDtypeStruct((KV_COLS // HEAD_DIM, L, HEAD_DIM), BF16),
            jax.ShapeDtypeStruct((L, TAIL), F32),
            jax.ShapeDtypeStruct((TAIL, L), F32),
        ],
        scratch_shapes=[pltpu.VMEM((tm, D_MODEL), BF16)],
        compiler_params=_params(("arbitrary", "arbitrary"), 56),
        name="in_projection",
    )(x2, norm_w, w_main, w_tail, w_tail_t)


def _compress_kernel(x_ref, pos_ref, w1_ref, w2_ref, o_ref, *, n_rows):
    half = CMP_LEN // 2
    acc0 = jnp.zeros((n_rows, CMP_HIDDEN), F32)
    acc1 = jnp.zeros((n_rows, CMP_HIDDEN), F32)
    for i in range(half):
        xi = x_ref[pl.ds(i, n_rows, stride=CMP_STRIDE), :]
        a0 = (xi + pos_ref[i:i + 1, :]).astype(BF16)
        a1 = (xi + pos_ref[half + i:half + i + 1, :]).astype(BF16)
        acc0 = acc0 + _dot(a0, w1_ref[i * HEAD_DIM:(i + 1) * HEAD_DIM, :])
        acc1 = acc1 + _dot(a1, w1_ref[(half + i) * HEAD_DIM:(half + i + 1) * HEAD_DIM, :])
    hid = acc0 + pltpu.roll(acc1, n_rows - 1, 0)
    o_ref[...] = _dot(jax.nn.gelu(hid).astype(BF16), w2_ref[...])


def _compress(p, pos, w1, w2):
    L = p.shape[0]
    n_rows = L // CMP_STRIDE
    col0 = (P_COLS - 2 * KV_DIM) // HEAD_DIM
    return pl.pallas_call(
        functools.partial(_compress_kernel, n_rows=n_rows),
        grid=(2, N_KV_HEADS),
        in_specs=[
            pl.BlockSpec((L, HEAD_DIM), lambda s, g: (0, col0 + N_KV_HEADS * s + g)),
            pl.BlockSpec((None, CMP_LEN, HEAD_DIM), lambda s, g: (s, 0, 0)),
            pl.BlockSpec((None, CMP_LEN * HEAD_DIM, CMP_HIDDEN), lambda s, g: (s, 0, 0)),
            pl.BlockSpec((None, CMP_HIDDEN, HEAD_DIM), lambda s, g: (s, 0, 0)),
        ],
        out_specs=pl.BlockSpec((None, None, n_rows, HEAD_DIM), lambda s, g: (s, g, 0, 0)),
        out_shape=jax.ShapeDtypeStruct((2, N_KV_HEADS, n_rows, HEAD_DIM), F32),
        compiler_params=_params(("arbitrary", "arbitrary"), 48),
        name="kv_compress",
    )(p, pos, w1, w2)


def _ssd_kernel(xs_ref, bc_ref, xs_halo_ref, bc_halo_ref, z_ref, tail_ref, tailt_ref, cw_ref, cb_ref,
                dtb_row_ref, dtb_col_ref, alog_row_ref, alog_col_ref, dskip_ref, expand_ref, nw_ref,
                o_ref, state_ref, cx_ref, cbc_ref):
    c = pl.program_id(0)
    Q = SSM_CHUNK
    GW = D_SSM // SSM_GROUPS
    halo = 8

    @pl.when(c == 0)
    def _():
        state_ref[...] = jnp.zeros_like(state_ref)
        cx_ref[0:halo, :] = jnp.zeros((halo, D_SSM), F32)
        cbc_ref[0:halo, :] = jnp.zeros((halo, 2 * SSM_GROUPS * SSM_STATE), F32)

    @pl.when(c > 0)
    def _():
        cx_ref[0:halo, :] = xs_halo_ref[...]
        cbc_ref[0:halo, :] = bc_halo_ref[...]

    cx_ref[halo:halo + Q, :] = xs_ref[...]
    cbc_ref[halo:halo + Q, :] = bc_ref[...]

    def conv_silu(ref, w, b):
        acc = b
        for k in range(SSM_CONV):
            acc = acc + ref[pl.ds(halo - (SSM_CONV - 1) + k, Q), :] * w[k:k + 1, :]
        return _silu(acc)

    cw = cw_ref[...]
    cb = cb_ref[...]
    xs = conv_silu(cx_ref, cw[:, :D_SSM], cb[:, :D_SSM])
    bcm = conv_silu(cbc_ref, cw[:, D_SSM:], cb[:, D_SSM:])

    def softplus(v):
        return jnp.maximum(v, 0.0) + jnp.log1p(jnp.exp(-jnp.abs(v)))

    dt = softplus(tail_ref[...] + dtb_row_ref[...])
    a = dt * (-jnp.exp(alog_row_ref[...]))
    rows = lax.broadcasted_iota(jnp.int32, (Q, Q), 0)
    cols = lax.broadcasted_iota(jnp.int32, (Q, Q), 1)
    causal = cols <= rows
    tri = jnp.where(causal, 1.0, 0.0).astype(BF16)
    a_cum = _dot3_left(tri, a)
    dt_t = softplus(tailt_ref[0:SSM_HEADS, :] + dtb_col_ref[...])
    a_t = dt_t * (-jnp.exp(alog_col_ref[...]))
    tri_t = jnp.where(rows <= cols, 1.0, 0.0).astype(BF16)
    a_cum_t = _dot3(a_t, tri_t)

    expand = expand_ref[...]
    dt_e = _dot3(dt, expand)
    ac_e = _dot3(a_cum, expand)
    xdt = xs * dt_e
    decay_to = jnp.exp(ac_e)
    a_last = ac_e[Q - 1:Q, :]
    xdd = xdt * jnp.exp(a_last - ac_e)
    chunk_decay = jnp.exp(a_last)

    lane = lax.broadcasted_iota(jnp.int32, (Q, 2 * SSM_HEAD_DIM), 1)
    first_head = lane < SSM_HEAD_DIM
    y_groups = []
    for g in range(SSM_GROUPS):
        bg = bcm[:, g * SSM_STATE:(g + 1) * SSM_STATE]
        cg = bcm[:, (SSM_GROUPS + g) * SSM_STATE:(SSM_GROUPS + g + 1) * SSM_STATE].astype(BF16)
        gmat = _dot_nt(cg, bg.astype(BF16))
        st = state_ref[g]
        y_off = _dot(cg, st.astype(BF16)) * decay_to[:, g * GW:(g + 1) * GW]
        new = _dot(bg.T.astype(BF16), xdd[:, g * GW:(g + 1) * GW].astype(BF16))
        state_ref[g] = st * chunk_decay[:, g * GW:(g + 1) * GW] + new
        pairs = []
        for pr in range(GW // (2 * SSM_HEAD_DIM)):
            h0 = g * (GW // SSM_HEAD_DIM) + 2 * pr
            xp = xdt[:, h0 * SSM_HEAD_DIM:(h0 + 2) * SSM_HEAD_DIM].astype(BF16)
            ys = []
            for h in (h0, h0 + 1):
                diff = a_cum[:, h:h + 1] - a_cum_t[h:h + 1, :]
                m = (gmat * jnp.exp(jnp.where(causal, diff, NEG_INF))).astype(BF16)
                ys.append(_dot(m, xp))
            pairs.append(jnp.where(first_head, ys[0], ys[1]))
        y_groups.append(jnp.concatenate(pairs, axis=1) + y_off)
    y = jnp.concatenate(y_groups, axis=1) + xs * dskip_ref[...]
    y = y * _silu(z_ref[...])
    o_ref[...] = _rms(y, nw_ref[...]).astype(BF16)


def _ssd(p, tail, tail_t, conv_w, conv_b, dtb_row, dtb_col, alog_row, alog_col, dskip_e, expand, norm_w):
    L = p.shape[0]
    Q = SSM_CHUNK
    bc_w = 2 * SSM_GROUPS * SSM_STATE
    xs_blk = D_SSM // D_SSM
    bc_blk = (2 * D_SSM) // bc_w
    hb = Q // 8
    full = lambda shape: pl.BlockSpec(shape, lambda c: (0,) * len(shape))
    return pl.pallas_call(
        _ssd_kernel,
        grid=(L // Q,),
        in_specs=[
            pl.BlockSpec((Q, D_SSM), lambda c: (c, xs_blk)),
            pl.BlockSpec((Q, bc_w), lambda c: (c, bc_blk)),
            pl.BlockSpec((8, D_SSM), lambda c: (jnp.maximum(c * hb - 1, 0), xs_blk)),
            pl.BlockSpec((8, bc_w), lambda c: (jnp.maximum(c * hb - 1, 0), bc_blk)),
            pl.BlockSpec((Q, D_SSM), lambda c: (c, 0)),
            pl.BlockSpec((Q, TAIL), lambda c: (c, 0)),
            pl.BlockSpec((TAIL, Q), lambda c: (0, c)),
            full(conv_w.shape), full(conv_b.shape), full(dtb_row.shape), full(dtb_col.shape),
            full(alog_row.shape), full(alog_col.shape), full(dskip_e.shape), full(expand.shape),
            full(norm_w.shape),
        ],
        out_specs=pl.BlockSpec((Q, D_SSM), lambda c: (c, 0)),
        out_shape=jax.ShapeDtypeStruct((L, D_SSM), BF16),
        scratch_shapes=[
            pltpu.VMEM((SSM_GROUPS, SSM_STATE, D_SSM // SSM_GROUPS), F32),
            pltpu.VMEM((Q + 8, D_SSM), F32),
            pltpu.VMEM((Q + 8, bc_w), F32),
        ],
        compiler_params=_params(("arbitrary",), 48),
        name="ssd_scan",
    )(p, p, p, p, p, tail, tail_t, conv_w, conv_b, dtb_row, dtb_col, alog_row, alog_col, dskip_e, expand, norm_w)


def _stack_heads(q):
    return jnp.concatenate([q[:, r * HEAD_DIM:(r + 1) * HEAD_DIM] for r in range(Q_PER_KV)], axis=0).astype(BF16)


LOG2E = 1.4426950408889634
M_INIT = -1e20


def _softmax_rows(v, valid):
    v = jnp.where(valid, v, NEG_INF)
    m = jnp.maximum(jnp.max(v, axis=-1, keepdims=True), M_INIT)
    p = jnp.exp2(v - m)
    l = jnp.sum(p, axis=-1, keepdims=True)
    return p * jnp.where(l > 0.0, 1.0 / l, 0.0)


def _cmp_win_kernel(slopes_ref, q_ref, kc_ref, vc_ref, kw_ref, vw_ref, imp_w_ref, oc_ref, ow_ref, sel_ref, any_ref, *,
                    n_cmp_rows, n_blk, n_sel, seq_len):
    g = pl.program_id(0)
    qb = pl.program_id(1)
    T = Q_BLOCK
    t0 = qb * T
    q4 = _stack_heads(q_ref[...] * (HEAD_DIM ** -0.5 * LOG2E))
    t_col = t0 + lax.broadcasted_iota(jnp.int32, (T, 1), 0)

    kc = kc_ref[...].astype(BF16)
    vc = vc_ref[...].astype(BF16)
    s_all = _dot_nt(q4, kc)
    cmp_end = lax.broadcasted_iota(jnp.int32, (1, n_cmp_rows), 1) * CMP_STRIDE + (CMP_LEN - 1)
    valid = t_col >= cmp_end
    key_rel = (cmp_end - t0).astype(F32)
    psum = jnp.zeros((T, n_cmp_rows), F32)
    for r in range(Q_PER_KV):
        sl = slopes_ref[g * Q_PER_KV + r] * LOG2E
        p = _softmax_rows(s_all[r * T:(r + 1) * T] + sl * key_rel, valid)
        psum = psum + p
        oc_ref[:, r * HEAD_DIM:(r + 1) * HEAD_DIM] = _dot(p.astype(BF16), vc)

    imp = _dot3(psum, imp_w_ref[...])
    jb = lax.broadcasted_iota(jnp.int32, (1, n_blk), 1)
    jt = t_col // SEL_BLOCK
    forced = (jb == 0) | (jb == jt) | (jb == jt - 1)
    work = jnp.where(forced, FORCE_SCORE, jnp.where(jb <= jt, imp, -1.0))
    jbf = jb.astype(F32)
    sel = jnp.zeros((T, n_blk), F32)
    for _ in range(n_sel):
        m = jnp.max(work, axis=-1, keepdims=True)
        first = jnp.min(jnp.where(work == m, jbf, float(n_blk)), axis=-1, keepdims=True)
        hit = jbf == first
        sel = jnp.where(hit, 1.0, sel)
        work = jnp.where(hit, NEG_INF, work)
    sel_ref[...] = sel.astype(BF16)
    any_ref[...] = jnp.max(sel, axis=0, keepdims=True)

    wlen = WINDOW + T
    start = pl.multiple_of(jnp.maximum(t0 - WINDOW, 0), T)
    kwin = kw_ref[pl.ds(start, wlen), :]
    vwin = vw_ref[pl.ds(start, wlen), :]
    s_all = _dot_nt(q4, kwin)
    key_iota = lax.broadcasted_iota(jnp.int32, (1, wlen), 1)
    dist = t_col - (start + key_iota)
    valid = (dist >= 0) & (dist < WINDOW)
    key_rel = key_iota.astype(F32)
    for r in range(Q_PER_KV):
        sl = slopes_ref[g * Q_PER_KV + r] * LOG2E
        p = _softmax_rows(s_all[r * T:(r + 1) * T] + sl * key_rel, valid)
        ow_ref[:, r * HEAD_DIM:(r + 1) * HEAD_DIM] = _dot(p.astype(BF16), vwin)


def _cmp_win_attention(slopes, p, cmp, kv, imp_w, n_sel):
    L = p.shape[0]
    n_cmp_rows = L // CMP_STRIDE
    n_blk = L // SEL_BLOCK
    q_blk0 = (P_COLS - 2 * KV_DIM - D_ATTN) // KV_DIM
    kern = functools.partial(_cmp_win_kernel, n_cmp_rows=n_cmp_rows, n_blk=n_blk, n_sel=n_sel, seq_len=L)
    return pl.pallas_call(
        kern,
        grid=(N_KV_HEADS, L // Q_BLOCK),
        in_specs=[
            pl.BlockSpec(memory_space=pltpu.SMEM),
            pl.BlockSpec((Q_BLOCK, KV_DIM), lambda g, i: (i, q_blk0 + g)),
            pl.BlockSpec((None, None, n_cmp_rows, HEAD_DIM), lambda g, i: (0, g, 0, 0)),
            pl.BlockSpec((None, None, n_cmp_rows, HEAD_DIM), lambda g, i: (1, g, 0, 0)),
            pl.BlockSpec((None, L, HEAD_DIM), lambda g, i: (2 * N_KV_HEADS + g, 0, 0)),
            pl.BlockSpec((None, L, HEAD_DIM), lambda g, i: (3 * N_KV_HEADS + g, 0, 0)),
            pl.BlockSpec((n_cmp_rows, n_blk), lambda g, i: (0, 0)),
        ],
        out_specs=[
            pl.BlockSpec((Q_BLOCK, KV_DIM), lambda g, i: (i, g)),
            pl.BlockSpec((Q_BLOCK, KV_DIM), lambda g, i: (i, g)),
            pl.BlockSpec((None, Q_BLOCK, n_blk), lambda g, i: (g, i, 0)),
            pl.BlockSpec((None, None, 1, n_blk), lambda g, i: (g, i, 0, 0)),
        ],
        out_shape=[
            jax.ShapeDtypeStruct((L, D_ATTN), F32),
            jax.ShapeDtypeStruct((L, D_ATTN), F32),
            jax.ShapeDtypeStruct((N_KV_HEADS, L, n_blk), BF16),
            jax.ShapeDtypeStruct((N_KV_HEADS, L // Q_BLOCK, 1, n_blk), F32),
        ],
        compiler_params=_params(("arbitrary", "arbitrary"), 48),
        name="cmp_win_attention",
    )(slopes, p, cmp, cmp, kv, kv, imp_w)


SEL_TK = 512


def _sel_kernel(tiles_ref, counts_ref, slopes_ref, q_ref, ks_ref, vs_ref, sel_ref, o_ref, m_ref, l_ref, acc_ref, *,
                n_blk, n_tiles_max):
    g = pl.program_id(0)
    qb = pl.program_id(1)
    T = Q_BLOCK
    R = Q_PER_KV
    TK = SEL_TK
    bpt = TK // SEL_BLOCK
    t0 = qb * T
    q4 = _stack_heads(q_ref[...] * (HEAD_DIM ** -0.5 * LOG2E))
    t_col = t0 + lax.broadcasted_iota(jnp.int32, (T, 1), 0)
    sel = sel_ref[...].astype(F32)
    spread8 = jnp.where(lax.broadcasted_iota(jnp.int32, (bpt, TK), 0)
                        == lax.broadcasted_iota(jnp.int32, (bpt, TK), 1) // SEL_BLOCK, 1.0, 0.0).astype(BF16)
    key_iota = lax.broadcasted_iota(jnp.int32, (1, TK), 1)
    head = lax.broadcasted_iota(jnp.int32, (R, 1, 1), 0)
    sl3 = jnp.zeros((R, 1, 1), F32)
    for r in range(R):
        sl3 = jnp.where(head == r, slopes_ref[g * R + r] * LOG2E, sl3)
    bias3 = sl3 * key_iota.astype(F32)[None]

    m_ref[...] = jnp.full(m_ref.shape, M_INIT, F32)
    l_ref[...] = jnp.zeros(l_ref.shape, F32)
    acc_ref[...] = jnp.zeros(acc_ref.shape, F32)
    step = g * pl.num_programs(1) + qb

    def tile(n, carry):
        kt = tiles_ref[step * n_tiles_max + n]
        k0 = pl.multiple_of(kt * TK, TK)
        ktile = ks_ref[pl.ds(k0, TK), :]
        vtile = vs_ref[pl.ds(k0, TK), :]
        s3 = _dot_nt(q4, ktile).reshape(R, T, TK)
        sel_t = pltpu.roll(sel, (n_blk - kt * bpt) % n_blk, 1)[:, :bpt].astype(BF16)
        chosen = _dot(sel_t, spread8)
        valid = (chosen > 0.5) & (t_col >= k0 + key_iota)
        col3 = sl3 * (t_col - k0).astype(F32)[None]
        v = jnp.where(valid[None], s3 + bias3, NEG_INF)
        m_old = m_ref[...]
        m_new = jnp.maximum(m_old, jnp.max(v, axis=-1, keepdims=True) - col3)
        alpha = jnp.exp2(m_old - m_new)
        p = jnp.exp2(v - (m_new + col3))
        l_ref[...] = alpha * l_ref[...] + jnp.sum(p, axis=-1, keepdims=True)
        pv = _dot(p.reshape(R * T, TK).astype(BF16), vtile)
        acc_ref[...] = alpha.reshape(R * T, 1) * acc_ref[...] + pv
        m_ref[...] = m_new
        return carry

    lax.fori_loop(0, counts_ref[step], tile, 0)
    l = l_ref[...].reshape(R * T, 1)
    o = acc_ref[...] * jnp.where(l > 0.0, 1.0 / l, 0.0)
    for r in range(R):
        o_ref[:, r * HEAD_DIM:(r + 1) * HEAD_DIM] = o[r * T:(r + 1) * T]


def _active_tiles(blk_any, L):
    G, nqb = blk_any.shape[:2]
    n_t = L // SEL_TK
    flags = blk_any.reshape(G, nqb, n_t, SEL_TK // SEL_BLOCK).max(axis=-1) > 0.5
    kt = jnp.arange(n_t, dtype=jnp.int32)
    last = (jnp.arange(nqb, dtype=jnp.int32) * Q_BLOCK + Q_BLOCK - 1) // SEL_TK
    flags = flags & (kt[None, None, :] <= last[None, :, None])
    order = jnp.sort(jnp.where(flags, kt, n_t + kt), axis=-1)
    tiles = jnp.where(order < n_t, order, 0).astype(jnp.int32)
    return tiles.reshape(-1), flags.sum(axis=-1).astype(jnp.int32).reshape(-1)


def _sel_attention(slopes, p, kv, sel, blk_any):
    L = p.shape[0]
    n_blk = L // SEL_BLOCK
    n_t = L // SEL_TK
    q_blk0 = (P_COLS - 2 * KV_DIM - D_ATTN) // KV_DIM
    rows = Q_PER_KV * Q_BLOCK
    tiles, counts = _active_tiles(blk_any, L)
    grid_spec = pltpu.PrefetchScalarGridSpec(
        num_scalar_prefetch=2,
        grid=(N_KV_HEADS, L // Q_BLOCK),
        in_specs=[
            pl.BlockSpec(memory_space=pltpu.SMEM),
            pl.BlockSpec((Q_BLOCK, KV_DIM), lambda g, i, *_: (i, q_blk0 + g)),
            pl.BlockSpec((None, L, HEAD_DIM), lambda g, i, *_: (g, 0, 0)),
            pl.BlockSpec((None, L, HEAD_DIM), lambda g, i, *_: (N_KV_HEADS + g, 0, 0)),
            pl.BlockSpec((None, Q_BLOCK, n_blk), lambda g, i, *_: (g, i, 0)),
        ],
        out_specs=pl.BlockSpec((Q_BLOCK, KV_DIM), lambda g, i, *_: (i, g)),
        scratch_shapes=[
            pltpu.VMEM((Q_PER_KV, Q_BLOCK, 1), F32),
            pltpu.VMEM((Q_PER_KV, Q_BLOCK, 1), F32),
            pltpu.VMEM((rows, HEAD_DIM), F32),
        ],
    )
    return pl.pallas_call(
        functools.partial(_sel_kernel, n_blk=n_blk, n_tiles_max=n_t),
        grid_spec=grid_spec,
        out_shape=jax.ShapeDtypeStruct((L, D_ATTN), F32),
        compiler_params=_params(("arbitrary", "arbitrary"), 48),
        name="selected_attention",
    )(tiles, counts, slopes, p, kv, kv, sel)


def _combine_kernel(oc_ref, os_ref, ow_ref, tail_ref, nw_ref, o_ref):
    gates = 1.0 / (1.0 + jnp.exp(-tail_ref[...]))
    parts = []
    for h in range(N_HEADS):
        cs = slice(h * HEAD_DIM, (h + 1) * HEAD_DIM)
        c0 = SSM_HEADS + h
        parts.append(gates[:, c0:c0 + 1] * oc_ref[:, cs]
                     + gates[:, c0 + N_HEADS:c0 + N_HEADS + 1] * os_ref[:, cs]
                     + gates[:, c0 + 2 * N_HEADS:c0 + 2 * N_HEADS + 1] * ow_ref[:, cs])
    o = jnp.concatenate(parts, axis=1)
    o_ref[...] = _rms(o, nw_ref[...]).astype(BF16)


def _combine(oc, osel, ow, tail, norm_w, tm):
    L = oc.shape[0]
    row = pl.BlockSpec((tm, D_ATTN), lambda i: (i, 0))
    return pl.pallas_call(
        _combine_kernel,
        grid=(L // tm,),
        in_specs=[row, row, row, pl.BlockSpec((tm, TAIL), lambda i: (i, 0)),
                  pl.BlockSpec((1, D_ATTN), lambda i: (0, 0))],
        out_specs=row,
        out_shape=jax.ShapeDtypeStruct((L, D_ATTN), BF16),
        compiler_params=_params(("arbitrary",), 48),
        name="gate_combine",
    )(oc, osel, ow, tail, norm_w)


def _outproj_kernel(x_ref, ya_ref, ys_ref, wa_ref, ws_ref, o_ref):
    o_ref[...] = x_ref[...] + _dot(ya_ref[...], wa_ref[...]) + _dot(ys_ref[...], ws_ref[...])


def _out_projection(x2, ya, ys, w_out, tm, tn):
    L = x2.shape[0]
    n_j = D_MODEL // tn
    return pl.pallas_call(
        _outproj_kernel,
        grid=(L // tm, n_j),
        in_specs=[
            pl.BlockSpec((tm, tn), lambda i, j: (i, j)),
            pl.BlockSpec((tm, D_ATTN), lambda i, j: (i, 0)),
            pl.BlockSpec((tm, D_SSM), lambda i, j: (i, 0)),
            pl.BlockSpec((D_ATTN, tn), lambda i, j: (0, j)),
            pl.BlockSpec((D_SSM, tn), lambda i, j: (1, j)),
        ],
        out_specs=pl.BlockSpec((tm, tn), lambda i, j: (i, j)),
        out_shape=jax.ShapeDtypeStruct((L, D_MODEL), F32),
        compiler_params=_params(("arbitrary", "arbitrary"), 48),
        name="out_projection",
    )(x2, ya, ys, w_out, w_out)


FFN_HALO = 16


def _ffn_kernel(x_ref, halo_ref, nw_ref, wg_ref, wv_ref, cwg_ref, cwv_ref, cbg_ref, cbv_ref, wd_ref, fw_ref,
                o_ref, hn_ref, ug_ref, uv_ref, acc_ref, *, tm):
    i = pl.program_id(0)
    j = pl.program_id(1)
    H = FFN_HALO

    @pl.when(j == 0)
    def _():
        hn_ref[H:H + tm, :] = _rms(x_ref[...], nw_ref[...]).astype(BF16)
        acc_ref[...] = jnp.zeros_like(acc_ref)

    @pl.when((j == 0) & (i == 0))
    def _():
        hn_ref[0:H, :] = jnp.zeros((H, D_MODEL), BF16)

    @pl.when((j == 0) & (i > 0))
    def _():
        hn_ref[0:H, :] = _rms(halo_ref[...], nw_ref[...]).astype(BF16)

    hn = hn_ref[...]
    ug_ref[...] = _dot(hn, wg_ref[...])
    uv_ref[...] = _dot(hn, wv_ref[...])

    def conv(u_ref, w_ref, b_ref):
        acc = b_ref[...]
        for k in range(FFN_CONV):
            acc = acc + u_ref[pl.ds(H - (FFN_CONV - 1) + k, tm), :] * w_ref[k:k + 1, :]
        return acc

    act = _silu(conv(ug_ref, cwg_ref, cbg_ref)) * conv(uv_ref, cwv_ref, cbv_ref)
    acc_ref[...] += _dot(act.astype(BF16), wd_ref[...])

    @pl.when(j == pl.num_programs(1) - 1)
    def _():
        o_ref[...] = _rms(x_ref[...] + acc_ref[...], fw_ref[...])


def _conv_ffn(x1, norm_w, w_up, conv_w, conv_b, w_down, final_w, tm, tf):
    L = x1.shape[0]
    n_f = D_FF // tf
    hb = tm // FFN_HALO
    return pl.pallas_call(
        functools.partial(_ffn_kernel, tm=tm),
        grid=(L // tm, n_f),
        in_specs=[
            pl.BlockSpec((tm, D_MODEL), lambda i, j: (i, 0)),
            pl.BlockSpec((FFN_HALO, D_MODEL), lambda i, j: (jnp.maximum(i * hb - 1, 0), 0)),
            pl.BlockSpec((1, D_MODEL), lambda i, j: (0, 0)),
            pl.BlockSpec((D_MODEL, tf), lambda i, j: (0, j)),
            pl.BlockSpec((D_MODEL, tf), lambda i, j: (0, n_f + j)),
            pl.BlockSpec((FFN_CONV, tf), lambda i, j: (0, j)),
            pl.BlockSpec((FFN_CONV, tf), lambda i, j: (0, n_f + j)),
            pl.BlockSpec((1, tf), lambda i, j: (0, j)),
            pl.BlockSpec((1, tf), lambda i, j: (0, n_f + j)),
            pl.BlockSpec((tf, D_MODEL), lambda i, j: (j, 0)),
            pl.BlockSpec((1, D_MODEL), lambda i, j: (0, 0)),
        ],
        out_specs=pl.BlockSpec((tm, D_MODEL), lambda i, j: (i, 0)),
        out_shape=jax.ShapeDtypeStruct((L, D_MODEL), F32),
        scratch_shapes=[
            pltpu.VMEM((tm + FFN_HALO, D_MODEL), BF16),
            pltpu.VMEM((tm + FFN_HALO, tf), F32),
            pltpu.VMEM((tm + FFN_HALO, tf), F32),
            pltpu.VMEM((tm, D_MODEL), F32),
        ],
        compiler_params=_params(("arbitrary", "arbitrary"), 56),
        name="conv_ffn",
    )(x1, x1, norm_w, w_up, w_up, conv_w, conv_w, conv_b, conv_b, w_down, final_w)


def _importance_matrix(n_cmp_rows, n_blk):
    a_r, b_r = SEL_BLOCK // CMP_STRIDE, CMP_LEN // CMP_STRIDE
    w = np.zeros((n_cmp_rows, n_blk), np.float32)
    for j in range(n_blk):
        for m in range(a_r):
            for n in range(b_r):
                idx = a_r * (j + 1) - 1 - (m + n)
                if 0 <= idx < n_cmp_rows:
                    w[idx, j] += 1.0
    return w


def _head_expand():
    e = np.zeros((TAIL, D_SSM), np.float32)
    for h in range(SSM_HEADS):
        e[h, h * SSM_HEAD_DIM:(h + 1) * SSM_HEAD_DIM] = 1.0
    return e


def _layer(x2, mix_norm_w, w_in, ssm_conv_w, ssm_conv_b, ssm_dt_bias, ssm_a_log, ssm_d, ssm_norm_w,
           cmp_pos, cmp_w1, cmp_w2, attn_norm_w, w_out, ffn_norm_w, w_up, ffn_conv_w, ffn_conv_b, w_down,
           final_norm_w):
    L = x2.shape[0]
    row = lambda v: v.reshape(1, -1).astype(F32)

    bounds = np.cumsum([0, D_SSM, D_SSM + 2 * SSM_GROUPS * SSM_STATE, SSM_HEADS, D_ATTN] + [KV_DIM] * 6
                       + [3 * N_HEADS])
    seg = lambda k: w_in[:, bounds[k]:bounds[k + 1]]
    z_w, xbc_w, dt_w, q_w, kc_w, vc_w, ks_w, vs_w, kw_w, vw_w, gl_w = [seg(k) for k in range(11)]
    w_main = jnp.concatenate([z_w, xbc_w, q_w, kc_w, vc_w, ks_w, vs_w, kw_w, vw_w], axis=1).astype(BF16)
    w_tail = jnp.concatenate([dt_w, gl_w, jnp.zeros((D_MODEL, TAIL - SSM_HEADS - 3 * N_HEADS), F32)], axis=1)
    w_tail = w_tail.astype(BF16)
    pad_row = lambda v: jnp.concatenate([v.astype(F32), jnp.zeros((TAIL - SSM_HEADS,), F32)]).reshape(1, TAIL)

    p, kv, tail, tail_t = _in_projection(x2, row(mix_norm_w), w_main, w_tail, w_tail.T, tm=min(1024, L))
    cmp = _compress(p, cmp_pos, cmp_w1.astype(BF16), cmp_w2.astype(BF16))

    y_ssm = _ssd(p, tail, tail_t, ssm_conv_w.astype(F32), row(ssm_conv_b), pad_row(ssm_dt_bias),
                 ssm_dt_bias.astype(F32).reshape(SSM_HEADS, 1), pad_row(ssm_a_log),
                 ssm_a_log.astype(F32).reshape(SSM_HEADS, 1),
                 row(jnp.repeat(ssm_d.astype(F32), SSM_HEAD_DIM)), jnp.asarray(_head_expand(), BF16),
                 row(ssm_norm_w))

    n_blk = L // SEL_BLOCK
    slopes = 2.0 ** (-8.0 * jnp.arange(1, N_HEADS + 1, dtype=F32) / N_HEADS)
    imp_w = jnp.asarray(_importance_matrix(L // CMP_STRIDE, n_blk), BF16)
    o_c, o_w, sel, blk_any = _cmp_win_attention(slopes, p, cmp, kv, imp_w, min(N_SEL, n_blk))
    o_s = _sel_attention(slopes, p, kv, sel, blk_any)
    y_attn = _combine(o_c, o_s, o_w, tail, row(attn_norm_w), tm=256)

    x1 = _out_projection(x2, y_attn, y_ssm, w_out.astype(BF16), tm=512, tn=1024)
    return _conv_ffn(x1, row(ffn_norm_w), w_up.astype(BF16), ffn_conv_w.astype(F32), row(ffn_conv_b),
                     w_down.astype(BF16), row(final_norm_w), tm=512, tf=512)


def kernel(x, mix_norm_w, w_in, ssm_conv_w, ssm_conv_b, ssm_dt_bias, ssm_a_log, ssm_d, ssm_norm_w, cmp_pos_k,
           cmp_w1_k, cmp_w2_k, cmp_pos_v, cmp_w1_v, cmp_w2_v, attn_norm_w, w_out, ffn_norm_w, w_up, ffn_conv_w,
           ffn_conv_b, w_down, final_norm_w):
    bsz, L, _ = x.shape
    assert bsz == 1 and mix_norm_w.shape[0] == 1, "single sequence, single layer"
    assert L % 1024 == 0 and L >= WINDOW + Q_BLOCK
    out = _layer(
        x[0], mix_norm_w[0], w_in[0], ssm_conv_w[0], ssm_conv_b[0], ssm_dt_bias[0], ssm_a_log[0], ssm_d[0],
        ssm_norm_w[0], jnp.stack([cmp_pos_k[0], cmp_pos_v[0]]), jnp.stack([cmp_w1_k[0], cmp_w1_v[0]]),
        jnp.stack([cmp_w2_k[0], cmp_w2_v[0]]), attn_norm_w[0], w_out[0], ffn_norm_w[0], w_up[0], ffn_conv_w[0],
        ffn_conv_b[0], w_down[0], final_norm_w)
    return out[None]
```

```python
import functools

import numpy as np
import jax
import jax.numpy as jnp
from jax import lax
from jax.experimental import pallas as pl
from jax.experimental.pallas import tpu as pltpu

F32 = jnp.float32
BF16 = jnp.bfloat16

D_MODEL = 2048
D_ATTN = 2048
D_SSM = 2048
N_HEADS = 16
HEAD_DIM = 128
N_KV_HEADS = 4
Q_PER_KV = 4
KV_DIM = 512
CMP_LEN = 32
CMP_STRIDE = 16
CMP_HIDDEN = 256
SEL_BLOCK = 64
N_SEL = 16
WINDOW = 512
Q_BLOCK = 128
SSM_HEAD_DIM = 64
SSM_HEADS = 32
SSM_GROUPS = 4
SSM_STATE = 128
SSM_CONV = 4
SSM_CHUNK = 256
D_FF = 5632
FFN_CONV = 3
NORM_EPS = 1e-6
NEG_INF = -1e30
FORCE_SCORE = 1e4

LOG2E = 1.4426950408889634
M_INIT = -1e20

P_COLS = 6144
K_COLS = 1024
T_ROWS = 3072
TAIL = 128
PROJ_TN = 1024
KC_COL0 = 5120
VC_COL0 = 5632
MIB = 1024 * 1024


def _params(sem, vmem_mib):
    return pltpu.CompilerParams(dimension_semantics=sem, vmem_limit_bytes=vmem_mib * MIB)


def _rms(x, w):
    return x * lax.rsqrt(jnp.mean(x * x, axis=-1, keepdims=True) + NORM_EPS) * w


def _dot(a, b):
    return jnp.dot(a, b, preferred_element_type=F32)


def _dot_nt(a, b):
    return lax.dot_general(a, b, (((1,), (1,)), ((), ())), preferred_element_type=F32)


def _split3(x):
    hi = x.astype(BF16)
    r1 = x - hi.astype(F32)
    mid = r1.astype(BF16)
    lo = (r1 - mid.astype(F32)).astype(BF16)
    return hi, mid, lo


def _dot3(x, m):
    hi, mid, lo = _split3(x)
    return _dot(hi, m) + _dot(mid, m) + _dot(lo, m)


def _dot3_left(m, x):
    hi, mid, lo = _split3(x)
    return _dot(m, hi) + _dot(m, mid) + _dot(m, lo)


def _silu(x):
    return x * (1.0 / (1.0 + jnp.exp(-x)))


def _sigmoid(x):
    return 1.0 / (1.0 + jnp.exp(-x))


def _inproj_kernel(x_ref, nw_ref, w_ref, wtr_ref, wt_ref, wtt_ref, p_ref, k_ref, t_ref, tail_ref, tailt_ref,
                   xn_ref, *, n_p, n_q):
    j = pl.program_id(1)
    hp = PROJ_TN // HEAD_DIM

    @pl.when(j == 0)
    def _():
        xn = _rms(x_ref[...], nw_ref[...]).astype(BF16)
        xn_ref[...] = xn
        tail_ref[...] = _dot(xn, wt_ref[...])
        tailt_ref[...] = _dot_nt(wtt_ref[...], xn)

    @pl.when(j <= n_p)
    def _():
        r = _dot(xn_ref[...], w_ref[...])

        @pl.when(j < n_p)
        def _():
            p_ref[...] = r

        @pl.when(j == n_p)
        def _():
            for c in range(hp):
                k_ref[c] = r[:, c * HEAD_DIM:(c + 1) * HEAD_DIM].astype(BF16)

    @pl.when(j > n_p)
    def _():
        r = _dot_nt(wtr_ref[...], xn_ref[...])
        r = r * jnp.where(j - n_p - 1 < n_q, HEAD_DIM ** -0.5 * LOG2E, 1.0)
        for c in range(hp):
            t_ref[c] = r[c * HEAD_DIM:(c + 1) * HEAD_DIM, :].astype(BF16)


def _in_projection(x2, norm_w, w_nat, w_tr, w_tail, w_tail_t, tm):
    L = x2.shape[0]
    n_p = P_COLS // PROJ_TN
    n_t = T_ROWS // PROJ_TN
    hp = PROJ_TN // HEAD_DIM
    return pl.pallas_call(
        functools.partial(_inproj_kernel, n_p=n_p, n_q=D_ATTN // PROJ_TN),
        grid=(L // tm, n_p + 1 + n_t),
        in_specs=[
            pl.BlockSpec((tm, D_MODEL), lambda i, j: (i, 0)),
            pl.BlockSpec((1, D_MODEL), lambda i, j: (0, 0)),
            pl.BlockSpec((D_MODEL, PROJ_TN), lambda i, j: (0, jnp.minimum(j, n_p))),
            pl.BlockSpec((PROJ_TN, D_MODEL), lambda i, j: (jnp.clip(j - n_p - 1, 0, n_t - 1), 0)),
            pl.BlockSpec((D_MODEL, TAIL), lambda i, j: (0, 0)),
            pl.BlockSpec((TAIL, D_MODEL), lambda i, j: (0, 0)),
        ],
        out_specs=[
            pl.BlockSpec((tm, PROJ_TN), lambda i, j: (i, jnp.minimum(j, n_p - 1))),
            pl.BlockSpec((hp, tm, HEAD_DIM), lambda i, j: (0, i, 0)),
            pl.BlockSpec((hp, HEAD_DIM, tm), lambda i, j: (jnp.clip(j - n_p - 1, 0, n_t - 1), 0, i)),
            pl.BlockSpec((tm, TAIL), lambda i, j: (i, 0)),
            pl.BlockSpec((TAIL, tm), lambda i, j: (0, i)),
        ],
        out_shape=[
            jax.ShapeDtypeStruct((L, P_COLS), F32),
            jax.ShapeDtypeStruct((K_COLS // HEAD_DIM, L, HEAD_DIM), BF16),
            jax.ShapeDtypeStruct((T_ROWS // HEAD_DIM, HEAD_DIM, L), BF16),
            jax.ShapeDtypeStruct((L, TAIL), F32),
            jax.ShapeDtypeStruct((TAIL, L), F32),
        ],
        scratch_shapes=[pltpu.VMEM((tm, D_MODEL), BF16)],
        compiler_params=_params(("arbitrary", "arbitrary"), 56),
        name="in_projection",
    )(x2, norm_w, w_nat, w_tr, w_tail, w_tail_t)


def _compress_hidden(x_ref, pos_ref, w1_ref, n_rows):
    half = CMP_LEN // 2
    acc0 = jnp.zeros((n_rows, CMP_HIDDEN), F32)
    acc1 = jnp.zeros((n_rows, CMP_HIDDEN), F32)
    for i in range(half):
        xi = x_ref[pl.ds(i, n_rows, stride=CMP_STRIDE), :]
        a0 = (xi + pos_ref[i:i + 1, :]).astype(BF16)
        a1 = (xi + pos_ref[half + i:half + i + 1, :]).astype(BF16)
        acc0 = acc0 + _dot(a0, w1_ref[i * HEAD_DIM:(i + 1) * HEAD_DIM, :])
        acc1 = acc1 + _dot(a1, w1_ref[(half + i) * HEAD_DIM:(half + i + 1) * HEAD_DIM, :])
    hid = acc0 + pltpu.roll(acc1, n_rows - 1, 0)
    return jax.nn.gelu(hid).astype(BF16)


def _compress_kernel(xk_ref, xv_ref, pos_ref, w1_ref, w2k_ref, w2vt_ref, kc_ref, vct_ref, *, n_rows):
    hk = _compress_hidden(xk_ref, pos_ref.at[0], w1_ref.at[0], n_rows)
    kc_ref[...] = _dot(hk, w2k_ref[...]).astype(BF16)
    hv = _compress_hidden(xv_ref, pos_ref.at[1], w1_ref.at[1], n_rows)
    vct_ref[...] = _dot_nt(w2vt_ref[...], hv).astype(BF16)


def _compress(p, pos, w1, w2k, w2vt):
    L = p.shape[0]
    n_rows = L // CMP_STRIDE
    full = lambda a: pl.BlockSpec(a.shape, lambda g: (0,) * a.ndim)
    return pl.pallas_call(
        functools.partial(_compress_kernel, n_rows=n_rows),
        grid=(N_KV_HEADS,),
        in_specs=[
            pl.BlockSpec((L, HEAD_DIM), lambda g: (0, KC_COL0 // HEAD_DIM + g)),
            pl.BlockSpec((L, HEAD_DIM), lambda g: (0, VC_COL0 // HEAD_DIM + g)),
            full(pos), full(w1), full(w2k), full(w2vt),
        ],
        out_specs=[
            pl.BlockSpec((None, n_rows, HEAD_DIM), lambda g: (g, 0, 0)),
            pl.BlockSpec((None, HEAD_DIM, n_rows), lambda g: (g, 0, 0)),
        ],
        out_shape=[
            jax.ShapeDtypeStruct((N_KV_HEADS, n_rows, HEAD_DIM), BF16),
            jax.ShapeDtypeStruct((N_KV_HEADS, HEAD_DIM, n_rows), BF16),
        ],
        compiler_params=_params(("arbitrary",), 56),
        name="kv_compress",
    )(p, p, pos, w1, w2k, w2vt)


def _ssd_kernel(xs_ref, bc_ref, xs_halo_ref, bc_halo_ref, z_ref, tail_ref, tailt_ref, cw_ref, cb_ref,
                dtb_row_ref, dtb_col_ref, alog_row_ref, alog_col_ref, dskip_ref, expand_ref, nw_ref,
                o_ref, state_ref, cx_ref, cbc_ref):
    c = pl.program_id(0)
    Q = SSM_CHUNK
    GW = D_SSM // SSM_GROUPS
    halo = 8

    @pl.when(c == 0)
    def _():
        state_ref[...] = jnp.zeros_like(state_ref)
        cx_ref[0:halo, :] = jnp.zeros((halo, D_SSM), F32)
        cbc_ref[0:halo, :] = jnp.zeros((halo, 2 * SSM_GROUPS * SSM_STATE), F32)

    @pl.when(c > 0)
    def _():
        cx_ref[0:halo, :] = xs_halo_ref[...]
        cbc_ref[0:halo, :] = bc_halo_ref[...]

    cx_ref[halo:halo + Q, :] = xs_ref[...]
    cbc_ref[halo:halo + Q, :] = bc_ref[...]

    def conv_silu(ref, w, b):
        acc = b
        for k in range(SSM_CONV):
            acc = acc + ref[pl.ds(halo - (SSM_CONV - 1) + k, Q), :] * w[k:k + 1, :]
        return _silu(acc)

    cw = cw_ref[...]
    cb = cb_ref[...]
    xs = conv_silu(cx_ref, cw[:, :D_SSM], cb[:, :D_SSM])
    bcm = conv_silu(cbc_ref, cw[:, D_SSM:], cb[:, D_SSM:])

    def softplus(v):
        return jnp.maximum(v, 0.0) + jnp.log1p(jnp.exp(-jnp.abs(v)))

    dt = softplus(tail_ref[...] + dtb_row_ref[...])
    a = dt * (-jnp.exp(alog_row_ref[...]))
    rows = lax.broadcasted_iota(jnp.int32, (Q, Q), 0)
    cols = lax.broadcasted_iota(jnp.int32, (Q, Q), 1)
    causal = cols <= rows
    tri = jnp.where(causal, 1.0, 0.0).astype(BF16)
    a_cum = _dot3_left(tri, a)
    dt_t = softplus(tailt_ref[0:SSM_HEADS, :] + dtb_col_ref[...])
    a_t = dt_t * (-jnp.exp(alog_col_ref[...]))
    tri_t = jnp.where(rows <= cols, 1.0, 0.0).astype(BF16)
    a_cum_t = _dot3(a_t, tri_t)

    expand = expand_ref[...]
    dt_e = _dot3(dt, expand)
    ac_e = _dot3(a_cum, expand)
    xdt = xs * dt_e
    decay_to = jnp.exp(ac_e)
    a_last = ac_e[Q - 1:Q, :]
    xdd = xdt * jnp.exp(a_last - ac_e)
    chunk_decay = jnp.exp(a_last)

    lane = lax.broadcasted_iota(jnp.int32, (Q, 2 * SSM_HEAD_DIM), 1)
    first_head = lane < SSM_HEAD_DIM
    y_groups = []
    for g in range(SSM_GROUPS):
        bg = bcm[:, g * SSM_STATE:(g + 1) * SSM_STATE]
        cg = bcm[:, (SSM_GROUPS + g) * SSM_STATE:(SSM_GROUPS + g + 1) * SSM_STATE].astype(BF16)
        gmat = _dot_nt(cg, bg.astype(BF16))
        st = state_ref[g]
        y_off = _dot(cg, st.astype(BF16)) * decay_to[:, g * GW:(g + 1) * GW]
        new = _dot(bg.T.astype(BF16), xdd[:, g * GW:(g + 1) * GW].astype(BF16))
        state_ref[g] = st * chunk_decay[:, g * GW:(g + 1) * GW] + new
        pairs = []
        for pr in range(GW // (2 * SSM_HEAD_DIM)):
            h0 = g * (GW // SSM_HEAD_DIM) + 2 * pr
            xp = xdt[:, h0 * SSM_HEAD_DIM:(h0 + 2) * SSM_HEAD_DIM].astype(BF16)
            ys = []
            for h in (h0, h0 + 1):
                diff = a_cum[:, h:h + 1] - a_cum_t[h:h + 1, :]
                m = (gmat * jnp.exp(jnp.where(causal, diff, NEG_INF))).astype(BF16)
                ys.append(_dot(m, xp))
            pairs.append(jnp.where(first_head, ys[0], ys[1]))
        y_groups.append(jnp.concatenate(pairs, axis=1) + y_off)
    y = jnp.concatenate(y_groups, axis=1) + xs * dskip_ref[...]
    y = y * _silu(z_ref[...])
    o_ref[...] = _rms(y, nw_ref[...]).astype(BF16)


def _ssd(p, tail, tail_t, conv_w, conv_b, dtb_row, dtb_col, alog_row, alog_col, dskip_e, expand, norm_w):
    L = p.shape[0]
    Q = SSM_CHUNK
    bc_w = 2 * SSM_GROUPS * SSM_STATE
    xs_blk = D_SSM // D_SSM
    bc_blk = (2 * D_SSM) // bc_w
    hb = Q // 8
    full = lambda shape: pl.BlockSpec(shape, lambda c: (0,) * len(shape))
    return pl.pallas_call(
        _ssd_kernel,
        grid=(L // Q,),
        in_specs=[
            pl.BlockSpec((Q, D_SSM), lambda c: (c, xs_blk)),
            pl.BlockSpec((Q, bc_w), lambda c: (c, bc_blk)),
            pl.BlockSpec((8, D_SSM), lambda c: (jnp.maximum(c * hb - 1, 0), xs_blk)),
            pl.BlockSpec((8, bc_w), lambda c: (jnp.maximum(c * hb - 1, 0), bc_blk)),
            pl.BlockSpec((Q, D_SSM), lambda c: (c, 0)),
            pl.BlockSpec((Q, TAIL), lambda c: (c, 0)),
            pl.BlockSpec((TAIL, Q), lambda c: (0, c)),
            full(conv_w.shape), full(conv_b.shape), full(dtb_row.shape), full(dtb_col.shape),
            full(alog_row.shape), full(alog_col.shape), full(dskip_e.shape), full(expand.shape),
            full(norm_w.shape),
        ],
        out_specs=pl.BlockSpec((Q, D_SSM), lambda c: (c, 0)),
        out_shape=jax.ShapeDtypeStruct((L, D_SSM), BF16),
        scratch_shapes=[
            pltpu.VMEM((SSM_GROUPS, SSM_STATE, D_SSM // SSM_GROUPS), F32),
            pltpu.VMEM((Q + 8, D_SSM), F32),
            pltpu.VMEM((Q + 8, bc_w), F32),
        ],
        compiler_params=_params(("arbitrary",), 48),
        name="ssd_scan",
    )(p, p, p, p, p, tail, tail_t, conv_w, conv_b, dtb_row, dtb_col, alog_row, alog_col, dskip_e, expand, norm_w)


GATE_ROW0 = SSM_HEADS


def _gate_row(tailt_ref, branch, head):
    return _sigmoid(tailt_ref[pl.ds(GATE_ROW0 + branch * N_HEADS + head, 1), :])


def _cmp_win_kernel(slopes_ref, qt_ref, kc_ref, vct_ref, kw_ref, vwt_ref, tailt_ref,
                    ocw_ref, selt_ref, any_ref, s_ref, psum_ref, *, n_cmp_rows, n_blk, n_sel, chunk):
    g = pl.program_id(0)
    qb = pl.program_id(1)
    T = Q_BLOCK
    R = Q_PER_KV
    CH = chunk
    t0 = qb * T
    qt4 = jnp.concatenate([qt_ref[r] for r in range(R)], axis=1)
    t_row = t0 + lax.broadcasted_iota(jnp.int32, (1, T), 1)
    sl = [slopes_ref[g * R + r] * LOG2E for r in range(R)]

    n_chunks = (t0 + T - CMP_LEN) // CMP_STRIDE // CH + 1
    end_rel = lax.broadcasted_iota(jnp.int32, (CH, T), 0) * CMP_STRIDE + (CMP_LEN - 1)
    end_rel_f = end_rel.astype(F32)

    def scores(c, m):
        r0 = pl.multiple_of(c * CH, CH)
        s = _dot(kc_ref[pl.ds(r0, CH), :], qt4)
        valid = (t_row - r0 * CMP_STRIDE) >= end_rel
        key_rel = end_rel_f + (r0 * CMP_STRIDE - t0).astype(F32)
        ms = []
        for r in range(R):
            v = jnp.where(valid, s[:, r * T:(r + 1) * T] + sl[r] * key_rel, NEG_INF)
            s_ref[pl.ds(r0, CH), r * T:(r + 1) * T] = v
            ms.append(jnp.max(v, axis=0, keepdims=True))
        return jnp.maximum(m, jnp.concatenate(ms, axis=1))

    m = lax.fori_loop(0, n_chunks, scores, jnp.full((1, R * T), M_INIT, F32))

    def probs(c, carry):
        l, acc = carry
        r0 = pl.multiple_of(c * CH, CH)
        p = jnp.exp2(s_ref[pl.ds(r0, CH), :] - m)
        s_ref[pl.ds(r0, CH), :] = p
        l = l + jnp.sum(p, axis=0, keepdims=True)
        acc = acc + _dot(vct_ref[:, pl.ds(r0, CH)], p.astype(BF16))
        return l, acc

    l, acc = lax.fori_loop(0, n_chunks, probs, (jnp.zeros((1, R * T), F32), jnp.zeros((HEAD_DIM, R * T), F32)))
    inv = jnp.where(l > 0.0, 1.0 / l, 0.0)

    PAD = 8
    psum_ref[0:PAD, :] = jnp.zeros((PAD, T), F32)
    for c in range(n_cmp_rows // CH):
        @pl.when(c < n_chunks)
        def _():
            pn = s_ref[c * CH:(c + 1) * CH, :] * inv
            tot = pn[:, 0:T]
            for r in range(1, R):
                tot = tot + pn[:, r * T:(r + 1) * T]
            psum_ref[PAD + c * CH:PAD + (c + 1) * CH, :] = tot

        @pl.when(c >= n_chunks)
        def _():
            psum_ref[PAD + c * CH:PAD + (c + 1) * CH, :] = jnp.zeros((CH, T), F32)

    ratio, b_r = SEL_BLOCK // CMP_STRIDE, CMP_LEN // CMP_STRIDE
    imp = jnp.zeros((n_blk, T), F32)
    for shift in range(ratio + b_r - 1):
        mult = sum(1 for mm in range(ratio) for nn in range(b_r) if mm + nn == shift)
        imp = imp + float(mult) * psum_ref[pl.ds(PAD + ratio - 1 - shift, n_blk, stride=ratio), :]

    jb = lax.broadcasted_iota(jnp.int32, (n_blk, T), 0)
    jt = t_row // SEL_BLOCK
    forced = (jb == 0) | (jb == jt) | (jb == jt - 1)
    quota = (n_sel - 1 - jnp.minimum(jt, 2)).astype(F32)
    work0 = jnp.where(forced, NEG_INF, jnp.where(jb <= jt, imp, -1.0))
    jbf = jb.astype(F32)

    def pick(i, work):
        top = jnp.max(work, axis=0, keepdims=True)
        first = jnp.min(jnp.where(work == top, jbf, float(n_blk)), axis=0, keepdims=True)
        first = jnp.where(i.astype(F32) < quota, first, -1.0)
        return jnp.where(jbf == first, NEG_INF, work)

    n_rounds = n_sel - jnp.where(t0 >= 2 * SEL_BLOCK, 3, 1)
    work = lax.fori_loop(0, n_rounds, pick, work0)
    sel = jnp.where(work == NEG_INF, 1.0, 0.0)
    selt_ref[...] = sel
    any_ref[...] = _dot_nt(jnp.ones((8, T), BF16), sel.astype(BF16))

    wlen = WINDOW + T
    start = pl.multiple_of(jnp.maximum(t0 - WINDOW, 0), T)
    s = _dot(kw_ref[pl.ds(start, wlen), :], qt4)
    krow = lax.broadcasted_iota(jnp.int32, (wlen, T), 0)
    dist = (t_row - start) - krow
    valid = (dist >= 0) & (dist < WINDOW)
    key_rel = (krow + (start - t0)).astype(F32)
    vwt = vwt_ref[:, pl.ds(start, wlen)]
    for r in range(R):
        cs = slice(r * T, (r + 1) * T)
        v = jnp.where(valid, s[:, cs] + sl[r] * key_rel, NEG_INF)
        p = jnp.exp2(v - jnp.max(v, axis=0, keepdims=True))
        lw = jnp.sum(p, axis=0, keepdims=True)
        ow = _dot(vwt, p.astype(BF16))
        head = g * R + r
        o_t = acc[:, cs] * (inv[:, cs] * _gate_row(tailt_ref, 0, head)) + ow * (_gate_row(tailt_ref, 2, head) / lw)
        ocw_ref[:, r * HEAD_DIM:(r + 1) * HEAD_DIM] = o_t.T


def _cmp_win_attention(slopes, t_out, kc, vct, k_nat, tail_t, n_sel):
    L = t_out.shape[2]
    n_cmp_rows = L // CMP_STRIDE
    n_blk = L // SEL_BLOCK
    nqb = L // Q_BLOCK
    chunk = min(256, n_cmp_rows)
    kern = functools.partial(_cmp_win_kernel, n_cmp_rows=n_cmp_rows, n_blk=n_blk, n_sel=n_sel, chunk=chunk)
    vw_head0 = (D_ATTN + KV_DIM) // HEAD_DIM
    return pl.pallas_call(
        kern,
        grid=(N_KV_HEADS, nqb),
        in_specs=[
            pl.BlockSpec(memory_space=pltpu.SMEM),
            pl.BlockSpec((Q_PER_KV, HEAD_DIM, Q_BLOCK), lambda g, i: (g, 0, i)),
            pl.BlockSpec((None, n_cmp_rows, HEAD_DIM), lambda g, i: (g, 0, 0)),
            pl.BlockSpec((None, HEAD_DIM, n_cmp_rows), lambda g, i: (g, 0, 0)),
            pl.BlockSpec((None, L, HEAD_DIM), lambda g, i: (N_KV_HEADS + g, 0, 0)),
            pl.BlockSpec((None, HEAD_DIM, L), lambda g, i: (vw_head0 + g, 0, 0)),
            pl.BlockSpec((TAIL, Q_BLOCK), lambda g, i: (0, i)),
        ],
        out_specs=[
            pl.BlockSpec((Q_BLOCK, KV_DIM), lambda g, i: (i, g)),
            pl.BlockSpec((None, None, n_blk, Q_BLOCK), lambda g, i: (g, i, 0, 0)),
            pl.BlockSpec((None, None, 8, n_blk), lambda g, i: (g, i, 0, 0)),
        ],
        out_shape=[
            jax.ShapeDtypeStruct((L, D_ATTN), F32),
            jax.ShapeDtypeStruct((N_KV_HEADS, nqb, n_blk, Q_BLOCK), F32),
            jax.ShapeDtypeStruct((N_KV_HEADS, nqb, 8, n_blk), F32),
        ],
        scratch_shapes=[
            pltpu.VMEM((n_cmp_rows, Q_PER_KV * Q_BLOCK), F32),
            pltpu.VMEM((n_cmp_rows + 8, Q_BLOCK), F32),
        ],
        compiler_params=_params(("arbitrary", "arbitrary"), 48),
        name="cmp_win_attention",
    )(slopes, t_out, kc, vct, k_nat, t_out, tail_t)


SEL_TK = 512
SEL_TQ = 256


def _sel_kernel(tiles_ref, counts_ref, slopes_ref, qt_ref, ks_ref, vst_ref, selt_ref, tailt_ref, ocw_ref, o_ref,
                m_ref, l_ref, acc_ref, *, n_tiles_max):
    g = pl.program_id(0)
    qb = pl.program_id(1)
    T = SEL_TQ
    R = Q_PER_KV
    TK = SEL_TK
    bpt = TK // SEL_BLOCK
    t0 = qb * T
    qt4 = jnp.concatenate([qt_ref[r] for r in range(R)], axis=1)
    t_row = t0 + lax.broadcasted_iota(jnp.int32, (1, T), 1)
    sl = [slopes_ref[g * R + r] * LOG2E for r in range(R)]
    krow = lax.broadcasted_iota(jnp.int32, (TK, T), 0)
    krow_f = krow.astype(F32)

    m_ref[...] = jnp.full(m_ref.shape, M_INIT, F32)
    l_ref[...] = jnp.zeros(l_ref.shape, F32)
    acc_ref[...] = jnp.zeros(acc_ref.shape, F32)
    step = g * pl.num_programs(1) + qb

    def tile(n, carry):
        kt = tiles_ref[step * n_tiles_max + n]
        k0 = pl.multiple_of(kt * TK, TK)
        s = _dot(ks_ref[pl.ds(k0, TK), :], qt4)
        b0 = pl.multiple_of(kt * bpt, bpt)
        sel8 = jnp.concatenate([selt_ref[u, pl.ds(b0, bpt), :] for u in range(T // Q_BLOCK)], axis=1)
        chosen = jnp.concatenate(
            [jnp.broadcast_to(sel8[c:c + 1, :], (SEL_BLOCK, T)) for c in range(bpt)], axis=0)
        t_rel = t_row - k0
        valid = (chosen > 0.5) & (krow <= t_rel)
        t_rel_f = t_rel.astype(F32)
        m_old = m_ref[...]
        vs, ms, cols = [], [], []
        for r in range(R):
            v = jnp.where(valid, s[:, r * T:(r + 1) * T] + sl[r] * krow_f, NEG_INF)
            vs.append(v)
            cols.append(sl[r] * t_rel_f)
            ms.append(jnp.max(v, axis=0, keepdims=True))
        col = jnp.concatenate(cols, axis=1)
        m_new = jnp.maximum(m_old, jnp.concatenate(ms, axis=1) - col)
        shift = m_new + col
        alpha = jnp.exp2(m_old - m_new)
        p = jnp.concatenate([jnp.exp2(vs[r] - shift[:, r * T:(r + 1) * T]) for r in range(R)], axis=1)
        l_ref[...] = alpha * l_ref[...] + jnp.sum(p, axis=0, keepdims=True)
        acc_ref[...] = alpha * acc_ref[...] + _dot(vst_ref[:, pl.ds(k0, TK)], p.astype(BF16))
        m_ref[...] = m_new
        return carry

    lax.fori_loop(0, counts_ref[step], tile, 0)
    l = l_ref[...]
    inv = jnp.where(l > 0.0, 1.0 / l, 0.0)
    for r in range(R):
        cs = slice(r * T, (r + 1) * T)
        o_t = acc_ref[:, cs] * (inv[:, cs] * _gate_row(tailt_ref, 1, g * R + r))
        hs = slice(r * HEAD_DIM, (r + 1) * HEAD_DIM)
        o_ref[:, hs] = ocw_ref[:, hs] + o_t.T


def _active_tiles(blk_count, L):
    G = blk_count.shape[0]
    nqb = L // SEL_TQ
    n_t = L // SEL_TK
    flags = blk_count.reshape(G, nqb, SEL_TQ // Q_BLOCK, n_t, SEL_TK // SEL_BLOCK).max(axis=(2, 4)) > 0.5
    kt = jnp.arange(n_t, dtype=jnp.int32)
    last = (jnp.arange(nqb, dtype=jnp.int32) * SEL_TQ + SEL_TQ - 1) // SEL_TK
    flags = flags & (kt[None, None, :] <= last[None, :, None])
    order = jnp.sort(jnp.where(flags, kt, n_t + kt), axis=-1)
    tiles = jnp.where(order < n_t, order, 0).astype(jnp.int32)
    return tiles.reshape(-1), flags.sum(axis=-1).astype(jnp.int32).reshape(-1)


def _sel_attention(slopes, t_out, k_nat, selt, blk_count, tail_t, ocw):
    L = t_out.shape[2]
    n_blk = L // SEL_BLOCK
    n_t = L // SEL_TK
    cols = Q_PER_KV * SEL_TQ
    vs_head0 = D_ATTN // HEAD_DIM
    tiles, counts = _active_tiles(blk_count, L)
    grid_spec = pltpu.PrefetchScalarGridSpec(
        num_scalar_prefetch=2,
        grid=(N_KV_HEADS, L // SEL_TQ),
        in_specs=[
            pl.BlockSpec(memory_space=pltpu.SMEM),
            pl.BlockSpec((Q_PER_KV, HEAD_DIM, SEL_TQ), lambda g, i, *_: (g, 0, i)),
            pl.BlockSpec((None, L, HEAD_DIM), lambda g, i, *_: (g, 0, 0)),
            pl.BlockSpec((None, HEAD_DIM, L), lambda g, i, *_: (vs_head0 + g, 0, 0)),
            pl.BlockSpec((None, SEL_TQ // Q_BLOCK, n_blk, Q_BLOCK), lambda g, i, *_: (g, i, 0, 0)),
            pl.BlockSpec((TAIL, SEL_TQ), lambda g, i, *_: (0, i)),
            pl.BlockSpec((SEL_TQ, KV_DIM), lambda g, i, *_: (i, g)),
        ],
        out_specs=pl.BlockSpec((SEL_TQ, KV_DIM), lambda g, i, *_: (i, g)),
        scratch_shapes=[
            pltpu.VMEM((1, cols), F32),
            pltpu.VMEM((1, cols), F32),
            pltpu.VMEM((HEAD_DIM, cols), F32),
        ],
    )
    return pl.pallas_call(
        functools.partial(_sel_kernel, n_tiles_max=n_t),
        grid_spec=grid_spec,
        out_shape=jax.ShapeDtypeStruct((L, D_ATTN), F32),
        compiler_params=_params(("arbitrary", "arbitrary"), 48),
        name="selected_attention",
    )(tiles, counts, slopes, t_out, k_nat, t_out, selt, tail_t, ocw)


def _outproj_kernel(x_ref, oa_ref, ys_ref, nw_ref, wa_ref, ws_ref, o_ref, ya_ref):
    @pl.when(pl.program_id(1) == 0)
    def _():
        ya_ref[...] = _rms(oa_ref[...], nw_ref[...]).astype(BF16)

    o_ref[...] = x_ref[...] + _dot(ya_ref[...], wa_ref[...]) + _dot(ys_ref[...], ws_ref[...])


def _out_projection(x2, o_attn, ys, attn_norm_w, w_out, tm, tn):
    L = x2.shape[0]
    n_j = D_MODEL // tn
    return pl.pallas_call(
        _outproj_kernel,
        grid=(L // tm, n_j),
        in_specs=[
            pl.BlockSpec((tm, tn), lambda i, j: (i, j)),
            pl.BlockSpec((tm, D_ATTN), lambda i, j: (i, 0)),
            pl.BlockSpec((tm, D_SSM), lambda i, j: (i, 0)),
            pl.BlockSpec((1, D_ATTN), lambda i, j: (0, 0)),
            pl.BlockSpec((D_ATTN, tn), lambda i, j: (0, j)),
            pl.BlockSpec((D_SSM, tn), lambda i, j: (1, j)),
        ],
        out_specs=pl.BlockSpec((tm, tn), lambda i, j: (i, j)),
        out_shape=jax.ShapeDtypeStruct((L, D_MODEL), F32),
        scratch_shapes=[pltpu.VMEM((tm, D_ATTN), BF16)],
        compiler_params=_params(("arbitrary", "arbitrary"), 48),
        name="out_projection",
    )(x2, o_attn, ys, attn_norm_w, w_out, w_out)


FFN_HALO = 16


def _ffn_kernel(x_ref, halo_ref, nw_ref, wg_ref, wv_ref, cwg_ref, cwv_ref, cbg_ref, cbv_ref, wd_ref, fw_ref,
                o_ref, hn_ref, ug_ref, uv_ref, acc_ref, *, tm):
    i = pl.program_id(0)
    j = pl.program_id(1)
    H = FFN_HALO

    @pl.when(j == 0)
    def _():
        hn_ref[H:H + tm, :] = _rms(x_ref[...], nw_ref[...]).astype(BF16)
        acc_ref[...] = jnp.zeros_like(acc_ref)

    @pl.when((j == 0) & (i == 0))
    def _():
        hn_ref[0:H, :] = jnp.zeros((H, D_MODEL), BF16)

    @pl.when((j == 0) & (i > 0))
    def _():
        hn_ref[0:H, :] = _rms(halo_ref[...], nw_ref[...]).astype(BF16)

    hn = hn_ref[...]
    ug_ref[...] = _dot(hn, wg_ref[...])
    uv_ref[...] = _dot(hn, wv_ref[...])

    def conv(u_ref, w_ref, b_ref):
        acc = b_ref[...]
        for k in range(FFN_CONV):
            acc = acc + u_ref[pl.ds(H - (FFN_CONV - 1) + k, tm), :] * w_ref[k:k + 1, :]
        return acc

    act = _silu(conv(ug_ref, cwg_ref, cbg_ref)) * conv(uv_ref, cwv_ref, cbv_ref)
    acc_ref[...] += _dot(act.astype(BF16), wd_ref[...])

    @pl.when(j == pl.num_programs(1) - 1)
    def _():
        o_ref[...] = _rms(x_ref[...] + acc_ref[...], fw_ref[...])


def _conv_ffn(x1, norm_w, w_up, conv_w, conv_b, w_down, final_w, tm, tf):
    L = x1.shape[0]
    n_f = D_FF // tf
    hb = tm // FFN_HALO
    return pl.pallas_call(
        functools.partial(_ffn_kernel, tm=tm),
        grid=(L // tm, n_f),
        in_specs=[
            pl.BlockSpec((tm, D_MODEL), lambda i, j: (i, 0)),
            pl.BlockSpec((FFN_HALO, D_MODEL), lambda i, j: (jnp.maximum(i * hb - 1, 0), 0)),
            pl.BlockSpec((1, D_MODEL), lambda i, j: (0, 0)),
            pl.BlockSpec((D_MODEL, tf), lambda i, j: (0, j)),
            pl.BlockSpec((D_MODEL, tf), lambda i, j: (0, n_f + j)),
            pl.BlockSpec((FFN_CONV, tf), lambda i, j: (0, j)),
            pl.BlockSpec((FFN_CONV, tf), lambda i, j: (0, n_f + j)),
            pl.BlockSpec((1, tf), lambda i, j: (0, j)),
            pl.BlockSpec((1, tf), lambda i, j: (0, n_f + j)),
            pl.BlockSpec((tf, D_MODEL), lambda i, j: (j, 0)),
            pl.BlockSpec((1, D_MODEL), lambda i, j: (0, 0)),
        ],
        out_specs=pl.BlockSpec((tm, D_MODEL), lambda i, j: (i, 0)),
        out_shape=jax.ShapeDtypeStruct((L, D_MODEL), F32),
        scratch_shapes=[
            pltpu.VMEM((tm + FFN_HALO, D_MODEL), BF16),
            pltpu.VMEM((tm + FFN_HALO, tf), F32),
            pltpu.VMEM((tm + FFN_HALO, tf), F32),
            pltpu.VMEM((tm, D_MODEL), F32),
        ],
        compiler_params=_params(("arbitrary", "arbitrary"), 56),
        name="conv_ffn",
    )(x1, x1, norm_w, w_up, w_up, conv_w, conv_w, conv_b, conv_b, w_down, final_w)


def _head_expand():
    e = np.zeros((TAIL, D_SSM), np.float32)
    for h in range(SSM_HEADS):
        e[h, h * SSM_HEAD_DIM:(h + 1) * SSM_HEAD_DIM] = 1.0
    return e


def _layer(x2, mix_norm_w, w_in, ssm_conv_w, ssm_conv_b, ssm_dt_bias, ssm_a_log, ssm_d, ssm_norm_w,
           cmp_pos, cmp_w1, cmp_w2_k, cmp_w2_v, attn_norm_w, w_out, ffn_norm_w, w_up, ffn_conv_w, ffn_conv_b,
           w_down, final_norm_w):
    L = x2.shape[0]
    row = lambda v: v.reshape(1, -1).astype(F32)

    bounds = np.cumsum([0, D_SSM, D_SSM + 2 * SSM_GROUPS * SSM_STATE, SSM_HEADS, D_ATTN] + [KV_DIM] * 6
                       + [3 * N_HEADS])
    seg = lambda k: w_in[:, bounds[k]:bounds[k + 1]]
    z_w, xbc_w, dt_w, q_w, kc_w, vc_w, ks_w, vs_w, kw_w, vw_w, gl_w = [seg(k) for k in range(11)]
    w_nat = jnp.concatenate([z_w, xbc_w, kc_w, vc_w, ks_w, kw_w], axis=1).astype(BF16)
    w_tr = jnp.concatenate([q_w, vs_w, vw_w], axis=1).T.astype(BF16)
    w_tail = jnp.concatenate([dt_w, gl_w, jnp.zeros((D_MODEL, TAIL - SSM_HEADS - 3 * N_HEADS), F32)], axis=1)
    w_tail = w_tail.astype(BF16)
    pad_row = lambda v: jnp.concatenate([v.astype(F32), jnp.zeros((TAIL - SSM_HEADS,), F32)]).reshape(1, TAIL)

    p, k_nat, t_out, tail, tail_t = _in_projection(x2, row(mix_norm_w), w_nat, w_tr, w_tail, w_tail.T,
                                                   tm=min(512, L))
    kc, vct = _compress(p, cmp_pos.astype(F32), cmp_w1.astype(BF16), cmp_w2_k.astype(BF16),
                        cmp_w2_v.T.astype(BF16))

    y_ssm = _ssd(p, tail, tail_t, ssm_conv_w.astype(F32), row(ssm_conv_b), pad_row(ssm_dt_bias),
                 ssm_dt_bias.astype(F32).reshape(SSM_HEADS, 1), pad_row(ssm_a_log),
                 ssm_a_log.astype(F32).reshape(SSM_HEADS, 1),
                 row(jnp.repeat(ssm_d.astype(F32), SSM_HEAD_DIM)), jnp.asarray(_head_expand(), BF16),
                 row(ssm_norm_w))

    n_blk = L // SEL_BLOCK
    slopes = 2.0 ** (-8.0 * jnp.arange(1, N_HEADS + 1, dtype=F32) / N_HEADS)
    ocw, selt, blk_count = _cmp_win_attention(slopes, t_out, kc, vct, k_nat, tail_t, min(N_SEL, n_blk))
    o_attn = _sel_attention(slopes, t_out, k_nat, selt, blk_count[:, :, 0, :], tail_t, ocw)

    x1 = _out_projection(x2, o_attn, y_ssm, row(attn_norm_w), w_out.astype(BF16), tm=512, tn=1024)
    return _conv_ffn(x1, row(ffn_norm_w), w_up.astype(BF16), ffn_conv_w.astype(F32), row(ffn_conv_b),
                     w_down.astype(BF16), row(final_norm_w), tm=512, tf=512)


def kernel(x, mix_norm_w, w_in, ssm_conv_w, ssm_conv_b, ssm_dt_bias, ssm_a_log, ssm_d, ssm_norm_w, cmp_pos_k,
           cmp_w1_k, cmp_w2_k, cmp_pos_v, cmp_w1_v, cmp_w2_v, attn_norm_w, w_out, ffn_norm_w, w_up, ffn_conv_w,
           ffn_conv_b, w_down, final_norm_w):
    bsz, L, _ = x.shape
    assert bsz == 1 and mix_norm_w.shape[0] == 1, "single sequence, single layer"
    assert L % 1024 == 0 and L >= WINDOW + Q_BLOCK
    out = _layer(
        x[0], mix_norm_w[0], w_in[0], ssm_conv_w[0], ssm_conv_b[0], ssm_dt_bias[0], ssm_a_log[0], ssm_d[0],
        ssm_norm_w[0], jnp.stack([cmp_pos_k[0], cmp_pos_v[0]]), jnp.stack([cmp_w1_k[0], cmp_w1_v[0]]),
        cmp_w2_k[0], cmp_w2_v[0], attn_norm_w[0], w_out[0], ffn_norm_w[0], w_up[0], ffn_conv_w[0],
        ffn_conv_b[0], w_down[0], final_norm_w)
    return out[None]
```

```python
import functools

import numpy as np
import jax
import jax.numpy as jnp
from jax import lax
from jax.experimental import pallas as pl
from jax.experimental.pallas import tpu as pltpu

F32 = jnp.float32
BF16 = jnp.bfloat16

D_MODEL = 2048
D_ATTN = 2048
D_SSM = 2048
N_HEADS = 16
HEAD_DIM = 128
N_KV_HEADS = 4
Q_PER_KV = 4
KV_DIM = 512
CMP_LEN = 32
CMP_STRIDE = 16
CMP_HIDDEN = 256
SEL_BLOCK = 64
N_SEL = 16
WINDOW = 512
ATT_TQ = 256
SSM_HEAD_DIM = 64
SSM_HEADS = 32
SSM_GROUPS = 4
SSM_STATE = 128
SSM_CONV = 4
SSM_CHUNK = 256
D_FF = 5632
FFN_CONV = 3
NORM_EPS = 1e-6
NEG_INF = -1e30
FORCE_SCORE = 1e4

LOG2E = 1.4426950408889634
M_INIT = -1e20

P_COLS = 6144
K_COLS = 1024
T_ROWS = 3072
TAIL = 128
PROJ_TN = 1024
KC_COL0 = 5120
VC_COL0 = 5632
MIB = 1024 * 1024


def _params(sem, vmem_mib):
    return pltpu.CompilerParams(dimension_semantics=sem, vmem_limit_bytes=vmem_mib * MIB)


def _rms(x, w):
    return x * lax.rsqrt(jnp.mean(x * x, axis=-1, keepdims=True) + NORM_EPS) * w


def _dot(a, b):
    return jnp.dot(a, b, preferred_element_type=F32)


def _dot_nt(a, b):
    return lax.dot_general(a, b, (((1,), (1,)), ((), ())), preferred_element_type=F32)


def _split3(x):
    hi = x.astype(BF16)
    r1 = x - hi.astype(F32)
    mid = r1.astype(BF16)
    lo = (r1 - mid.astype(F32)).astype(BF16)
    return hi, mid, lo


def _dot3(x, m):
    hi, mid, lo = _split3(x)
    return _dot(hi, m) + _dot(mid, m) + _dot(lo, m)


def _dot3_left(m, x):
    hi, mid, lo = _split3(x)
    return _dot(m, hi) + _dot(m, mid) + _dot(m, lo)


def _silu(x):
    return x * (1.0 / (1.0 + jnp.exp(-x)))


def _sigmoid(x):
    return 1.0 / (1.0 + jnp.exp(-x))


def _inproj_kernel(x_ref, nw_ref, w_ref, wtr_ref, wt_ref, wtt_ref, p_ref, k_ref, t_ref, tail_ref, tailt_ref,
                   xn_ref, *, n_p, n_q):
    j = pl.program_id(1)
    hp = PROJ_TN // HEAD_DIM

    @pl.when(j == 0)
    def _():
        xn = _rms(x_ref[...], nw_ref[...]).astype(BF16)
        xn_ref[...] = xn
        tail_ref[...] = _dot(xn, wt_ref[...])
        tailt_ref[...] = _dot_nt(wtt_ref[...], xn)

    @pl.when(j <= n_p)
    def _():
        r = _dot(xn_ref[...], w_ref[...])

        @pl.when(j < n_p)
        def _():
            p_ref[...] = r

        @pl.when(j == n_p)
        def _():
            for c in range(hp):
                k_ref[c] = r[:, c * HEAD_DIM:(c + 1) * HEAD_DIM].astype(BF16)

    @pl.when(j > n_p)
    def _():
        r = _dot_nt(wtr_ref[...], xn_ref[...])
        r = r * jnp.where(j - n_p - 1 < n_q, HEAD_DIM ** -0.5 * LOG2E, 1.0)
        for c in range(hp):
            t_ref[c] = r[c * HEAD_DIM:(c + 1) * HEAD_DIM, :].astype(BF16)


def _in_projection(x2, norm_w, w_nat, w_tr, w_tail, w_tail_t, tm):
    L = x2.shape[0]
    n_p = P_COLS // PROJ_TN
    n_t = T_ROWS // PROJ_TN
    hp = PROJ_TN // HEAD_DIM
    return pl.pallas_call(
        functools.partial(_inproj_kernel, n_p=n_p, n_q=D_ATTN // PROJ_TN),
        grid=(L // tm, n_p + 1 + n_t),
        in_specs=[
            pl.BlockSpec((tm, D_MODEL), lambda i, j: (i, 0), pipeline_mode=pl.Buffered(1)),
            pl.BlockSpec((1, D_MODEL), lambda i, j: (0, 0)),
            pl.BlockSpec((D_MODEL, PROJ_TN), lambda i, j: (0, jnp.minimum(j, n_p))),
            pl.BlockSpec((PROJ_TN, D_MODEL), lambda i, j: (jnp.clip(j - n_p - 1, 0, n_t - 1), 0)),
            pl.BlockSpec((D_MODEL, TAIL), lambda i, j: (0, 0)),
            pl.BlockSpec((TAIL, D_MODEL), lambda i, j: (0, 0)),
        ],
        out_specs=[
            pl.BlockSpec((tm, PROJ_TN), lambda i, j: (i, jnp.minimum(j, n_p - 1))),
            pl.BlockSpec((hp, tm, HEAD_DIM), lambda i, j: (0, i, 0)),
            pl.BlockSpec((hp, HEAD_DIM, tm), lambda i, j: (jnp.clip(j - n_p - 1, 0, n_t - 1), 0, i)),
            pl.BlockSpec((tm, TAIL), lambda i, j: (i, 0)),
            pl.BlockSpec((TAIL, tm), lambda i, j: (0, i)),
        ],
        out_shape=[
            jax.ShapeDtypeStruct((L, P_COLS), F32),
            jax.ShapeDtypeStruct((K_COLS // HEAD_DIM, L, HEAD_DIM), BF16),
            jax.ShapeDtypeStruct((T_ROWS // HEAD_DIM, HEAD_DIM, L), BF16),
            jax.ShapeDtypeStruct((L, TAIL), F32),
            jax.ShapeDtypeStruct((TAIL, L), F32),
        ],
        scratch_shapes=[pltpu.VMEM((tm, D_MODEL), BF16)],
        compiler_params=_params(("arbitrary", "arbitrary"), 56),
        name="in_projection",
    )(x2, norm_w, w_nat, w_tr, w_tail, w_tail_t)


def _compress_hidden(x_ref, pos_ref, w1_ref, n_rows):
    half = CMP_LEN // 2
    acc0 = jnp.zeros((n_rows, CMP_HIDDEN), F32)
    acc1 = jnp.zeros((n_rows, CMP_HIDDEN), F32)
    for i in range(half):
        xi = x_ref[pl.ds(i, n_rows, stride=CMP_STRIDE), :]
        a0 = (xi + pos_ref[i:i + 1, :]).astype(BF16)
        a1 = (xi + pos_ref[half + i:half + i + 1, :]).astype(BF16)
        acc0 = acc0 + _dot(a0, w1_ref[i * HEAD_DIM:(i + 1) * HEAD_DIM, :])
        acc1 = acc1 + _dot(a1, w1_ref[(half + i) * HEAD_DIM:(half + i + 1) * HEAD_DIM, :])
    hid = acc0 + pltpu.roll(acc1, n_rows - 1, 0)
    return jax.nn.gelu(hid).astype(BF16)


def _compress_kernel(xk_ref, xv_ref, pos_ref, w1_ref, w2k_ref, w2vt_ref, kc_ref, vct_ref, *, n_rows):
    hk = _compress_hidden(xk_ref, pos_ref.at[0], w1_ref.at[0], n_rows)
    kc_ref[...] = _dot(hk, w2k_ref[...]).astype(BF16)
    hv = _compress_hidden(xv_ref, pos_ref.at[1], w1_ref.at[1], n_rows)
    vct_ref[...] = _dot_nt(w2vt_ref[...], hv).astype(BF16)


def _compress(p, pos, w1, w2k, w2vt):
    L = p.shape[0]
    n_rows = L // CMP_STRIDE
    full = lambda a: pl.BlockSpec(a.shape, lambda g: (0,) * a.ndim)
    return pl.pallas_call(
        functools.partial(_compress_kernel, n_rows=n_rows),
        grid=(N_KV_HEADS,),
        in_specs=[
            pl.BlockSpec((L, HEAD_DIM), lambda g: (0, KC_COL0 // HEAD_DIM + g)),
            pl.BlockSpec((L, HEAD_DIM), lambda g: (0, VC_COL0 // HEAD_DIM + g)),
            full(pos), full(w1), full(w2k), full(w2vt),
        ],
        out_specs=[
            pl.BlockSpec((None, n_rows, HEAD_DIM), lambda g: (g, 0, 0)),
            pl.BlockSpec((None, HEAD_DIM, n_rows), lambda g: (g, 0, 0)),
        ],
        out_shape=[
            jax.ShapeDtypeStruct((N_KV_HEADS, n_rows, HEAD_DIM), BF16),
            jax.ShapeDtypeStruct((N_KV_HEADS, HEAD_DIM, n_rows), BF16),
        ],
        compiler_params=_params(("arbitrary",), 56),
        name="kv_compress",
    )(p, p, pos, w1, w2k, w2vt)


def _ssd_kernel(xs_ref, bc_ref, xs_halo_ref, bc_halo_ref, z_ref, tail_ref, tailt_ref, cw_ref, cb_ref,
                dtb_row_ref, dtb_col_ref, alog_row_ref, alog_col_ref, dskip_ref, expand_ref, nw_ref,
                o_ref, state_ref, cx_ref, cbc_ref):
    c = pl.program_id(0)
    Q = SSM_CHUNK
    GW = D_SSM // SSM_GROUPS
    halo = 8

    @pl.when(c == 0)
    def _():
        state_ref[...] = jnp.zeros_like(state_ref)
        cx_ref[0:halo, :] = jnp.zeros((halo, D_SSM), F32)
        cbc_ref[0:halo, :] = jnp.zeros((halo, 2 * SSM_GROUPS * SSM_STATE), F32)

    @pl.when(c > 0)
    def _():
        cx_ref[0:halo, :] = xs_halo_ref[...]
        cbc_ref[0:halo, :] = bc_halo_ref[...]

    cx_ref[halo:halo + Q, :] = xs_ref[...]
    cbc_ref[halo:halo + Q, :] = bc_ref[...]

    def conv_silu(ref, w, b):
        acc = b
        for k in range(SSM_CONV):
            acc = acc + ref[pl.ds(halo - (SSM_CONV - 1) + k, Q), :] * w[k:k + 1, :]
        return _silu(acc)

    cw = cw_ref[...]
    cb = cb_ref[...]
    xs = conv_silu(cx_ref, cw[:, :D_SSM], cb[:, :D_SSM])
    bcm = conv_silu(cbc_ref, cw[:, D_SSM:], cb[:, D_SSM:])

    def softplus(v):
        return jnp.maximum(v, 0.0) + jnp.log1p(jnp.exp(-jnp.abs(v)))

    dt = softplus(tail_ref[...] + dtb_row_ref[...])
    a = dt * (-jnp.exp(alog_row_ref[...]))
    rows = lax.broadcasted_iota(jnp.int32, (Q, Q), 0)
    cols = lax.broadcasted_iota(jnp.int32, (Q, Q), 1)
    causal = cols <= rows
    tri = jnp.where(causal, 1.0, 0.0).astype(BF16)
    a_cum = _dot3_left(tri, a)
    dt_t = softplus(tailt_ref[0:SSM_HEADS, :] + dtb_col_ref[...])
    a_t = dt_t * (-jnp.exp(alog_col_ref[...]))
    tri_t = jnp.where(rows <= cols, 1.0, 0.0).astype(BF16)
    a_cum_t = _dot3(a_t, tri_t)

    expand = expand_ref[...]
    dt_e = _dot3(dt, expand)
    ac_e = _dot3(a_cum, expand)
    xdt = xs * dt_e
    decay_to = jnp.exp(ac_e)
    a_last = ac_e[Q - 1:Q, :]
    xdd = xdt * jnp.exp(a_last - ac_e)
    chunk_decay = jnp.exp(a_last)

    lane = lax.broadcasted_iota(jnp.int32, (Q, 2 * SSM_HEAD_DIM), 1)
    first_head = lane < SSM_HEAD_DIM
    y_groups = []
    for g in range(SSM_GROUPS):
        bg = bcm[:, g * SSM_STATE:(g + 1) * SSM_STATE]
        cg = bcm[:, (SSM_GROUPS + g) * SSM_STATE:(SSM_GROUPS + g + 1) * SSM_STATE].astype(BF16)
        gmat = _dot_nt(cg, bg.astype(BF16))
        st = state_ref[g]
        y_off = _dot(cg, st.astype(BF16)) * decay_to[:, g * GW:(g + 1) * GW]
        new = _dot(bg.T.astype(BF16), xdd[:, g * GW:(g + 1) * GW].astype(BF16))
        state_ref[g] = st * chunk_decay[:, g * GW:(g + 1) * GW] + new
        pairs = []
        for pr in range(GW // (2 * SSM_HEAD_DIM)):
            h0 = g * (GW // SSM_HEAD_DIM) + 2 * pr
            xp = xdt[:, h0 * SSM_HEAD_DIM:(h0 + 2) * SSM_HEAD_DIM].astype(BF16)
            ys = []
            for h in (h0, h0 + 1):
                diff = a_cum[:, h:h + 1] - a_cum_t[h:h + 1, :]
                m = (gmat * jnp.exp(jnp.where(causal, diff, NEG_INF))).astype(BF16)
                ys.append(_dot(m, xp))
            pairs.append(jnp.where(first_head, ys[0], ys[1]))
        y_groups.append(jnp.concatenate(pairs, axis=1) + y_off)
    y = jnp.concatenate(y_groups, axis=1) + xs * dskip_ref[...]
    y = y * _silu(z_ref[...])
    o_ref[...] = _rms(y, nw_ref[...]).astype(BF16)


def _ssd(p, tail, tail_t, conv_w, conv_b, dtb_row, dtb_col, alog_row, alog_col, dskip_e, expand, norm_w):
    L = p.shape[0]
    Q = SSM_CHUNK
    bc_w = 2 * SSM_GROUPS * SSM_STATE
    xs_blk = D_SSM // D_SSM
    bc_blk = (2 * D_SSM) // bc_w
    hb = Q // 8
    full = lambda shape: pl.BlockSpec(shape, lambda c: (0,) * len(shape))
    return pl.pallas_call(
        _ssd_kernel,
        grid=(L // Q,),
        in_specs=[
            pl.BlockSpec((Q, D_SSM), lambda c: (c, xs_blk)),
            pl.BlockSpec((Q, bc_w), lambda c: (c, bc_blk)),
            pl.BlockSpec((8, D_SSM), lambda c: (jnp.maximum(c * hb - 1, 0), xs_blk)),
            pl.BlockSpec((8, bc_w), lambda c: (jnp.maximum(c * hb - 1, 0), bc_blk)),
            pl.BlockSpec((Q, D_SSM), lambda c: (c, 0)),
            pl.BlockSpec((Q, TAIL), lambda c: (c, 0)),
            pl.BlockSpec((TAIL, Q), lambda c: (0, c)),
            full(conv_w.shape), full(conv_b.shape), full(dtb_row.shape), full(dtb_col.shape),
            full(alog_row.shape), full(alog_col.shape), full(dskip_e.shape), full(expand.shape),
            full(norm_w.shape),
        ],
        out_specs=pl.BlockSpec((Q, D_SSM), lambda c: (c, 0)),
        out_shape=jax.ShapeDtypeStruct((L, D_SSM), BF16),
        scratch_shapes=[
            pltpu.VMEM((SSM_GROUPS, SSM_STATE, D_SSM // SSM_GROUPS), F32),
            pltpu.VMEM((Q + 8, D_SSM), F32),
            pltpu.VMEM((Q + 8, bc_w), F32),
        ],
        compiler_params=_params(("arbitrary",), 48),
        name="ssd_scan",
    )(p, p, p, p, p, tail, tail_t, conv_w, conv_b, dtb_row, dtb_col, alog_row, alog_col, dskip_e, expand, norm_w)


GATE_ROW0 = SSM_HEADS


def _gate_row(tailt_ref, branch, head):
    return _sigmoid(tailt_ref[pl.ds(GATE_ROW0 + branch * N_HEADS + head, 1), :])


def _cmp_win_kernel(slopes_ref, qt_ref, kc_ref, vct_ref, kw_ref, vwt_ref, tailt_ref,
                    ocw_ref, selt_ref, any_ref, s_ref, psum_ref, *, n_cmp_rows, n_blk, n_sel, chunk):
    g = pl.program_id(0)
    qb = pl.program_id(1)
    T = ATT_TQ
    R = Q_PER_KV
    CH = chunk
    t0 = qb * T
    qt4 = jnp.concatenate([qt_ref[r] for r in range(R)], axis=1)
    t_row = t0 + lax.broadcasted_iota(jnp.int32, (1, T), 1)
    sl = [slopes_ref[g * R + r] * LOG2E for r in range(R)]

    n_chunks = (t0 + T - CMP_LEN) // CMP_STRIDE // CH + 1
    end_rel = lax.broadcasted_iota(jnp.int32, (CH, T), 0) * CMP_STRIDE + (CMP_LEN - 1)
    end_rel_f = end_rel.astype(F32)

    def scores(c, m):
        r0 = pl.multiple_of(c * CH, CH)
        s = _dot(kc_ref[pl.ds(r0, CH), :], qt4)
        valid = (t_row - r0 * CMP_STRIDE) >= end_rel
        key_rel = end_rel_f + (r0 * CMP_STRIDE - t0).astype(F32)
        ms = []
        for r in range(R):
            v = jnp.where(valid, s[:, r * T:(r + 1) * T] + sl[r] * key_rel, NEG_INF)
            s_ref[pl.ds(r0, CH), r * T:(r + 1) * T] = v
            ms.append(jnp.max(v, axis=0, keepdims=True))
        return jnp.maximum(m, jnp.concatenate(ms, axis=1))

    m = lax.fori_loop(0, n_chunks, scores, jnp.full((1, R * T), M_INIT, F32))

    def probs(c, carry):
        l, acc = carry
        r0 = pl.multiple_of(c * CH, CH)
        p = jnp.exp2(s_ref[pl.ds(r0, CH), :] - m)
        s_ref[pl.ds(r0, CH), :] = p
        l = l + jnp.sum(p, axis=0, keepdims=True)
        acc = acc + _dot(vct_ref[:, pl.ds(r0, CH)], p.astype(BF16))
        return l, acc

    l, acc = lax.fori_loop(0, n_chunks, probs, (jnp.zeros((1, R * T), F32), jnp.zeros((HEAD_DIM, R * T), F32)))
    inv = jnp.where(l > 0.0, 1.0 / l, 0.0)

    PAD = 8
    LW = 128
    for u in range(T // LW):
        psum_ref[u, 0:PAD, :] = jnp.zeros((PAD, LW), F32)
    for c in range(n_cmp_rows // CH):
        @pl.when(c < n_chunks)
        def _():
            pn = s_ref[c * CH:(c + 1) * CH, :] * inv
            tot = pn[:, 0:T]
            for r in range(1, R):
                tot = tot + pn[:, r * T:(r + 1) * T]
            for u in range(T // LW):
                psum_ref[u, PAD + c * CH:PAD + (c + 1) * CH, :] = tot[:, u * LW:(u + 1) * LW]

        @pl.when(c >= n_chunks)
        def _():
            for u in range(T // LW):
                psum_ref[u, PAD + c * CH:PAD + (c + 1) * CH, :] = jnp.zeros((CH, LW), F32)

    ratio, b_r = SEL_BLOCK // CMP_STRIDE, CMP_LEN // CMP_STRIDE
    imp = jnp.zeros((n_blk, T), F32)
    for shift in range(ratio + b_r - 1):
        mult = sum(1 for mm in range(ratio) for nn in range(b_r) if mm + nn == shift)
        rows = pl.ds(PAD + ratio - 1 - shift, n_blk, stride=ratio)
        imp = imp + float(mult) * jnp.concatenate([psum_ref[u, rows, :] for u in range(T // LW)], axis=1)

    jb = lax.broadcasted_iota(jnp.int32, (n_blk, T), 0)
    jt = t_row // SEL_BLOCK
    forced = (jb == 0) | (jb == jt) | (jb == jt - 1)
    quota = (n_sel - 1 - jnp.minimum(jt, 2)).astype(F32)
    work0 = jnp.where(forced, NEG_INF, jnp.where(jb <= jt, imp, -1.0))

    n_rounds = n_sel - jnp.where(t0 >= 2 * SEL_BLOCK, 3, 1)

    def select_among(rows):
        jbf_v = lax.broadcasted_iota(jnp.int32, (rows, T), 0).astype(F32)

        def pick(i, work):
            top = jnp.max(work, axis=0, keepdims=True)
            first = jnp.min(jnp.where(work == top, jbf_v, float(n_blk)), axis=0, keepdims=True)
            first = jnp.where(i.astype(F32) < quota, first, -1.0)
            return jnp.where(jbf_v == first, NEG_INF, work)

        work = lax.fori_loop(0, n_rounds, pick, work0[:rows])
        selt_ref[0:rows, :] = jnp.where(work == NEG_INF, 1.0, 0.0)
        if rows < n_blk:
            selt_ref[rows:n_blk, :] = jnp.zeros((n_blk - rows, T), F32)

    n_var = 4
    sec_rows = n_blk // n_var
    section = jnp.minimum(((t0 + T) // SEL_BLOCK - 1) // sec_rows, n_var - 1)
    for var in range(n_var):
        pl.when(section == var)(functools.partial(select_among, (var + 1) * sec_rows))
    any_ref[...] = _dot_nt(jnp.ones((8, T), BF16), selt_ref[...].astype(BF16))

    wlen = WINDOW + T
    start = pl.multiple_of(jnp.maximum(t0 - WINDOW, 0), T)
    s = _dot(kw_ref[pl.ds(start, wlen), :], qt4)
    krow = lax.broadcasted_iota(jnp.int32, (wlen, T), 0)
    dist = (t_row - start) - krow
    valid = (dist >= 0) & (dist < WINDOW)
    key_rel = (krow + (start - t0)).astype(F32)
    vwt = vwt_ref[:, pl.ds(start, wlen)]
    for r in range(R):
        cs = slice(r * T, (r + 1) * T)
        v = jnp.where(valid, s[:, cs] + sl[r] * key_rel, NEG_INF)
        p = jnp.exp2(v - jnp.max(v, axis=0, keepdims=True))
        lw = jnp.sum(p, axis=0, keepdims=True)
        ow = _dot(vwt, p.astype(BF16))
        head = g * R + r
        o_t = acc[:, cs] * (inv[:, cs] * _gate_row(tailt_ref, 0, head)) + ow * (_gate_row(tailt_ref, 2, head) / lw)
        ocw_ref[:, r * HEAD_DIM:(r + 1) * HEAD_DIM] = o_t.T


def _cmp_win_attention(slopes, t_out, kc, vct, k_nat, tail_t, n_sel):
    L = t_out.shape[2]
    n_cmp_rows = L // CMP_STRIDE
    n_blk = L // SEL_BLOCK
    nqb = L // ATT_TQ
    chunk = min(256, n_cmp_rows)
    kern = functools.partial(_cmp_win_kernel, n_cmp_rows=n_cmp_rows, n_blk=n_blk, n_sel=n_sel, chunk=chunk)
    vw_head0 = (D_ATTN + KV_DIM) // HEAD_DIM
    return pl.pallas_call(
        kern,
        grid=(N_KV_HEADS, nqb),
        in_specs=[
            pl.BlockSpec(memory_space=pltpu.SMEM),
            pl.BlockSpec((Q_PER_KV, HEAD_DIM, ATT_TQ), lambda g, i: (g, 0, i)),
            pl.BlockSpec((None, n_cmp_rows, HEAD_DIM), lambda g, i: (g, 0, 0)),
            pl.BlockSpec((None, HEAD_DIM, n_cmp_rows), lambda g, i: (g, 0, 0)),
            pl.BlockSpec((None, L, HEAD_DIM), lambda g, i: (N_KV_HEADS + g, 0, 0)),
            pl.BlockSpec((None, HEAD_DIM, L), lambda g, i: (vw_head0 + g, 0, 0)),
            pl.BlockSpec((TAIL, ATT_TQ), lambda g, i: (0, i)),
        ],
        out_specs=[
            pl.BlockSpec((ATT_TQ, KV_DIM), lambda g, i: (i, g)),
            pl.BlockSpec((None, None, n_blk, ATT_TQ), lambda g, i: (g, i, 0, 0)),
            pl.BlockSpec((None, None, 8, n_blk), lambda g, i: (g, i, 0, 0)),
        ],
        out_shape=[
            jax.ShapeDtypeStruct((L, D_ATTN), F32),
            jax.ShapeDtypeStruct((N_KV_HEADS, nqb, n_blk, ATT_TQ), F32),
            jax.ShapeDtypeStruct((N_KV_HEADS, nqb, 8, n_blk), F32),
        ],
        scratch_shapes=[
            pltpu.VMEM((n_cmp_rows, Q_PER_KV * ATT_TQ), F32),
            pltpu.VMEM((ATT_TQ // 128, n_cmp_rows + 8, 128), F32),
        ],
        compiler_params=_params(("arbitrary", "arbitrary"), 48),
        name="cmp_win_attention",
    )(slopes, t_out, kc, vct, k_nat, t_out, tail_t)


SEL_TK = 512


def _sel_kernel(tiles_ref, counts_ref, slopes_ref, qt_ref, ks_ref, vst_ref, selt_ref, tailt_ref, ocw_ref, o_ref,
                m_ref, l_ref, acc_ref, *, n_tiles_max):
    g = pl.program_id(0)
    qb = pl.program_id(1)
    T = ATT_TQ
    R = Q_PER_KV
    TK = SEL_TK
    bpt = TK // SEL_BLOCK
    t0 = qb * T
    qt4 = jnp.concatenate([qt_ref[r] for r in range(R)], axis=1)
    t_row = t0 + lax.broadcasted_iota(jnp.int32, (1, T), 1)
    sl = [slopes_ref[g * R + r] * LOG2E for r in range(R)]
    krow = lax.broadcasted_iota(jnp.int32, (TK, T), 0)
    krow_f = krow.astype(F32)

    m_ref[...] = jnp.full(m_ref.shape, M_INIT, F32)
    l_ref[...] = jnp.zeros(l_ref.shape, F32)
    acc_ref[...] = jnp.zeros(acc_ref.shape, F32)
    step = g * pl.num_programs(1) + qb

    def tile(n, carry):
        kt = tiles_ref[step * n_tiles_max + n]
        k0 = pl.multiple_of(kt * TK, TK)
        s = _dot(ks_ref[pl.ds(k0, TK), :], qt4)
        b0 = pl.multiple_of(kt * bpt, bpt)
        sel8 = selt_ref[pl.ds(b0, bpt), :]
        chosen = jnp.concatenate(
            [jnp.broadcast_to(sel8[c:c + 1, :], (SEL_BLOCK, T)) for c in range(bpt)], axis=0)
        t_rel = t_row - k0
        valid = (chosen > 0.5) & (krow <= t_rel)
        t_rel_f = t_rel.astype(F32)
        m_old = m_ref[...]
        vs, ms, cols = [], [], []
        for r in range(R):
            v = jnp.where(valid, s[:, r * T:(r + 1) * T] + sl[r] * krow_f, NEG_INF)
            vs.append(v)
            cols.append(sl[r] * t_rel_f)
            ms.append(jnp.max(v, axis=0, keepdims=True))
        col = jnp.concatenate(cols, axis=1)
        m_new = jnp.maximum(m_old, jnp.concatenate(ms, axis=1) - col)
        shift = m_new + col
        alpha = jnp.exp2(m_old - m_new)
        p = jnp.concatenate([jnp.exp2(vs[r] - shift[:, r * T:(r + 1) * T]) for r in range(R)], axis=1)
        l_ref[...] = alpha * l_ref[...] + jnp.sum(p, axis=0, keepdims=True)
        acc_ref[...] = alpha * acc_ref[...] + _dot(vst_ref[:, pl.ds(k0, TK)], p.astype(BF16))
        m_ref[...] = m_new
        return carry

    lax.fori_loop(0, counts_ref[step], tile, 0)
    l = l_ref[...]
    inv = jnp.where(l > 0.0, 1.0 / l, 0.0)
    for r in range(R):
        cs = slice(r * T, (r + 1) * T)
        o_t = acc_ref[:, cs] * (inv[:, cs] * _gate_row(tailt_ref, 1, g * R + r))
        hs = slice(r * HEAD_DIM, (r + 1) * HEAD_DIM)
        o_ref[:, hs] = ocw_ref[:, hs] + o_t.T


def _active_tiles(blk_count, L):
    G = blk_count.shape[0]
    nqb = L // ATT_TQ
    n_t = L // SEL_TK
    flags = blk_count.reshape(G, nqb, n_t, SEL_TK // SEL_BLOCK).max(axis=-1) > 0.5
    kt = jnp.arange(n_t, dtype=jnp.int32)
    last = (jnp.arange(nqb, dtype=jnp.int32) * ATT_TQ + ATT_TQ - 1) // SEL_TK
    flags = flags & (kt[None, None, :] <= last[None, :, None])
    order = jnp.sort(jnp.where(flags, kt, n_t + kt), axis=-1)
    tiles = jnp.where(order < n_t, order, 0).astype(jnp.int32)
    return tiles.reshape(-1), flags.sum(axis=-1).astype(jnp.int32).reshape(-1)


def _sel_attention(slopes, t_out, k_nat, selt, blk_count, tail_t, ocw):
    L = t_out.shape[2]
    n_blk = L // SEL_BLOCK
    n_t = L // SEL_TK
    cols = Q_PER_KV * ATT_TQ
    vs_head0 = D_ATTN // HEAD_DIM
    tiles, counts = _active_tiles(blk_count, L)
    grid_spec = pltpu.PrefetchScalarGridSpec(
        num_scalar_prefetch=2,
        grid=(N_KV_HEADS, L // ATT_TQ),
        in_specs=[
            pl.BlockSpec(memory_space=pltpu.SMEM),
            pl.BlockSpec((Q_PER_KV, HEAD_DIM, ATT_TQ), lambda g, i, *_: (g, 0, i)),
            pl.BlockSpec((None, L, HEAD_DIM), lambda g, i, *_: (g, 0, 0)),
            pl.BlockSpec((None, HEAD_DIM, L), lambda g, i, *_: (vs_head0 + g, 0, 0)),
            pl.BlockSpec((None, None, n_blk, ATT_TQ), lambda g, i, *_: (g, i, 0, 0)),
            pl.BlockSpec((TAIL, ATT_TQ), lambda g, i, *_: (0, i)),
            pl.BlockSpec((ATT_TQ, KV_DIM), lambda g, i, *_: (i, g)),
        ],
        out_specs=pl.BlockSpec((ATT_TQ, KV_DIM), lambda g, i, *_: (i, g)),
        scratch_shapes=[
            pltpu.VMEM((1, cols), F32),
            pltpu.VMEM((1, cols), F32),
            pltpu.VMEM((HEAD_DIM, cols), F32),
        ],
    )
    return pl.pallas_call(
        functools.partial(_sel_kernel, n_tiles_max=n_t),
        grid_spec=grid_spec,
        out_shape=jax.ShapeDtypeStruct((L, D_ATTN), F32),
        compiler_params=_params(("arbitrary", "arbitrary"), 48),
        name="selected_attention",
    )(tiles, counts, slopes, t_out, k_nat, t_out, selt, tail_t, ocw)


def _outproj_kernel(x_ref, oa_ref, ys_ref, nw_ref, wa_ref, ws_ref, o_ref, ya_ref):
    @pl.when(pl.program_id(1) == 0)
    def _():
        ya_ref[...] = _rms(oa_ref[...], nw_ref[...]).astype(BF16)

    o_ref[...] = x_ref[...] + _dot(ya_ref[...], wa_ref[...]) + _dot(ys_ref[...], ws_ref[...])


def _out_projection(x2, o_attn, ys, attn_norm_w, w_out, tm, tn):
    L = x2.shape[0]
    n_j = D_MODEL // tn
    return pl.pallas_call(
        _outproj_kernel,
        grid=(L // tm, n_j),
        in_specs=[
            pl.BlockSpec((tm, tn), lambda i, j: (i, j)),
            pl.BlockSpec((tm, D_ATTN), lambda i, j: (i, 0)),
            pl.BlockSpec((tm, D_SSM), lambda i, j: (i, 0)),
            pl.BlockSpec((1, D_ATTN), lambda i, j: (0, 0)),
            pl.BlockSpec((D_ATTN, tn), lambda i, j: (0, j)),
            pl.BlockSpec((D_SSM, tn), lambda i, j: (1, j)),
        ],
        out_specs=pl.BlockSpec((tm, tn), lambda i, j: (i, j)),
        out_shape=jax.ShapeDtypeStruct((L, D_MODEL), F32),
        scratch_shapes=[pltpu.VMEM((tm, D_ATTN), BF16)],
        compiler_params=_params(("arbitrary", "arbitrary"), 48),
        name="out_projection",
    )(x2, o_attn, ys, attn_norm_w, w_out, w_out)


FFN_HALO = 16


def _ffn_kernel(x_ref, halo_ref, nw_ref, wg_ref, wv_ref, cwg_ref, cwv_ref, cbg_ref, cbv_ref, wd_ref, fw_ref,
                o_ref, hn_ref, ug_ref, uv_ref, acc_ref, *, tm):
    i = pl.program_id(0)
    j = pl.program_id(1)
    H = FFN_HALO

    @pl.when(j == 0)
    def _():
        hn_ref[H:H + tm, :] = _rms(x_ref[...], nw_ref[...]).astype(BF16)
        acc_ref[...] = jnp.zeros_like(acc_ref)

    @pl.when((j == 0) & (i == 0))
    def _():
        hn_ref[0:H, :] = jnp.zeros((H, D_MODEL), BF16)

    @pl.when((j == 0) & (i > 0))
    def _():
        hn_ref[0:H, :] = _rms(halo_ref[...], nw_ref[...]).astype(BF16)

    hn = hn_ref[...]
    ug_ref[...] = _dot(hn, wg_ref[...])
    uv_ref[...] = _dot(hn, wv_ref[...])

    def conv(u_ref, w_ref, b_ref):
        acc = b_ref[...]
        for k in range(FFN_CONV):
            acc = acc + u_ref[pl.ds(H - (FFN_CONV - 1) + k, tm), :] * w_ref[k:k + 1, :]
        return acc

    act = _silu(conv(ug_ref, cwg_ref, cbg_ref)) * conv(uv_ref, cwv_ref, cbv_ref)
    acc_ref[...] += _dot(act.astype(BF16), wd_ref[...])

    @pl.when(j == pl.num_programs(1) - 1)
    def _():
        o_ref[...] = _rms(x_ref[...] + acc_ref[...], fw_ref[...])


def _conv_ffn(x1, norm_w, w_up, conv_w, conv_b, w_down, final_w, tm, tf):
    L = x1.shape[0]
    n_f = D_FF // tf
    hb = tm // FFN_HALO
    return pl.pallas_call(
        functools.partial(_ffn_kernel, tm=tm),
        grid=(L // tm, n_f),
        in_specs=[
            pl.BlockSpec((tm, D_MODEL), lambda i, j: (i, 0)),
            pl.BlockSpec((FFN_HALO, D_MODEL), lambda i, j: (jnp.maximum(i * hb - 1, 0), 0)),
            pl.BlockSpec((1, D_MODEL), lambda i, j: (0, 0)),
            pl.BlockSpec((D_MODEL, tf), lambda i, j: (0, j)),
            pl.BlockSpec((D_MODEL, tf), lambda i, j: (0, n_f + j)),
            pl.BlockSpec((FFN_CONV, tf), lambda i, j: (0, j)),
            pl.BlockSpec((FFN_CONV, tf), lambda i, j: (0, n_f + j)),
            pl.BlockSpec((1, tf), lambda i, j: (0, j)),
            pl.BlockSpec((1, tf), lambda i, j: (0, n_f + j)),
            pl.BlockSpec((tf, D_MODEL), lambda i, j: (j, 0)),
            pl.BlockSpec((1, D_MODEL), lambda i, j: (0, 0)),
        ],
        out_specs=pl.BlockSpec((tm, D_MODEL), lambda i, j: (i, 0)),
        out_shape=jax.ShapeDtypeStruct((L, D_MODEL), F32),
        scratch_shapes=[
            pltpu.VMEM((tm + FFN_HALO, D_MODEL), BF16),
            pltpu.VMEM((tm + FFN_HALO, tf), F32),
            pltpu.VMEM((tm + FFN_HALO, tf), F32),
            pltpu.VMEM((tm, D_MODEL), F32),
        ],
        compiler_params=_params(("arbitrary", "arbitrary"), 56),
        name="conv_ffn",
    )(x1, x1, norm_w, w_up, w_up, conv_w, conv_w, conv_b, conv_b, w_down, final_w)


def _head_expand():
    e = np.zeros((TAIL, D_SSM), np.float32)
    for h in range(SSM_HEADS):
        e[h, h * SSM_HEAD_DIM:(h + 1) * SSM_HEAD_DIM] = 1.0
    return e


def _layer(x2, mix_norm_w, w_in, ssm_conv_w, ssm_conv_b, ssm_dt_bias, ssm_a_log, ssm_d, ssm_norm_w,
           cmp_pos, cmp_w1, cmp_w2_k, cmp_w2_v, attn_norm_w, w_out, ffn_norm_w, w_up, ffn_conv_w, ffn_conv_b,
           w_down, final_norm_w):
    L = x2.shape[0]
    row = lambda v: v.reshape(1, -1).astype(F32)

    bounds = np.cumsum([0, D_SSM, D_SSM + 2 * SSM_GROUPS * SSM_STATE, SSM_HEADS, D_ATTN] + [KV_DIM] * 6
                       + [3 * N_HEADS])
    seg = lambda k: w_in[:, bounds[k]:bounds[k + 1]]
    z_w, xbc_w, dt_w, q_w, kc_w, vc_w, ks_w, vs_w, kw_w, vw_w, gl_w = [seg(k) for k in range(11)]
    w_nat = jnp.concatenate([z_w, xbc_w, kc_w, vc_w, ks_w, kw_w], axis=1).astype(BF16)
    w_tr = jnp.concatenate([q_w, vs_w, vw_w], axis=1).T.astype(BF16)
    w_tail = jnp.concatenate([dt_w, gl_w, jnp.zeros((D_MODEL, TAIL - SSM_HEADS - 3 * N_HEADS), F32)], axis=1)
    w_tail = w_tail.astype(BF16)
    pad_row = lambda v: jnp.concatenate([v.astype(F32), jnp.zeros((TAIL - SSM_HEADS,), F32)]).reshape(1, TAIL)

    p, k_nat, t_out, tail, tail_t = _in_projection(x2, row(mix_norm_w), w_nat, w_tr, w_tail, w_tail.T,
                                                   tm=min(1024, L))
    kc, vct = _compress(p, cmp_pos.astype(F32), cmp_w1.astype(BF16), cmp_w2_k.astype(BF16),
                        cmp_w2_v.T.astype(BF16))

    y_ssm = _ssd(p, tail, tail_t, ssm_conv_w.astype(F32), row(ssm_conv_b), pad_row(ssm_dt_bias),
                 ssm_dt_bias.astype(F32).reshape(SSM_HEADS, 1), pad_row(ssm_a_log),
                 ssm_a_log.astype(F32).reshape(SSM_HEADS, 1),
                 row(jnp.repeat(ssm_d.astype(F32), SSM_HEAD_DIM)), jnp.asarray(_head_expand(), BF16),
                 row(ssm_norm_w))

    n_blk = L // SEL_BLOCK
    slopes = 2.0 ** (-8.0 * jnp.arange(1, N_HEADS + 1, dtype=F32) / N_HEADS)
    ocw, selt, blk_count = _cmp_win_attention(slopes, t_out, kc, vct, k_nat, tail_t, min(N_SEL, n_blk))
    o_attn = _sel_attention(slopes, t_out, k_nat, selt, blk_count[:, :, 0, :], tail_t, ocw)

    x1 = _out_projection(x2, o_attn, y_ssm, row(attn_norm_w), w_out.astype(BF16), tm=512, tn=1024)
    return _conv_ffn(x1, row(ffn_norm_w), w_up.astype(BF16), ffn_conv_w.astype(F32), row(ffn_conv_b),
                     w_down.astype(BF16), row(final_norm_w), tm=512, tf=512)


def kernel(x, mix_norm_w, w_in, ssm_conv_w, ssm_conv_b, ssm_dt_bias, ssm_a_log, ssm_d, ssm_norm_w, cmp_pos_k,
           cmp_w1_k, cmp_w2_k, cmp_pos_v, cmp_w1_v, cmp_w2_v, attn_norm_w, w_out, ffn_norm_w, w_up, ffn_conv_w,
           ffn_conv_b, w_down, final_norm_w):
    bsz, L, _ = x.shape
    assert bsz == 1 and mix_norm_w.shape[0] == 1, "single sequence, single layer"
    assert L % 1024 == 0 and L >= WINDOW + ATT_TQ
    out = _layer(
        x[0], mix_norm_w[0], w_in[0], ssm_conv_w[0], ssm_conv_b[0], ssm_dt_bias[0], ssm_a_log[0], ssm_d[0],
        ssm_norm_w[0], jnp.stack([cmp_pos_k[0], cmp_pos_v[0]]), jnp.stack([cmp_w1_k[0], cmp_w1_v[0]]),
        cmp_w2_k[0], cmp_w2_v[0], attn_norm_w[0], w_out[0], ffn_norm_w[0], w_up[0], ffn_conv_w[0],
        ffn_conv_b[0], w_down[0], final_norm_w)
    return out[None]
```

```python
import functools

import numpy as np
import jax
import jax.numpy as jnp
from jax import lax
from jax.experimental import pallas as pl
from jax.experimental.pallas import tpu as pltpu

F32 = jnp.float32
BF16 = jnp.bfloat16

D_MODEL = 2048
D_ATTN = 2048
D_SSM = 2048
N_HEADS = 16
HEAD_DIM = 128
N_KV_HEADS = 4
Q_PER_KV = 4
KV_DIM = 512
CMP_LEN = 32
CMP_STRIDE = 16
CMP_HIDDEN = 256
SEL_BLOCK = 64
N_SEL = 16
WINDOW = 512
ATT_TQ = 256
SSM_HEAD_DIM = 64
SSM_HEADS = 32
SSM_GROUPS = 4
SSM_STATE = 128
SSM_CONV = 4
SSM_CHUNK = 256
D_FF = 5632
FFN_CONV = 3
NORM_EPS = 1e-6
NEG_INF = -1e30
FORCE_SCORE = 1e4

LOG2E = 1.4426950408889634
M_INIT = -1e20

P_COLS = 6144
K_COLS = 1024
T_ROWS = 3072
TAIL = 128
PROJ_TN = 1024
KC_COL0 = 5120
VC_COL0 = 5632
MIB = 1024 * 1024


def _params(sem, vmem_mib):
    return pltpu.CompilerParams(dimension_semantics=sem, vmem_limit_bytes=vmem_mib * MIB)


def _rms(x, w):
    return x * lax.rsqrt(jnp.mean(x * x, axis=-1, keepdims=True) + NORM_EPS) * w


def _dot(a, b):
    return jnp.dot(a, b, preferred_element_type=F32)


def _dot_nt(a, b):
    return lax.dot_general(a, b, (((1,), (1,)), ((), ())), preferred_element_type=F32)


def _split3(x):
    hi = x.astype(BF16)
    r1 = x - hi.astype(F32)
    mid = r1.astype(BF16)
    lo = (r1 - mid.astype(F32)).astype(BF16)
    return hi, mid, lo


def _dot3(x, m):
    hi, mid, lo = _split3(x)
    return _dot(hi, m) + _dot(mid, m) + _dot(lo, m)


def _dot3_left(m, x):
    hi, mid, lo = _split3(x)
    return _dot(m, hi) + _dot(m, mid) + _dot(m, lo)


def _silu(x):
    return x * (1.0 / (1.0 + jnp.exp(-x)))


def _sigmoid(x):
    return 1.0 / (1.0 + jnp.exp(-x))


def _inproj_kernel(x_ref, nw_ref, w_ref, wtr_ref, wt_ref, wtt_ref, p_ref, k_ref, t_ref, tail_ref, tailt_ref,
                   xn_ref, *, n_p, n_q):
    j = pl.program_id(1)
    hp = PROJ_TN // HEAD_DIM

    @pl.when(j == 0)
    def _():
        xn = _rms(x_ref[...], nw_ref[...]).astype(BF16)
        xn_ref[...] = xn
        tail_ref[...] = _dot(xn, wt_ref[...])
        tailt_ref[...] = _dot_nt(wtt_ref[...], xn)

    @pl.when(j <= n_p)
    def _():
        r = _dot(xn_ref[...], w_ref[...])

        @pl.when(j < n_p)
        def _():
            p_ref[...] = r

        @pl.when(j == n_p)
        def _():
            for c in range(hp):
                k_ref[c] = r[:, c * HEAD_DIM:(c + 1) * HEAD_DIM].astype(BF16)

    @pl.when(j > n_p)
    def _():
        r = _dot_nt(wtr_ref[...], xn_ref[...])
        r = r * jnp.where(j - n_p - 1 < n_q, HEAD_DIM ** -0.5 * LOG2E, 1.0)
        for c in range(hp):
            t_ref[c] = r[c * HEAD_DIM:(c + 1) * HEAD_DIM, :].astype(BF16)


def _in_projection(x2, norm_w, w_nat, w_tr, w_tail, w_tail_t, tm):
    L = x2.shape[0]
    n_p = P_COLS // PROJ_TN
    n_t = T_ROWS // PROJ_TN
    hp = PROJ_TN // HEAD_DIM
    return pl.pallas_call(
        functools.partial(_inproj_kernel, n_p=n_p, n_q=D_ATTN // PROJ_TN),
        grid=(L // tm, n_p + 1 + n_t),
        in_specs=[
            pl.BlockSpec((tm, D_MODEL), lambda i, j: (i, 0), pipeline_mode=pl.Buffered(1)),
            pl.BlockSpec((1, D_MODEL), lambda i, j: (0, 0)),
            pl.BlockSpec((D_MODEL, PROJ_TN), lambda i, j: (0, jnp.minimum(j, n_p))),
            pl.BlockSpec((PROJ_TN, D_MODEL), lambda i, j: (jnp.clip(j - n_p - 1, 0, n_t - 1), 0)),
            pl.BlockSpec((D_MODEL, TAIL), lambda i, j: (0, 0)),
            pl.BlockSpec((TAIL, D_MODEL), lambda i, j: (0, 0)),
        ],
        out_specs=[
            pl.BlockSpec((tm, PROJ_TN), lambda i, j: (i, jnp.minimum(j, n_p - 1))),
            pl.BlockSpec((hp, tm, HEAD_DIM), lambda i, j: (0, i, 0)),
            pl.BlockSpec((hp, HEAD_DIM, tm), lambda i, j: (jnp.clip(j - n_p - 1, 0, n_t - 1), 0, i)),
            pl.BlockSpec((tm, TAIL), lambda i, j: (i, 0)),
            pl.BlockSpec((TAIL, tm), lambda i, j: (0, i)),
        ],
        out_shape=[
            jax.ShapeDtypeStruct((L, P_COLS), F32),
            jax.ShapeDtypeStruct((K_COLS // HEAD_DIM, L, HEAD_DIM), BF16),
            jax.ShapeDtypeStruct((T_ROWS // HEAD_DIM, HEAD_DIM, L), BF16),
            jax.ShapeDtypeStruct((L, TAIL), F32),
            jax.ShapeDtypeStruct((TAIL, L), F32),
        ],
        scratch_shapes=[pltpu.VMEM((tm, D_MODEL), BF16)],
        compiler_params=_params(("arbitrary", "arbitrary"), 56),
        name="in_projection",
    )(x2, norm_w, w_nat, w_tr, w_tail, w_tail_t)


def _compress_hidden(x_ref, pos_ref, w1_ref, n_rows):
    half = CMP_LEN // 2
    acc0 = jnp.zeros((n_rows, CMP_HIDDEN), F32)
    acc1 = jnp.zeros((n_rows, CMP_HIDDEN), F32)
    for i in range(half):
        xi = x_ref[pl.ds(i, n_rows, stride=CMP_STRIDE), :]
        a0 = (xi + pos_ref[i:i + 1, :]).astype(BF16)
        a1 = (xi + pos_ref[half + i:half + i + 1, :]).astype(BF16)
        acc0 = acc0 + _dot(a0, w1_ref[i * HEAD_DIM:(i + 1) * HEAD_DIM, :])
        acc1 = acc1 + _dot(a1, w1_ref[(half + i) * HEAD_DIM:(half + i + 1) * HEAD_DIM, :])
    hid = acc0 + pltpu.roll(acc1, n_rows - 1, 0)
    return jax.nn.gelu(hid).astype(BF16)


def _compress_kernel(xk_ref, xv_ref, pos_ref, w1_ref, w2k_ref, w2vt_ref, kc_ref, vct_ref, *, n_rows):
    hk = _compress_hidden(xk_ref, pos_ref.at[0], w1_ref.at[0], n_rows)
    kc_ref[...] = _dot(hk, w2k_ref[...]).astype(BF16)
    hv = _compress_hidden(xv_ref, pos_ref.at[1], w1_ref.at[1], n_rows)
    vct_ref[...] = _dot_nt(w2vt_ref[...], hv).astype(BF16)


def _compress(p, pos, w1, w2k, w2vt):
    L = p.shape[0]
    n_rows = L // CMP_STRIDE
    full = lambda a: pl.BlockSpec(a.shape, lambda g: (0,) * a.ndim)
    return pl.pallas_call(
        functools.partial(_compress_kernel, n_rows=n_rows),
        grid=(N_KV_HEADS,),
        in_specs=[
            pl.BlockSpec((L, HEAD_DIM), lambda g: (0, KC_COL0 // HEAD_DIM + g)),
            pl.BlockSpec((L, HEAD_DIM), lambda g: (0, VC_COL0 // HEAD_DIM + g)),
            full(pos), full(w1), full(w2k), full(w2vt),
        ],
        out_specs=[
            pl.BlockSpec((None, n_rows, HEAD_DIM), lambda g: (g, 0, 0)),
            pl.BlockSpec((None, HEAD_DIM, n_rows), lambda g: (g, 0, 0)),
        ],
        out_shape=[
            jax.ShapeDtypeStruct((N_KV_HEADS, n_rows, HEAD_DIM), BF16),
            jax.ShapeDtypeStruct((N_KV_HEADS, HEAD_DIM, n_rows), BF16),
        ],
        compiler_params=_params(("arbitrary",), 56),
        name="kv_compress",
    )(p, p, pos, w1, w2k, w2vt)


def _ssd_kernel(xs_ref, bc_ref, xs_halo_ref, bc_halo_ref, z_ref, tail_ref, tailt_ref, cw_ref, cb_ref,
                dtb_row_ref, dtb_col_ref, alog_row_ref, alog_col_ref, dskip_ref, expand_ref, nw_ref,
                o_ref, state_ref, cx_ref, cbc_ref):
    c = pl.program_id(0)
    Q = SSM_CHUNK
    GW = D_SSM // SSM_GROUPS
    halo = 8

    @pl.when(c == 0)
    def _():
        state_ref[...] = jnp.zeros_like(state_ref)
        cx_ref[0:halo, :] = jnp.zeros((halo, D_SSM), F32)
        cbc_ref[0:halo, :] = jnp.zeros((halo, 2 * SSM_GROUPS * SSM_STATE), F32)

    @pl.when(c > 0)
    def _():
        cx_ref[0:halo, :] = xs_halo_ref[...]
        cbc_ref[0:halo, :] = bc_halo_ref[...]

    cx_ref[halo:halo + Q, :] = xs_ref[...]
    cbc_ref[halo:halo + Q, :] = bc_ref[...]

    def conv_silu(ref, w, b):
        acc = b
        for k in range(SSM_CONV):
            acc = acc + ref[pl.ds(halo - (SSM_CONV - 1) + k, Q), :] * w[k:k + 1, :]
        return _silu(acc)

    cw = cw_ref[...]
    cb = cb_ref[...]
    xs = conv_silu(cx_ref, cw[:, :D_SSM], cb[:, :D_SSM])
    bcm = conv_silu(cbc_ref, cw[:, D_SSM:], cb[:, D_SSM:])

    def softplus(v):
        return jnp.maximum(v, 0.0) + jnp.log1p(jnp.exp(-jnp.abs(v)))

    dt = softplus(tail_ref[...] + dtb_row_ref[...])
    a = dt * (-jnp.exp(alog_row_ref[...]))
    rows = lax.broadcasted_iota(jnp.int32, (Q, Q), 0)
    cols = lax.broadcasted_iota(jnp.int32, (Q, Q), 1)
    causal = cols <= rows
    tri = jnp.where(causal, 1.0, 0.0).astype(BF16)
    a_cum = _dot3_left(tri, a)
    dt_t = softplus(tailt_ref[0:SSM_HEADS, :] + dtb_col_ref[...])
    a_t = dt_t * (-jnp.exp(alog_col_ref[...]))
    tri_t = jnp.where(rows <= cols, 1.0, 0.0).astype(BF16)
    a_cum_t = _dot3(a_t, tri_t)

    expand = expand_ref[...]
    dt_e = _dot3(dt, expand)
    ac_e = _dot3(a_cum, expand)
    xdt = xs * dt_e
    decay_to = jnp.exp(ac_e)
    a_last = ac_e[Q - 1:Q, :]
    xdd = xdt * jnp.exp(a_last - ac_e)
    chunk_decay = jnp.exp(a_last)

    lane = lax.broadcasted_iota(jnp.int32, (Q, 2 * SSM_HEAD_DIM), 1)
    first_head = lane < SSM_HEAD_DIM
    y_groups = []
    for g in range(SSM_GROUPS):
        bg = bcm[:, g * SSM_STATE:(g + 1) * SSM_STATE]
        cg = bcm[:, (SSM_GROUPS + g) * SSM_STATE:(SSM_GROUPS + g + 1) * SSM_STATE].astype(BF16)
        gmat = _dot_nt(cg, bg.astype(BF16))
        st = state_ref[g]
        y_off = _dot(cg, st.astype(BF16)) * decay_to[:, g * GW:(g + 1) * GW]
        new = _dot(bg.T.astype(BF16), xdd[:, g * GW:(g + 1) * GW].astype(BF16))
        state_ref[g] = st * chunk_decay[:, g * GW:(g + 1) * GW] + new
        pairs = []
        for pr in range(GW // (2 * SSM_HEAD_DIM)):
            h0 = g * (GW // SSM_HEAD_DIM) + 2 * pr
            xp = xdt[:, h0 * SSM_HEAD_DIM:(h0 + 2) * SSM_HEAD_DIM].astype(BF16)
            ys = []
            for h in (h0, h0 + 1):
                diff = a_cum[:, h:h + 1] - a_cum_t[h:h + 1, :]
                m = (gmat * jnp.exp(jnp.where(causal, diff, NEG_INF))).astype(BF16)
                ys.append(_dot(m, xp))
            pairs.append(jnp.where(first_head, ys[0], ys[1]))
        y_groups.append(jnp.concatenate(pairs, axis=1) + y_off)
    y = jnp.concatenate(y_groups, axis=1) + xs * dskip_ref[...]
    y = y * _silu(z_ref[...])
    o_ref[...] = _rms(y, nw_ref[...]).astype(BF16)


def _ssd(p, tail, tail_t, conv_w, conv_b, dtb_row, dtb_col, alog_row, alog_col, dskip_e, expand, norm_w):
    L = p.shape[0]
    Q = SSM_CHUNK
    bc_w = 2 * SSM_GROUPS * SSM_STATE
    xs_blk = D_SSM // D_SSM
    bc_blk = (2 * D_SSM) // bc_w
    hb = Q // 8
    full = lambda shape: pl.BlockSpec(shape, lambda c: (0,) * len(shape))
    return pl.pallas_call(
        _ssd_kernel,
        grid=(L // Q,),
        in_specs=[
            pl.BlockSpec((Q, D_SSM), lambda c: (c, xs_blk)),
            pl.BlockSpec((Q, bc_w), lambda c: (c, bc_blk)),
            pl.BlockSpec((8, D_SSM), lambda c: (jnp.maximum(c * hb - 1, 0), xs_blk)),
            pl.BlockSpec((8, bc_w), lambda c: (jnp.maximum(c * hb - 1, 0), bc_blk)),
            pl.BlockSpec((Q, D_SSM), lambda c: (c, 0)),
            pl.BlockSpec((Q, TAIL), lambda c: (c, 0)),
            pl.BlockSpec((TAIL, Q), lambda c: (0, c)),
            full(conv_w.shape), full(conv_b.shape), full(dtb_row.shape), full(dtb_col.shape),
            full(alog_row.shape), full(alog_col.shape), full(dskip_e.shape), full(expand.shape),
            full(norm_w.shape),
        ],
        out_specs=pl.BlockSpec((Q, D_SSM), lambda c: (c, 0)),
        out_shape=jax.ShapeDtypeStruct((L, D_SSM), BF16),
        scratch_shapes=[
            pltpu.VMEM((SSM_GROUPS, SSM_STATE, D_SSM // SSM_GROUPS), F32),
            pltpu.VMEM((Q + 8, D_SSM), F32),
            pltpu.VMEM((Q + 8, bc_w), F32),
        ],
        compiler_params=_params(("arbitrary",), 48),
        name="ssd_scan",
    )(p, p, p, p, p, tail, tail_t, conv_w, conv_b, dtb_row, dtb_col, alog_row, alog_col, dskip_e, expand, norm_w)


GATE_ROW0 = SSM_HEADS


def _gate_row(tailt_ref, branch, head):
    return _sigmoid(tailt_ref[pl.ds(GATE_ROW0 + branch * N_HEADS + head, 1), :])


POS_COARSE = 64


def _position_columns(n_keys, with_block_onehot):
    k = np.arange(n_keys)
    e = np.zeros((n_keys, HEAD_DIM), np.float32)
    e[:, 0:3] = ((k // POS_COARSE) * POS_COARSE)[:, None]
    e[:, 3:6] = (k % POS_COARSE)[:, None]
    if with_block_onehot:
        e[k, 8 + k // SEL_BLOCK] = 1.0
    return e


def _slope_rows(sl, T, extra8):
    n_heads = len(sl)
    lane_head = lax.broadcasted_iota(jnp.int32, (1, n_heads * T), 1) // T
    row = jnp.zeros((1, n_heads * T), F32)
    for r in range(n_heads):
        row = jnp.where(lane_head == r, sl[r], row)
    pieces = [x.astype(F32) for x in _split3(row)]
    top = jnp.concatenate(pieces + pieces + [jnp.zeros((2, n_heads * T), F32)], axis=0)
    return jnp.concatenate([top, extra8], axis=0).astype(BF16)


def _cmp_win_kernel(slopes_ref, qt_ref, kc_ref, vct_ref, kw_ref, vwt_ref, tailt_ref, posw_ref,
                    ocw_ref, selt_ref, any_ref, s_ref, psum_ref, *, n_cmp_rows, n_blk, n_sel, chunk):
    g = pl.program_id(0)
    qb = pl.program_id(1)
    T = ATT_TQ
    R = Q_PER_KV
    CH = chunk
    t0 = qb * T
    qt4 = jnp.concatenate([qt_ref[r] for r in range(R)], axis=1)
    t_row = t0 + lax.broadcasted_iota(jnp.int32, (1, T), 1)
    sl = [slopes_ref[g * R + r] * LOG2E for r in range(R)]

    n_chunks = (t0 + T - CMP_LEN) // CMP_STRIDE // CH + 1
    end_rel = lax.broadcasted_iota(jnp.int32, (CH, T), 0) * CMP_STRIDE + (CMP_LEN - 1)
    end_rel_f = end_rel.astype(F32)

    def scores(c, m):
        r0 = pl.multiple_of(c * CH, CH)
        s = _dot(kc_ref[pl.ds(r0, CH), :], qt4)
        valid = (t_row - r0 * CMP_STRIDE) >= end_rel
        key_rel = end_rel_f + (r0 * CMP_STRIDE - t0).astype(F32)
        ms = []
        for r in range(R):
            v = jnp.where(valid, s[:, r * T:(r + 1) * T] + sl[r] * key_rel, NEG_INF)
            s_ref[pl.ds(r0, CH), r * T:(r + 1) * T] = v
            ms.append(jnp.max(v, axis=0, keepdims=True))
        return jnp.maximum(m, jnp.concatenate(ms, axis=1))

    m = lax.fori_loop(0, n_chunks, scores, jnp.full((1, R * T), M_INIT, F32))

    def probs(c, carry):
        l, acc = carry
        r0 = pl.multiple_of(c * CH, CH)
        p = jnp.exp2(s_ref[pl.ds(r0, CH), :] - m)
        s_ref[pl.ds(r0, CH), :] = p
        l = l + jnp.sum(p, axis=0, keepdims=True)
        acc = acc + _dot(vct_ref[:, pl.ds(r0, CH)], p.astype(BF16))
        return l, acc

    l, acc = lax.fori_loop(0, n_chunks, probs, (jnp.zeros((1, R * T), F32), jnp.zeros((HEAD_DIM, R * T), F32)))
    inv = jnp.where(l > 0.0, 1.0 / l, 0.0)

    PAD = 8
    LW = 128
    for u in range(T // LW):
        psum_ref[u, 0:PAD, :] = jnp.zeros((PAD, LW), F32)
    for c in range(n_cmp_rows // CH):
        @pl.when(c < n_chunks)
        def _():
            pn = s_ref[c * CH:(c + 1) * CH, :] * inv
            tot = pn[:, 0:T]
            for r in range(1, R):
                tot = tot + pn[:, r * T:(r + 1) * T]
            for u in range(T // LW):
                psum_ref[u, PAD + c * CH:PAD + (c + 1) * CH, :] = tot[:, u * LW:(u + 1) * LW]

        @pl.when(c >= n_chunks)
        def _():
            for u in range(T // LW):
                psum_ref[u, PAD + c * CH:PAD + (c + 1) * CH, :] = jnp.zeros((CH, LW), F32)

    ratio, b_r = SEL_BLOCK // CMP_STRIDE, CMP_LEN // CMP_STRIDE
    imp = jnp.zeros((n_blk, T), F32)
    for shift in range(ratio + b_r - 1):
        mult = sum(1 for mm in range(ratio) for nn in range(b_r) if mm + nn == shift)
        rows = pl.ds(PAD + ratio - 1 - shift, n_blk, stride=ratio)
        imp = imp + float(mult) * jnp.concatenate([psum_ref[u, rows, :] for u in range(T // LW)], axis=1)

    jb = lax.broadcasted_iota(jnp.int32, (n_blk, T), 0)
    jt = t_row // SEL_BLOCK
    forced = (jb == 0) | (jb == jt) | (jb == jt - 1)
    quota = (n_sel - 1 - jnp.minimum(jt, 2)).astype(F32)
    work0 = jnp.where(forced, NEG_INF, jnp.where(jb <= jt, imp, -1.0))

    n_rounds = n_sel - jnp.where(t0 >= 2 * SEL_BLOCK, 3, 1)

    def select_among(rows):
        jbf_v = lax.broadcasted_iota(jnp.int32, (rows, T), 0).astype(F32)

        def pick(i, work):
            top = jnp.max(work, axis=0, keepdims=True)
            first = jnp.min(jnp.where(work == top, jbf_v, float(n_blk)), axis=0, keepdims=True)
            first = jnp.where(i.astype(F32) < quota, first, -1.0)
            return jnp.where(jbf_v == first, NEG_INF, work)

        work = lax.fori_loop(0, n_rounds, pick, work0[:rows])
        selt_ref[0:rows, :] = jnp.where(work == NEG_INF, 1.0, 0.0)
        if rows < n_blk:
            selt_ref[rows:n_blk, :] = jnp.zeros((n_blk - rows, T), F32)

    n_var = 4
    sec_rows = n_blk // n_var
    section = jnp.minimum(((t0 + T) // SEL_BLOCK - 1) // sec_rows, n_var - 1)
    for var in range(n_var):
        pl.when(section == var)(functools.partial(select_among, (var + 1) * sec_rows))
    any_ref[...] = _dot_nt(jnp.ones((8, T), BF16), selt_ref[...].astype(BF16))

    wlen = WINDOW + T
    start = pl.multiple_of(jnp.maximum(t0 - WINDOW, 0), T)
    q_aug = jnp.concatenate([qt4, _slope_rows(sl, T, jnp.zeros((8, R * T), F32)),
                             jnp.zeros((HEAD_DIM - 16, R * T), BF16)], axis=0)
    s = _dot(jnp.concatenate([kw_ref[pl.ds(start, wlen), :], posw_ref[...]], axis=1), q_aug)
    krow = lax.broadcasted_iota(jnp.int32, (wlen, T), 0)
    dist = (t_row - start) - krow
    valid = (dist >= 0) & (dist < WINDOW)
    vwt = vwt_ref[:, pl.ds(start, wlen)]
    for r in range(R):
        cs = slice(r * T, (r + 1) * T)
        v = jnp.where(valid, s[:, cs], NEG_INF)
        p = jnp.exp2(v - jnp.max(v, axis=0, keepdims=True))
        lw = jnp.sum(p, axis=0, keepdims=True)
        ow = _dot(vwt, p.astype(BF16))
        head = g * R + r
        o_t = acc[:, cs] * (inv[:, cs] * _gate_row(tailt_ref, 0, head)) + ow * (_gate_row(tailt_ref, 2, head) / lw)
        ocw_ref[:, r * HEAD_DIM:(r + 1) * HEAD_DIM] = o_t.T


def _cmp_win_attention(slopes, t_out, kc, vct, k_nat, tail_t, n_sel):
    L = t_out.shape[2]
    n_cmp_rows = L // CMP_STRIDE
    n_blk = L // SEL_BLOCK
    nqb = L // ATT_TQ
    chunk = min(256, n_cmp_rows)
    kern = functools.partial(_cmp_win_kernel, n_cmp_rows=n_cmp_rows, n_blk=n_blk, n_sel=n_sel, chunk=chunk)
    vw_head0 = (D_ATTN + KV_DIM) // HEAD_DIM
    return pl.pallas_call(
        kern,
        grid=(N_KV_HEADS, nqb),
        in_specs=[
            pl.BlockSpec(memory_space=pltpu.SMEM),
            pl.BlockSpec((Q_PER_KV, HEAD_DIM, ATT_TQ), lambda g, i: (g, 0, i)),
            pl.BlockSpec((None, n_cmp_rows, HEAD_DIM), lambda g, i: (g, 0, 0)),
            pl.BlockSpec((None, HEAD_DIM, n_cmp_rows), lambda g, i: (g, 0, 0)),
            pl.BlockSpec((None, L, HEAD_DIM), lambda g, i: (N_KV_HEADS + g, 0, 0)),
            pl.BlockSpec((None, HEAD_DIM, L), lambda g, i: (vw_head0 + g, 0, 0)),
            pl.BlockSpec((TAIL, ATT_TQ), lambda g, i: (0, i)),
            pl.BlockSpec((WINDOW + ATT_TQ, HEAD_DIM), lambda g, i: (0, 0)),
        ],
        out_specs=[
            pl.BlockSpec((ATT_TQ, KV_DIM), lambda g, i: (i, g)),
            pl.BlockSpec((None, None, n_blk, ATT_TQ), lambda g, i: (g, i, 0, 0)),
            pl.BlockSpec((None, None, 8, n_blk), lambda g, i: (g, i, 0, 0)),
        ],
        out_shape=[
            jax.ShapeDtypeStruct((L, D_ATTN), F32),
            jax.ShapeDtypeStruct((N_KV_HEADS, nqb, n_blk, ATT_TQ), F32),
            jax.ShapeDtypeStruct((N_KV_HEADS, nqb, 8, n_blk), F32),
        ],
        scratch_shapes=[
            pltpu.VMEM((n_cmp_rows, Q_PER_KV * ATT_TQ), F32),
            pltpu.VMEM((ATT_TQ // 128, n_cmp_rows + 8, 128), F32),
        ],
        compiler_params=_params(("arbitrary", "arbitrary"), 48),
        name="cmp_win_attention",
    )(slopes, t_out, kc, vct, k_nat, t_out, tail_t,
      jnp.asarray(_position_columns(WINDOW + ATT_TQ, False), BF16))


SEL_TK = 512


MASK_BIG = 2.0 ** 100


def _sel_kernel(tiles_ref, counts_ref, slopes_ref, qt_ref, ks_ref, vst_ref, selt_ref, tailt_ref, ocw_ref, posk_ref,
                o_ref, m_ref, l_ref, acc_ref, *, n_tiles_max):
    g = pl.program_id(0)
    qb = pl.program_id(1)
    T = ATT_TQ
    R = Q_PER_KV
    TK = SEL_TK
    bpt = TK // SEL_BLOCK
    t0 = qb * T
    qt4 = jnp.concatenate([qt_ref[r] for r in range(R)], axis=1)
    sl = [slopes_ref[g * R + r] * LOG2E for r in range(R)]
    lane_head = lax.broadcasted_iota(jnp.int32, (1, R * T), 1) // T
    sl_row = jnp.zeros((1, R * T), F32)
    for r in range(R):
        sl_row = jnp.where(lane_head == r, sl[r], sl_row)
    t_row4 = t0 + lax.broadcasted_iota(jnp.int32, (1, R * T), 1) % T
    zeros_tail = jnp.zeros((HEAD_DIM - 16, R * T), BF16)

    m_ref[...] = jnp.full(m_ref.shape, M_INIT, F32)
    l_ref[...] = jnp.zeros(l_ref.shape, F32)
    acc_ref[...] = jnp.zeros(acc_ref.shape, F32)
    step = g * pl.num_programs(1) + qb

    def visit(n, causal):
        kt = tiles_ref[step * n_tiles_max + n]
        k0 = pl.multiple_of(kt * TK, TK)
        sel8 = selt_ref[pl.ds(pl.multiple_of(kt * bpt, bpt), bpt), :]
        mask8 = (sel8 - 1.0) * MASK_BIG
        q_aug = jnp.concatenate([qt4, _slope_rows(sl, T, jnp.concatenate([mask8] * R, axis=1)), zeros_tail], axis=0)
        s = _dot(jnp.concatenate([ks_ref[pl.ds(k0, TK), :], posk_ref[...]], axis=1), q_aug)
        t_rel = t_row4 - k0
        if causal:
            s = jnp.where(lax.broadcasted_iota(jnp.int32, (TK, R * T), 0) <= t_rel, s, NEG_INF)
        col = sl_row * t_rel.astype(F32)
        m_old = m_ref[...]
        m_new = jnp.maximum(m_old, jnp.max(s, axis=0, keepdims=True) - col)
        alpha = jnp.exp2(m_old - m_new)
        p = jnp.exp2(s - (m_new + col))
        l_ref[...] = alpha * l_ref[...] + jnp.sum(p, axis=0, keepdims=True)
        acc_ref[...] = alpha * acc_ref[...] + _dot(vst_ref[:, pl.ds(k0, TK)], p.astype(BF16))
        m_ref[...] = m_new

    count = counts_ref[step]

    def past_tile(n, carry):
        visit(n, causal=False)
        return carry

    lax.fori_loop(0, count - 1, past_tile, 0)
    visit(count - 1, causal=True)
    l = l_ref[...]
    inv = jnp.where(l > 0.0, 1.0 / l, 0.0)
    for r in range(R):
        cs = slice(r * T, (r + 1) * T)
        o_t = acc_ref[:, cs] * (inv[:, cs] * _gate_row(tailt_ref, 1, g * R + r))
        hs = slice(r * HEAD_DIM, (r + 1) * HEAD_DIM)
        o_ref[:, hs] = ocw_ref[:, hs] + o_t.T


def _active_tiles(blk_count, L):
    G = blk_count.shape[0]
    nqb = L // ATT_TQ
    n_t = L // SEL_TK
    flags = blk_count.reshape(G, nqb, n_t, SEL_TK // SEL_BLOCK).max(axis=-1) > 0.5
    kt = jnp.arange(n_t, dtype=jnp.int32)
    last = (jnp.arange(nqb, dtype=jnp.int32) * ATT_TQ + ATT_TQ - 1) // SEL_TK
    flags = flags & (kt[None, None, :] <= last[None, :, None])
    order = jnp.sort(jnp.where(flags, kt, n_t + kt), axis=-1)
    tiles = jnp.where(order < n_t, order, 0).astype(jnp.int32)
    return tiles.reshape(-1), flags.sum(axis=-1).astype(jnp.int32).reshape(-1)


def _sel_attention(slopes, t_out, k_nat, selt, blk_count, tail_t, ocw):
    L = t_out.shape[2]
    n_blk = L // SEL_BLOCK
    n_t = L // SEL_TK
    cols = Q_PER_KV * ATT_TQ
    vs_head0 = D_ATTN // HEAD_DIM
    tiles, counts = _active_tiles(blk_count, L)
    grid_spec = pltpu.PrefetchScalarGridSpec(
        num_scalar_prefetch=2,
        grid=(N_KV_HEADS, L // ATT_TQ),
        in_specs=[
            pl.BlockSpec(memory_space=pltpu.SMEM),
            pl.BlockSpec((Q_PER_KV, HEAD_DIM, ATT_TQ), lambda g, i, *_: (g, 0, i)),
            pl.BlockSpec((None, L, HEAD_DIM), lambda g, i, *_: (g, 0, 0)),
            pl.BlockSpec((None, HEAD_DIM, L), lambda g, i, *_: (vs_head0 + g, 0, 0)),
            pl.BlockSpec((None, None, n_blk, ATT_TQ), lambda g, i, *_: (g, i, 0, 0)),
            pl.BlockSpec((TAIL, ATT_TQ), lambda g, i, *_: (0, i)),
            pl.BlockSpec((ATT_TQ, KV_DIM), lambda g, i, *_: (i, g)),
            pl.BlockSpec((SEL_TK, HEAD_DIM), lambda g, i, *_: (0, 0)),
        ],
        out_specs=pl.BlockSpec((ATT_TQ, KV_DIM), lambda g, i, *_: (i, g)),
        scratch_shapes=[
            pltpu.VMEM((1, cols), F32),
            pltpu.VMEM((1, cols), F32),
            pltpu.VMEM((HEAD_DIM, cols), F32),
        ],
    )
    return pl.pallas_call(
        functools.partial(_sel_kernel, n_tiles_max=n_t),
        grid_spec=grid_spec,
        out_shape=jax.ShapeDtypeStruct((L, D_ATTN), F32),
        compiler_params=_params(("arbitrary", "arbitrary"), 48),
        name="selected_attention",
    )(tiles, counts, slopes, t_out, k_nat, t_out, selt, tail_t, ocw,
      jnp.asarray(_position_columns(SEL_TK, True), BF16))


def _outproj_kernel(x_ref, oa_ref, ys_ref, nw_ref, wa_ref, ws_ref, o_ref, ya_ref):
    @pl.when(pl.program_id(1) == 0)
    def _():
        ya_ref[...] = _rms(oa_ref[...], nw_ref[...]).astype(BF16)

    o_ref[...] = x_ref[...] + _dot(ya_ref[...], wa_ref[...]) + _dot(ys_ref[...], ws_ref[...])


def _out_projection(x2, o_attn, ys, attn_norm_w, w_out, tm, tn):
    L = x2.shape[0]
    n_j = D_MODEL // tn
    return pl.pallas_call(
        _outproj_kernel,
        grid=(L // tm, n_j),
        in_specs=[
            pl.BlockSpec((tm, tn), lambda i, j: (i, j)),
            pl.BlockSpec((tm, D_ATTN), lambda i, j: (i, 0)),
            pl.BlockSpec((tm, D_SSM), lambda i, j: (i, 0)),
            pl.BlockSpec((1, D_ATTN), lambda i, j: (0, 0)),
            pl.BlockSpec((D_ATTN, tn), lambda i, j: (0, j), pipeline_mode=pl.Buffered(1 if n_j == 1 else 2)),
            pl.BlockSpec((D_SSM, tn), lambda i, j: (1, j), pipeline_mode=pl.Buffered(1 if n_j == 1 else 2)),
        ],
        out_specs=pl.BlockSpec((tm, tn), lambda i, j: (i, j)),
        out_shape=jax.ShapeDtypeStruct((L, D_MODEL), F32),
        scratch_shapes=[pltpu.VMEM((tm, D_ATTN), BF16)],
        compiler_params=_params(("arbitrary", "arbitrary"), 56),
        name="out_projection",
    )(x2, o_attn, ys, attn_norm_w, w_out, w_out)


FFN_HALO = 16


def _ffn_kernel(x_ref, halo_ref, nw_ref, wg_ref, wv_ref, cwg_ref, cwv_ref, cbg_ref, cbv_ref, wd_ref, fw_ref,
                o_ref, hn_ref, ug_ref, uv_ref, acc_ref, *, tm):
    i = pl.program_id(0)
    j = pl.program_id(1)
    H = FFN_HALO

    @pl.when(j == 0)
    def _():
        hn_ref[H:H + tm, :] = _rms(x_ref[...], nw_ref[...]).astype(BF16)
        acc_ref[...] = jnp.zeros_like(acc_ref)

    @pl.when((j == 0) & (i == 0))
    def _():
        hn_ref[0:H, :] = jnp.zeros((H, D_MODEL), BF16)

    @pl.when((j == 0) & (i > 0))
    def _():
        hn_ref[0:H, :] = _rms(halo_ref[...], nw_ref[...]).astype(BF16)

    hn = hn_ref[...]
    ug_ref[...] = _dot(hn, wg_ref[...])
    uv_ref[...] = _dot(hn, wv_ref[...])

    def conv(u_ref, w_ref, b_ref):
        acc = b_ref[...]
        for k in range(FFN_CONV):
            acc = acc + u_ref[pl.ds(H - (FFN_CONV - 1) + k, tm), :] * w_ref[k:k + 1, :]
        return acc

    act = _silu(conv(ug_ref, cwg_ref, cbg_ref)) * conv(uv_ref, cwv_ref, cbv_ref)
    acc_ref[...] += _dot(act.astype(BF16), wd_ref[...])

    @pl.when(j == pl.num_programs(1) - 1)
    def _():
        o_ref[...] = _rms(x_ref[...] + acc_ref[...], fw_ref[...])


def _conv_ffn(x1, norm_w, w_up, conv_w, conv_b, w_down, final_w, tm, tf):
    L = x1.shape[0]
    n_f = D_FF // tf
    hb = tm // FFN_HALO
    return pl.pallas_call(
        functools.partial(_ffn_kernel, tm=tm),
        grid=(L // tm, n_f),
        in_specs=[
            pl.BlockSpec((tm, D_MODEL), lambda i, j: (i, 0)),
            pl.BlockSpec((FFN_HALO, D_MODEL), lambda i, j: (jnp.maximum(i * hb - 1, 0), 0)),
            pl.BlockSpec((1, D_MODEL), lambda i, j: (0, 0)),
            pl.BlockSpec((D_MODEL, tf), lambda i, j: (0, j)),
            pl.BlockSpec((D_MODEL, tf), lambda i, j: (0, n_f + j)),
            pl.BlockSpec((FFN_CONV, tf), lambda i, j: (0, j)),
            pl.BlockSpec((FFN_CONV, tf), lambda i, j: (0, n_f + j)),
            pl.BlockSpec((1, tf), lambda i, j: (0, j)),
            pl.BlockSpec((1, tf), lambda i, j: (0, n_f + j)),
            pl.BlockSpec((tf, D_MODEL), lambda i, j: (j, 0)),
            pl.BlockSpec((1, D_MODEL), lambda i, j: (0, 0)),
        ],
        out_specs=pl.BlockSpec((tm, D_MODEL), lambda i, j: (i, 0)),
        out_shape=jax.ShapeDtypeStruct((L, D_MODEL), F32),
        scratch_shapes=[
            pltpu.VMEM((tm + FFN_HALO, D_MODEL), BF16),
            pltpu.VMEM((tm + FFN_HALO, tf), F32),
            pltpu.VMEM((tm + FFN_HALO, tf), F32),
            pltpu.VMEM((tm, D_MODEL), F32),
        ],
        compiler_params=_params(("arbitrary", "arbitrary"), 56),
        name="conv_ffn",
    )(x1, x1, norm_w, w_up, w_up, conv_w, conv_w, conv_b, conv_b, w_down, final_w)


def _head_expand():
    e = np.zeros((TAIL, D_SSM), np.float32)
    for h in range(SSM_HEADS):
        e[h, h * SSM_HEAD_DIM:(h + 1) * SSM_HEAD_DIM] = 1.0
    return e


def _layer(x2, mix_norm_w, w_in, ssm_conv_w, ssm_conv_b, ssm_dt_bias, ssm_a_log, ssm_d, ssm_norm_w,
           cmp_pos, cmp_w1, cmp_w2_k, cmp_w2_v, attn_norm_w, w_out, ffn_norm_w, w_up, ffn_conv_w, ffn_conv_b,
           w_down, final_norm_w):
    L = x2.shape[0]
    row = lambda v: v.reshape(1, -1).astype(F32)

    bounds = np.cumsum([0, D_SSM, D_SSM + 2 * SSM_GROUPS * SSM_STATE, SSM_HEADS, D_ATTN] + [KV_DIM] * 6
                       + [3 * N_HEADS])
    seg = lambda k: w_in[:, bounds[k]:bounds[k + 1]]
    z_w, xbc_w, dt_w, q_w, kc_w, vc_w, ks_w, vs_w, kw_w, vw_w, gl_w = [seg(k) for k in range(11)]
    w_nat = jnp.concatenate([z_w, xbc_w, kc_w, vc_w, ks_w, kw_w], axis=1).astype(BF16)
    w_tr = jnp.concatenate([q_w, vs_w, vw_w], axis=1).T.astype(BF16)
    w_tail = jnp.concatenate([dt_w, gl_w, jnp.zeros((D_MODEL, TAIL - SSM_HEADS - 3 * N_HEADS), F32)], axis=1)
    w_tail = w_tail.astype(BF16)
    pad_row = lambda v: jnp.concatenate([v.astype(F32), jnp.zeros((TAIL - SSM_HEADS,), F32)]).reshape(1, TAIL)

    p, k_nat, t_out, tail, tail_t = _in_projection(x2, row(mix_norm_w), w_nat, w_tr, w_tail, w_tail.T,
                                                   tm=min(1024, L))
    kc, vct = _compress(p, cmp_pos.astype(F32), cmp_w1.astype(BF16), cmp_w2_k.astype(BF16),
                        cmp_w2_v.T.astype(BF16))

    y_ssm = _ssd(p, tail, tail_t, ssm_conv_w.astype(F32), row(ssm_conv_b), pad_row(ssm_dt_bias),
                 ssm_dt_bias.astype(F32).reshape(SSM_HEADS, 1), pad_row(ssm_a_log),
                 ssm_a_log.astype(F32).reshape(SSM_HEADS, 1),
                 row(jnp.repeat(ssm_d.astype(F32), SSM_HEAD_DIM)), jnp.asarray(_head_expand(), BF16),
                 row(ssm_norm_w))

    n_blk = L // SEL_BLOCK
    slopes = 2.0 ** (-8.0 * jnp.arange(1, N_HEADS + 1, dtype=F32) / N_HEADS)
    ocw, selt, blk_count = _cmp_win_attention(slopes, t_out, kc, vct, k_nat, tail_t, min(N_SEL, n_blk))
    o_attn = _sel_attention(slopes, t_out, k_nat, selt, blk_count[:, :, 0, :], tail_t, ocw)

    x1 = _out_projection(x2, o_attn, y_ssm, row(attn_norm_w), w_out.astype(BF16), tm=512, tn=D_MODEL)
    return _conv_ffn(x1, row(ffn_norm_w), w_up.astype(BF16), ffn_conv_w.astype(F32), row(ffn_conv_b),
                     w_down.astype(BF16), row(final_norm_w), tm=512, tf=512)


def kernel(x, mix_norm_w, w_in, ssm_conv_w, ssm_conv_b, ssm_dt_bias, ssm_a_log, ssm_d, ssm_norm_w, cmp_pos_k,
           cmp_w1_k, cmp_w2_k, cmp_pos_v, cmp_w1_v, cmp_w2_v, attn_norm_w, w_out, ffn_norm_w, w_up, ffn_conv_w,
           ffn_conv_b, w_down, final_norm_w):
    bsz, L, _ = x.shape
    assert bsz == 1 and mix_norm_w.shape[0] == 1, "single sequence, single layer"
    assert L % 1024 == 0 and L >= WINDOW + ATT_TQ
    out = _layer(
        x[0], mix_norm_w[0], w_in[0], ssm_conv_w[0], ssm_conv_b[0], ssm_dt_bias[0], ssm_a_log[0], ssm_d[0],
        ssm_norm_w[0], jnp.stack([cmp_pos_k[0], cmp_pos_v[0]]), jnp.stack([cmp_w1_k[0], cmp_w1_v[0]]),
        cmp_w2_k[0], cmp_w2_v[0], attn_norm_w[0], w_out[0], ffn_norm_w[0], w_up[0], ffn_conv_w[0],
        ffn_conv_b[0], w_down[0], final_norm_w)
    return out[None]
```

```python
import functools

import numpy as np
import jax
import jax.numpy as jnp
from jax import lax
from jax.experimental import pallas as pl
from jax.experimental.pallas import tpu as pltpu

F32 = jnp.float32
BF16 = jnp.bfloat16

D_MODEL = 2048
D_ATTN = 2048
D_SSM = 2048
N_HEADS = 16
HEAD_DIM = 128
N_KV_HEADS = 4
Q_PER_KV = 4
KV_DIM = 512
CMP_LEN = 32
CMP_STRIDE = 16
CMP_HIDDEN = 256
SEL_BLOCK = 64
N_SEL = 16
WINDOW = 512
ATT_TQ = 256
SEL_TQ = 512
SSM_HEAD_DIM = 64
SSM_HEADS = 32
SSM_GROUPS = 4
SSM_STATE = 128
SSM_CONV = 4
SSM_CHUNK = 256
D_FF = 5632
FFN_CONV = 3
NORM_EPS = 1e-6
NEG_INF = -1e30
FORCE_SCORE = 1e4

LOG2E = 1.4426950408889634
M_INIT = -1e20

P_COLS = 6144
K_COLS = 1024
T_ROWS = 3072
TAIL = 128
PROJ_TN = 1024
KC_COL0 = 5120
VC_COL0 = 5632
MIB = 1024 * 1024


def _params(sem, vmem_mib):
    return pltpu.CompilerParams(dimension_semantics=sem, vmem_limit_bytes=vmem_mib * MIB)


def _rms(x, w):
    return x * lax.rsqrt(jnp.mean(x * x, axis=-1, keepdims=True) + NORM_EPS) * w


def _dot(a, b):
    return jnp.dot(a, b, preferred_element_type=F32)


def _dot_nt(a, b):
    return lax.dot_general(a, b, (((1,), (1,)), ((), ())), preferred_element_type=F32)


def _split3(x):
    hi = x.astype(BF16)
    r1 = x - hi.astype(F32)
    mid = r1.astype(BF16)
    lo = (r1 - mid.astype(F32)).astype(BF16)
    return hi, mid, lo


def _dot3(x, m):
    hi, mid, lo = _split3(x)
    return _dot(hi, m) + _dot(mid, m) + _dot(lo, m)


def _dot3_left(m, x):
    hi, mid, lo = _split3(x)
    return _dot(m, hi) + _dot(m, mid) + _dot(m, lo)


def _silu(x):
    return x * (1.0 / (1.0 + jnp.exp(-x)))


def _sigmoid(x):
    return 1.0 / (1.0 + jnp.exp(-x))


def _inproj_kernel(x_ref, nw_ref, w_ref, wtr_ref, wt_ref, wtt_ref, p_ref, k_ref, t_ref, tail_ref, tailt_ref,
                   xn_ref, *, n_p, n_q):
    j = pl.program_id(1)
    hp = PROJ_TN // HEAD_DIM

    @pl.when(j == 0)
    def _():
        xn = _rms(x_ref[...], nw_ref[...]).astype(BF16)
        xn_ref[...] = xn
        tail_ref[...] = _dot(xn, wt_ref[...])
        tailt_ref[...] = _dot_nt(wtt_ref[...], xn)

    @pl.when(j <= n_p)
    def _():
        r = _dot(xn_ref[...], w_ref[...])

        @pl.when(j < n_p)
        def _():
            p_ref[...] = r

        @pl.when(j == n_p)
        def _():
            for c in range(hp):
                k_ref[c] = r[:, c * HEAD_DIM:(c + 1) * HEAD_DIM].astype(BF16)

    @pl.when(j > n_p)
    def _():
        r = _dot_nt(wtr_ref[...], xn_ref[...])
        r = r * jnp.where(j - n_p - 1 < n_q, HEAD_DIM ** -0.5 * LOG2E, 1.0)
        for c in range(hp):
            t_ref[c] = r[c * HEAD_DIM:(c + 1) * HEAD_DIM, :].astype(BF16)


def _in_projection(x2, norm_w, w_nat, w_tr, w_tail, w_tail_t, tm):
    L = x2.shape[0]
    n_p = P_COLS // PROJ_TN
    n_t = T_ROWS // PROJ_TN
    hp = PROJ_TN // HEAD_DIM
    return pl.pallas_call(
        functools.partial(_inproj_kernel, n_p=n_p, n_q=D_ATTN // PROJ_TN),
        grid=(L // tm, n_p + 1 + n_t),
        in_specs=[
            pl.BlockSpec((tm, D_MODEL), lambda i, j: (i, 0), pipeline_mode=pl.Buffered(1)),
            pl.BlockSpec((1, D_MODEL), lambda i, j: (0, 0)),
            pl.BlockSpec((D_MODEL, PROJ_TN), lambda i, j: (0, jnp.minimum(j, n_p))),
            pl.BlockSpec((PROJ_TN, D_MODEL), lambda i, j: (jnp.clip(j - n_p - 1, 0, n_t - 1), 0)),
            pl.BlockSpec((D_MODEL, TAIL), lambda i, j: (0, 0)),
            pl.BlockSpec((TAIL, D_MODEL), lambda i, j: (0, 0)),
        ],
        out_specs=[
            pl.BlockSpec((tm, PROJ_TN), lambda i, j: (i, jnp.minimum(j, n_p - 1))),
            pl.BlockSpec((hp, tm, HEAD_DIM), lambda i, j: (0, i, 0)),
            pl.BlockSpec((hp, HEAD_DIM, tm), lambda i, j: (jnp.clip(j - n_p - 1, 0, n_t - 1), 0, i)),
            pl.BlockSpec((tm, TAIL), lambda i, j: (i, 0)),
            pl.BlockSpec((TAIL, tm), lambda i, j: (0, i)),
        ],
        out_shape=[
            jax.ShapeDtypeStruct((L, P_COLS), F32),
            jax.ShapeDtypeStruct((K_COLS // HEAD_DIM, L, HEAD_DIM), BF16),
            jax.ShapeDtypeStruct((T_ROWS // HEAD_DIM, HEAD_DIM, L), BF16),
            jax.ShapeDtypeStruct((L, TAIL), F32),
            jax.ShapeDtypeStruct((TAIL, L), F32),
        ],
        scratch_shapes=[pltpu.VMEM((tm, D_MODEL), BF16)],
        compiler_params=_params(("arbitrary", "arbitrary"), 56),
        name="in_projection",
    )(x2, norm_w, w_nat, w_tr, w_tail, w_tail_t)


def _compress_hidden(x_ref, pos_ref, w1_ref, n_rows):
    half = CMP_LEN // 2
    acc0 = jnp.zeros((n_rows, CMP_HIDDEN), F32)
    acc1 = jnp.zeros((n_rows, CMP_HIDDEN), F32)
    for i in range(half):
        xi = x_ref[pl.ds(i, n_rows, stride=CMP_STRIDE), :]
        a0 = (xi + pos_ref[i:i + 1, :]).astype(BF16)
        a1 = (xi + pos_ref[half + i:half + i + 1, :]).astype(BF16)
        acc0 = acc0 + _dot(a0, w1_ref[i * HEAD_DIM:(i + 1) * HEAD_DIM, :])
        acc1 = acc1 + _dot(a1, w1_ref[(half + i) * HEAD_DIM:(half + i + 1) * HEAD_DIM, :])
    hid = acc0 + pltpu.roll(acc1, n_rows - 1, 0)
    return jax.nn.gelu(hid).astype(BF16)


def _compress_kernel(xk_ref, xv_ref, pos_ref, w1_ref, w2k_ref, w2vt_ref, kc_ref, vct_ref, *, n_rows):
    hk = _compress_hidden(xk_ref, pos_ref.at[0], w1_ref.at[0], n_rows)
    kc_ref[...] = _dot(hk, w2k_ref[...]).astype(BF16)
    hv = _compress_hidden(xv_ref, pos_ref.at[1], w1_ref.at[1], n_rows)
    vct_ref[...] = _dot_nt(w2vt_ref[...], hv).astype(BF16)


def _compress(p, pos, w1, w2k, w2vt):
    L = p.shape[0]
    n_rows = L // CMP_STRIDE
    full = lambda a: pl.BlockSpec(a.shape, lambda g: (0,) * a.ndim)
    return pl.pallas_call(
        functools.partial(_compress_kernel, n_rows=n_rows),
        grid=(N_KV_HEADS,),
        in_specs=[
            pl.BlockSpec((L, HEAD_DIM), lambda g: (0, KC_COL0 // HEAD_DIM + g)),
            pl.BlockSpec((L, HEAD_DIM), lambda g: (0, VC_COL0 // HEAD_DIM + g)),
            full(pos), full(w1), full(w2k), full(w2vt),
        ],
        out_specs=[
            pl.BlockSpec((None, n_rows, HEAD_DIM), lambda g: (g, 0, 0)),
            pl.BlockSpec((None, HEAD_DIM, n_rows), lambda g: (g, 0, 0)),
        ],
        out_shape=[
            jax.ShapeDtypeStruct((N_KV_HEADS, n_rows, HEAD_DIM), BF16),
            jax.ShapeDtypeStruct((N_KV_HEADS, HEAD_DIM, n_rows), BF16),
        ],
        compiler_params=_params(("arbitrary",), 56),
        name="kv_compress",
    )(p, p, pos, w1, w2k, w2vt)


def _ssd_kernel(xs_ref, bc_ref, xs_halo_ref, bc_halo_ref, z_ref, tail_ref, tailt_ref, cw_ref, cb_ref,
                dtb_row_ref, dtb_col_ref, alog_row_ref, alog_col_ref, dskip_ref, expand_ref, nw_ref,
                o_ref, state_ref, cx_ref, cbc_ref):
    c = pl.program_id(0)
    Q = SSM_CHUNK
    GW = D_SSM // SSM_GROUPS
    halo = 8

    @pl.when(c == 0)
    def _():
        state_ref[...] = jnp.zeros_like(state_ref)
        cx_ref[0:halo, :] = jnp.zeros((halo, D_SSM), F32)
        cbc_ref[0:halo, :] = jnp.zeros((halo, 2 * SSM_GROUPS * SSM_STATE), F32)

    @pl.when(c > 0)
    def _():
        cx_ref[0:halo, :] = xs_halo_ref[...]
        cbc_ref[0:halo, :] = bc_halo_ref[...]

    cx_ref[halo:halo + Q, :] = xs_ref[...]
    cbc_ref[halo:halo + Q, :] = bc_ref[...]

    def conv_silu(ref, w, b):
        acc = b
        for k in range(SSM_CONV):
            acc = acc + ref[pl.ds(halo - (SSM_CONV - 1) + k, Q), :] * w[k:k + 1, :]
        return _silu(acc)

    cw = cw_ref[...]
    cb = cb_ref[...]
    xs = conv_silu(cx_ref, cw[:, :D_SSM], cb[:, :D_SSM])
    bcm = conv_silu(cbc_ref, cw[:, D_SSM:], cb[:, D_SSM:])

    def softplus(v):
        return jnp.maximum(v, 0.0) + jnp.log1p(jnp.exp(-jnp.abs(v)))

    dt = softplus(tail_ref[...] + dtb_row_ref[...])
    a = dt * (-jnp.exp(alog_row_ref[...]))
    rows = lax.broadcasted_iota(jnp.int32, (Q, Q), 0)
    cols = lax.broadcasted_iota(jnp.int32, (Q, Q), 1)
    causal = cols <= rows
    tri = jnp.where(causal, 1.0, 0.0).astype(BF16)
    a_cum = _dot3_left(tri, a)
    dt_t = softplus(tailt_ref[0:SSM_HEADS, :] + dtb_col_ref[...])
    a_t = dt_t * (-jnp.exp(alog_col_ref[...]))
    tri_t = jnp.where(rows <= cols, 1.0, 0.0).astype(BF16)
    a_cum_t = _dot3(a_t, tri_t)

    expand = expand_ref[...]
    dt_e = _dot3(dt, expand)
    ac_e = _dot3(a_cum, expand)
    xdt = xs * dt_e
    decay_to = jnp.exp(ac_e)
    a_last = ac_e[Q - 1:Q, :]
    xdd = xdt * jnp.exp(a_last - ac_e)
    chunk_decay = jnp.exp(a_last)

    lane = lax.broadcasted_iota(jnp.int32, (Q, 2 * SSM_HEAD_DIM), 1)
    first_head = lane < SSM_HEAD_DIM
    y_groups = []
    for g in range(SSM_GROUPS):
        bg = bcm[:, g * SSM_STATE:(g + 1) * SSM_STATE]
        cg = bcm[:, (SSM_GROUPS + g) * SSM_STATE:(SSM_GROUPS + g + 1) * SSM_STATE].astype(BF16)
        gmat = _dot_nt(cg, bg.astype(BF16))
        st = state_ref[g]
        y_off = _dot(cg, st.astype(BF16)) * decay_to[:, g * GW:(g + 1) * GW]
        new = _dot(bg.T.astype(BF16), xdd[:, g * GW:(g + 1) * GW].astype(BF16))
        state_ref[g] = st * chunk_decay[:, g * GW:(g + 1) * GW] + new
        pairs = []
        for pr in range(GW // (2 * SSM_HEAD_DIM)):
            h0 = g * (GW // SSM_HEAD_DIM) + 2 * pr
            xp = xdt[:, h0 * SSM_HEAD_DIM:(h0 + 2) * SSM_HEAD_DIM].astype(BF16)
            ys = []
            for h in (h0, h0 + 1):
                diff = a_cum[:, h:h + 1] - a_cum_t[h:h + 1, :]
                m = (gmat * jnp.exp(jnp.where(causal, diff, NEG_INF))).astype(BF16)
                ys.append(_dot(m, xp))
            pairs.append(jnp.where(first_head, ys[0], ys[1]))
        y_groups.append(jnp.concatenate(pairs, axis=1) + y_off)
    y = jnp.concatenate(y_groups, axis=1) + xs * dskip_ref[...]
    y = y * _silu(z_ref[...])
    o_ref[...] = _rms(y, nw_ref[...]).astype(BF16)


def _ssd(p, tail, tail_t, conv_w, conv_b, dtb_row, dtb_col, alog_row, alog_col, dskip_e, expand, norm_w):
    L = p.shape[0]
    Q = SSM_CHUNK
    bc_w = 2 * SSM_GROUPS * SSM_STATE
    xs_blk = D_SSM // D_SSM
    bc_blk = (2 * D_SSM) // bc_w
    hb = Q // 8
    full = lambda shape: pl.BlockSpec(shape, lambda c: (0,) * len(shape))
    return pl.pallas_call(
        _ssd_kernel,
        grid=(L // Q,),
        in_specs=[
            pl.BlockSpec((Q, D_SSM), lambda c: (c, xs_blk)),
            pl.BlockSpec((Q, bc_w), lambda c: (c, bc_blk)),
            pl.BlockSpec((8, D_SSM), lambda c: (jnp.maximum(c * hb - 1, 0), xs_blk)),
            pl.BlockSpec((8, bc_w), lambda c: (jnp.maximum(c * hb - 1, 0), bc_blk)),
            pl.BlockSpec((Q, D_SSM), lambda c: (c, 0)),
            pl.BlockSpec((Q, TAIL), lambda c: (c, 0)),
            pl.BlockSpec((TAIL, Q), lambda c: (0, c)),
            full(conv_w.shape), full(conv_b.shape), full(dtb_row.shape), full(dtb_col.shape),
            full(alog_row.shape), full(alog_col.shape), full(dskip_e.shape), full(expand.shape),
            full(norm_w.shape),
        ],
        out_specs=pl.BlockSpec((Q, D_SSM), lambda c: (c, 0)),
        out_shape=jax.ShapeDtypeStruct((L, D_SSM), BF16),
        scratch_shapes=[
            pltpu.VMEM((SSM_GROUPS, SSM_STATE, D_SSM // SSM_GROUPS), F32),
            pltpu.VMEM((Q + 8, D_SSM), F32),
            pltpu.VMEM((Q + 8, bc_w), F32),
        ],
        compiler_params=_params(("arbitrary",), 48),
        name="ssd_scan",
    )(p, p, p, p, p, tail, tail_t, conv_w, conv_b, dtb_row, dtb_col, alog_row, alog_col, dskip_e, expand, norm_w)


GATE_ROW0 = SSM_HEADS


def _gate_row(tailt_ref, branch, head):
    return _sigmoid(tailt_ref[pl.ds(GATE_ROW0 + branch * N_HEADS + head, 1), :])


POS_COARSE = 64


def _position_columns(positions, with_block_onehot):
    positions = np.asarray(positions)
    e = np.zeros((positions.shape[0], HEAD_DIM), np.float32)
    e[:, 0:3] = ((positions // POS_COARSE) * POS_COARSE)[:, None]
    e[:, 3:6] = (positions % POS_COARSE)[:, None]
    if with_block_onehot:
        k = np.arange(positions.shape[0])
        e[k, 8 + k // SEL_BLOCK] = 1.0
    return e


def _slope_lane_row(sl, T):
    n_heads = len(sl)
    lane_head = lax.broadcasted_iota(jnp.int32, (1, n_heads * T), 1) // T
    row = jnp.zeros((1, n_heads * T), F32)
    for r in range(n_heads):
        row = jnp.where(lane_head == r, sl[r], row)
    return row


def _slope_rows(sl, T, extra8):
    pieces = [x.astype(F32) for x in _split3(_slope_lane_row(sl, T))]
    top = jnp.concatenate(pieces + pieces + [jnp.zeros((2, len(sl) * T), F32)], axis=0)
    return jnp.concatenate([top, extra8], axis=0).astype(BF16)


def _cmp_win_kernel(slopes_ref, qt_ref, kc_ref, vct_ref, kw_ref, vwt_ref, tailt_ref, posw_ref,
                    ocw_ref, selt_ref, any_ref, s_ref, psum_ref, *, n_cmp_rows, n_blk, n_sel, chunk):
    g = pl.program_id(0)
    qb = pl.program_id(1)
    T = ATT_TQ
    R = Q_PER_KV
    CH = chunk
    t0 = qb * T
    qt4 = jnp.concatenate([qt_ref[r] for r in range(R)], axis=1)
    t_row = t0 + lax.broadcasted_iota(jnp.int32, (1, T), 1)
    sl = [slopes_ref[g * R + r] * LOG2E for r in range(R)]

    n_chunks = (t0 + T - CMP_LEN) // CMP_STRIDE // CH + 1
    end_rel = lax.broadcasted_iota(jnp.int32, (CH, T), 0) * CMP_STRIDE + (CMP_LEN - 1)
    end_rel_f = end_rel.astype(F32)

    def scores(c, m):
        r0 = pl.multiple_of(c * CH, CH)
        s = _dot(kc_ref[pl.ds(r0, CH), :], qt4)
        valid = (t_row - r0 * CMP_STRIDE) >= end_rel
        key_rel = end_rel_f + (r0 * CMP_STRIDE - t0).astype(F32)
        ms = []
        for r in range(R):
            v = jnp.where(valid, s[:, r * T:(r + 1) * T] + sl[r] * key_rel, NEG_INF)
            s_ref[pl.ds(r0, CH), r * T:(r + 1) * T] = v
            ms.append(jnp.max(v, axis=0, keepdims=True))
        return jnp.maximum(m, jnp.concatenate(ms, axis=1))

    m = lax.fori_loop(0, n_chunks, scores, jnp.full((1, R * T), M_INIT, F32))

    def probs(c, carry):
        l, acc = carry
        r0 = pl.multiple_of(c * CH, CH)
        p = jnp.exp2(s_ref[pl.ds(r0, CH), :] - m)
        s_ref[pl.ds(r0, CH), :] = p
        l = l + jnp.sum(p, axis=0, keepdims=True)
        acc = acc + _dot(vct_ref[:, pl.ds(r0, CH)], p.astype(BF16))
        return l, acc

    l, acc = lax.fori_loop(0, n_chunks, probs, (jnp.zeros((1, R * T), F32), jnp.zeros((HEAD_DIM, R * T), F32)))
    inv = jnp.where(l > 0.0, 1.0 / l, 0.0)

    PAD = 8
    LW = 128
    for u in range(T // LW):
        psum_ref[u, 0:PAD, :] = jnp.zeros((PAD, LW), F32)
    for c in range(n_cmp_rows // CH):
        @pl.when(c < n_chunks)
        def _():
            pn = s_ref[c * CH:(c + 1) * CH, :] * inv
            tot = pn[:, 0:T]
            for r in range(1, R):
                tot = tot + pn[:, r * T:(r + 1) * T]
            for u in range(T // LW):
                psum_ref[u, PAD + c * CH:PAD + (c + 1) * CH, :] = tot[:, u * LW:(u + 1) * LW]

        @pl.when(c >= n_chunks)
        def _():
            for u in range(T // LW):
                psum_ref[u, PAD + c * CH:PAD + (c + 1) * CH, :] = jnp.zeros((CH, LW), F32)

    ratio, b_r = SEL_BLOCK // CMP_STRIDE, CMP_LEN // CMP_STRIDE
    imp = jnp.zeros((n_blk, T), F32)
    for shift in range(ratio + b_r - 1):
        mult = sum(1 for mm in range(ratio) for nn in range(b_r) if mm + nn == shift)
        rows = pl.ds(PAD + ratio - 1 - shift, n_blk, stride=ratio)
        imp = imp + float(mult) * jnp.concatenate([psum_ref[u, rows, :] for u in range(T // LW)], axis=1)

    jb = lax.broadcasted_iota(jnp.int32, (n_blk, T), 0)
    jt = t_row // SEL_BLOCK
    forced = (jb == 0) | (jb == jt) | (jb == jt - 1)
    quota = (n_sel - 1 - jnp.minimum(jt, 2)).astype(F32)
    work0 = jnp.where(forced, NEG_INF, jnp.where(jb <= jt, imp, -1.0))

    n_rounds = n_sel - jnp.where(t0 >= 2 * SEL_BLOCK, 3, 1)

    def select_among(rows):
        jbf_v = lax.broadcasted_iota(jnp.int32, (rows, T), 0).astype(F32)

        def pick(i, work):
            top = jnp.max(work, axis=0, keepdims=True)
            first = jnp.min(jnp.where(work == top, jbf_v, float(n_blk)), axis=0, keepdims=True)
            first = jnp.where(i.astype(F32) < quota, first, -1.0)
            return jnp.where(jbf_v == first, NEG_INF, work)

        work = lax.fori_loop(0, n_rounds, pick, work0[:rows])
        selt_ref[0:rows, :] = jnp.where(work == NEG_INF, 1.0, 0.0)
        if rows < n_blk:
            selt_ref[rows:n_blk, :] = jnp.zeros((n_blk - rows, T), F32)

    n_var = min(8, n_blk // 8)
    sec_rows = n_blk // n_var
    section = jnp.minimum(((t0 + T) // SEL_BLOCK - 1) // sec_rows, n_var - 1)
    for var in range(n_var):
        pl.when(section == var)(functools.partial(select_among, (var + 1) * sec_rows))
    any_ref[...] = _dot_nt(jnp.ones((8, T), BF16), selt_ref[...].astype(BF16))

    wlen = WINDOW + T
    start = pl.multiple_of(jnp.maximum(t0 - WINDOW, 0), T)
    q_aug = jnp.concatenate([qt4, _slope_rows(sl, T, jnp.zeros((8, R * T), F32)),
                             jnp.zeros((HEAD_DIM - 16, R * T), BF16)], axis=0)
    s = _dot(jnp.concatenate([kw_ref[pl.ds(start, wlen), :], posw_ref[...]], axis=1), q_aug)
    krow = lax.broadcasted_iota(jnp.int32, (wlen, T), 0)
    dist = (t_row - start) - krow
    valid = (dist >= 0) & (dist < WINDOW)
    vwt = vwt_ref[:, pl.ds(start, wlen)]
    for r in range(R):
        cs = slice(r * T, (r + 1) * T)
        v = jnp.where(valid, s[:, cs], NEG_INF)
        p = jnp.exp2(v - jnp.max(v, axis=0, keepdims=True))
        lw = jnp.sum(p, axis=0, keepdims=True)
        ow = _dot(vwt, p.astype(BF16))
        head = g * R + r
        o_t = acc[:, cs] * (inv[:, cs] * _gate_row(tailt_ref, 0, head)) + ow * (_gate_row(tailt_ref, 2, head) / lw)
        ocw_ref[:, r * HEAD_DIM:(r + 1) * HEAD_DIM] = o_t.T


def _cmp_win_attention(slopes, t_out, kc, vct, k_nat, tail_t, n_sel):
    L = t_out.shape[2]
    n_cmp_rows = L // CMP_STRIDE
    n_blk = L // SEL_BLOCK
    nqb = L // ATT_TQ
    chunk = min(256, n_cmp_rows)
    kern = functools.partial(_cmp_win_kernel, n_cmp_rows=n_cmp_rows, n_blk=n_blk, n_sel=n_sel, chunk=chunk)
    vw_head0 = (D_ATTN + KV_DIM) // HEAD_DIM
    return pl.pallas_call(
        kern,
        grid=(N_KV_HEADS, nqb),
        in_specs=[
            pl.BlockSpec(memory_space=pltpu.SMEM),
            pl.BlockSpec((Q_PER_KV, HEAD_DIM, ATT_TQ), lambda g, i: (g, 0, i)),
            pl.BlockSpec((None, n_cmp_rows, HEAD_DIM), lambda g, i: (g, 0, 0)),
            pl.BlockSpec((None, HEAD_DIM, n_cmp_rows), lambda g, i: (g, 0, 0)),
            pl.BlockSpec((None, L, HEAD_DIM), lambda g, i: (N_KV_HEADS + g, 0, 0)),
            pl.BlockSpec((None, HEAD_DIM, L), lambda g, i: (vw_head0 + g, 0, 0)),
            pl.BlockSpec((TAIL, ATT_TQ), lambda g, i: (0, i)),
            pl.BlockSpec((WINDOW + ATT_TQ, HEAD_DIM), lambda g, i: (0, 0)),
        ],
        out_specs=[
            pl.BlockSpec((ATT_TQ, KV_DIM), lambda g, i: (i, g)),
            pl.BlockSpec((None, None, n_blk, ATT_TQ), lambda g, i: (g, i, 0, 0)),
            pl.BlockSpec((None, None, 8, n_blk), lambda g, i: (g, i, 0, 0)),
        ],
        out_shape=[
            jax.ShapeDtypeStruct((L, D_ATTN), F32),
            jax.ShapeDtypeStruct((N_KV_HEADS, nqb, n_blk, ATT_TQ), F32),
            jax.ShapeDtypeStruct((N_KV_HEADS, nqb, 8, n_blk), F32),
        ],
        scratch_shapes=[
            pltpu.VMEM((n_cmp_rows, Q_PER_KV * ATT_TQ), F32),
            pltpu.VMEM((ATT_TQ // 128, n_cmp_rows + 8, 128), F32),
        ],
        compiler_params=_params(("arbitrary", "arbitrary"), 48),
        name="cmp_win_attention",
    )(slopes, t_out, kc, vct, k_nat, t_out, tail_t,
      jnp.asarray(_position_columns(np.arange(WINDOW + ATT_TQ), False), BF16))


SEL_TK = 512


MASK_BIG = 2.0 ** 100


def _sel_kernel(tiles_ref, counts_ref, slopes_ref, qt_ref, ks_ref, vst_ref, selt_ref, tailt_ref, ocw_ref, posk_ref,
                o_ref, m_ref, l_ref, acc_ref, *, n_tiles_max):
    g = pl.program_id(0)
    qb = pl.program_id(1)
    T = SEL_TQ
    R = Q_PER_KV
    TK = SEL_TK
    bpt = TK // SEL_BLOCK
    t0 = qb * T
    qt4 = jnp.concatenate([qt_ref[r] for r in range(R)], axis=1)
    sl = [slopes_ref[g * R + r] * LOG2E for r in range(R)]
    sl_row = _slope_lane_row(sl, T)
    t_row4 = t0 + lax.broadcasted_iota(jnp.int32, (1, R * T), 1) % T
    zeros_tail = jnp.zeros((HEAD_DIM - 16, R * T), BF16)

    m_ref[...] = jnp.full(m_ref.shape, M_INIT, F32)
    l_ref[...] = jnp.zeros(l_ref.shape, F32)
    acc_ref[...] = jnp.zeros(acc_ref.shape, F32)
    step = g * pl.num_programs(1) + qb

    def visit(n, causal):
        kt = tiles_ref[step * n_tiles_max + n]
        k0 = pl.multiple_of(kt * TK, TK)
        b0 = pl.multiple_of(kt * bpt, bpt)
        sel8 = jnp.concatenate([selt_ref[u, pl.ds(b0, bpt), :] for u in range(T // ATT_TQ)], axis=1)
        mask8 = (sel8 - 1.0) * MASK_BIG
        q_aug = jnp.concatenate([qt4, _slope_rows(sl, T, jnp.concatenate([mask8] * R, axis=1)), zeros_tail], axis=0)
        s = _dot(jnp.concatenate([ks_ref[pl.ds(k0, TK), :], posk_ref[...]], axis=1), q_aug)
        t_rel = t_row4 - k0
        if causal:
            s = jnp.where(lax.broadcasted_iota(jnp.int32, (TK, R * T), 0) <= t_rel, s, NEG_INF)
        col = sl_row * t_rel.astype(F32)
        m_old = m_ref[...]
        m_new = jnp.maximum(m_old, jnp.max(s, axis=0, keepdims=True) - col)
        alpha = jnp.exp2(m_old - m_new)
        p = jnp.exp2(s - (m_new + col))
        l_ref[...] = alpha * l_ref[...] + jnp.sum(p, axis=0, keepdims=True)
        acc_ref[...] = alpha * acc_ref[...] + _dot(vst_ref[:, pl.ds(k0, TK)], p.astype(BF16))
        m_ref[...] = m_new

    count = counts_ref[step]

    def past_tile(n, carry):
        visit(n, causal=False)
        return carry

    lax.fori_loop(0, count - 1, past_tile, 0)
    visit(count - 1, causal=True)
    l = l_ref[...]
    inv = jnp.where(l > 0.0, 1.0 / l, 0.0)
    for r in range(R):
        cs = slice(r * T, (r + 1) * T)
        o_t = acc_ref[:, cs] * (inv[:, cs] * _gate_row(tailt_ref, 1, g * R + r))
        hs = slice(r * HEAD_DIM, (r + 1) * HEAD_DIM)
        o_ref[:, hs] = ocw_ref[:, hs] + o_t.T


def _active_tiles(blk_count, L):
    G = blk_count.shape[0]
    nqb = L // SEL_TQ
    n_t = L // SEL_TK
    flags = blk_count.reshape(G, nqb, SEL_TQ // ATT_TQ, n_t, SEL_TK // SEL_BLOCK).max(axis=(2, 4)) > 0.5
    kt = jnp.arange(n_t, dtype=jnp.int32)
    last = (jnp.arange(nqb, dtype=jnp.int32) * SEL_TQ + SEL_TQ - 1) // SEL_TK
    flags = flags & (kt[None, None, :] <= last[None, :, None])
    order = jnp.sort(jnp.where(flags, kt, n_t + kt), axis=-1)
    tiles = jnp.where(order < n_t, order, 0).astype(jnp.int32)
    return tiles.reshape(-1), flags.sum(axis=-1).astype(jnp.int32).reshape(-1)


def _sel_attention(slopes, t_out, k_nat, selt, blk_count, tail_t, ocw):
    L = t_out.shape[2]
    n_blk = L // SEL_BLOCK
    n_t = L // SEL_TK
    cols = Q_PER_KV * SEL_TQ
    vs_head0 = D_ATTN // HEAD_DIM
    tiles, counts = _active_tiles(blk_count, L)
    grid_spec = pltpu.PrefetchScalarGridSpec(
        num_scalar_prefetch=2,
        grid=(N_KV_HEADS, L // SEL_TQ),
        in_specs=[
            pl.BlockSpec(memory_space=pltpu.SMEM),
            pl.BlockSpec((Q_PER_KV, HEAD_DIM, SEL_TQ), lambda g, i, *_: (g, 0, i)),
            pl.BlockSpec((None, L, HEAD_DIM), lambda g, i, *_: (g, 0, 0)),
            pl.BlockSpec((None, HEAD_DIM, L), lambda g, i, *_: (vs_head0 + g, 0, 0)),
            pl.BlockSpec((None, SEL_TQ // ATT_TQ, n_blk, ATT_TQ), lambda g, i, *_: (g, i, 0, 0)),
            pl.BlockSpec((TAIL, SEL_TQ), lambda g, i, *_: (0, i)),
            pl.BlockSpec((SEL_TQ, KV_DIM), lambda g, i, *_: (i, g)),
            pl.BlockSpec((SEL_TK, HEAD_DIM), lambda g, i, *_: (0, 0)),
        ],
        out_specs=pl.BlockSpec((SEL_TQ, KV_DIM), lambda g, i, *_: (i, g)),
        scratch_shapes=[
            pltpu.VMEM((1, cols), F32),
            pltpu.VMEM((1, cols), F32),
            pltpu.VMEM((HEAD_DIM, cols), F32),
        ],
    )
    return pl.pallas_call(
        functools.partial(_sel_kernel, n_tiles_max=n_t),
        grid_spec=grid_spec,
        out_shape=jax.ShapeDtypeStruct((L, D_ATTN), F32),
        compiler_params=_params(("arbitrary", "arbitrary"), 48),
        name="selected_attention",
    )(tiles, counts, slopes, t_out, k_nat, t_out, selt, tail_t, ocw,
      jnp.asarray(_position_columns(np.arange(SEL_TK), True), BF16))


def _outproj_kernel(x_ref, oa_ref, ys_ref, nw_ref, wa_ref, ws_ref, o_ref, ya_ref):
    @pl.when(pl.program_id(1) == 0)
    def _():
        ya_ref[...] = _rms(oa_ref[...], nw_ref[...]).astype(BF16)

    o_ref[...] = x_ref[...] + _dot(ya_ref[...], wa_ref[...]) + _dot(ys_ref[...], ws_ref[...])


def _out_projection(x2, o_attn, ys, attn_norm_w, w_out, tm, tn):
    L = x2.shape[0]
    n_j = D_MODEL // tn
    return pl.pallas_call(
        _outproj_kernel,
        grid=(L // tm, n_j),
        in_specs=[
            pl.BlockSpec((tm, tn), lambda i, j: (i, j)),
            pl.BlockSpec((tm, D_ATTN), lambda i, j: (i, 0)),
            pl.BlockSpec((tm, D_SSM), lambda i, j: (i, 0)),
            pl.BlockSpec((1, D_ATTN), lambda i, j: (0, 0)),
            pl.BlockSpec((D_ATTN, tn), lambda i, j: (0, j), pipeline_mode=pl.Buffered(1 if n_j == 1 else 2)),
            pl.BlockSpec((D_SSM, tn), lambda i, j: (1, j), pipeline_mode=pl.Buffered(1 if n_j == 1 else 2)),
        ],
        out_specs=pl.BlockSpec((tm, tn), lambda i, j: (i, j)),
        out_shape=jax.ShapeDtypeStruct((L, D_MODEL), F32),
        scratch_shapes=[pltpu.VMEM((tm, D_ATTN), BF16)],
        compiler_params=_params(("arbitrary", "arbitrary"), 56),
        name="out_projection",
    )(x2, o_attn, ys, attn_norm_w, w_out, w_out)


FFN_HALO = 16


def _ffn_kernel(x_ref, halo_ref, nw_ref, wg_ref, wv_ref, cwg_ref, cwv_ref, cbg_ref, cbv_ref, wd_ref, fw_ref,
                o_ref, hn_ref, ug_ref, uv_ref, acc_ref, *, tm):
    i = pl.program_id(0)
    j = pl.program_id(1)
    H = FFN_HALO

    @pl.when(j == 0)
    def _():
        hn_ref[H:H + tm, :] = _rms(x_ref[...], nw_ref[...]).astype(BF16)
        acc_ref[...] = jnp.zeros_like(acc_ref)

    @pl.when((j == 0) & (i == 0))
    def _():
        hn_ref[0:H, :] = jnp.zeros((H, D_MODEL), BF16)

    @pl.when((j == 0) & (i > 0))
    def _():
        hn_ref[0:H, :] = _rms(halo_ref[...], nw_ref[...]).astype(BF16)

    hn = hn_ref[...]
    ug_ref[...] = _dot(hn, wg_ref[...])
    uv_ref[...] = _dot(hn, wv_ref[...])

    def conv(u_ref, w_ref, b_ref):
        acc = b_ref[...]
        for k in range(FFN_CONV):
            acc = acc + u_ref[pl.ds(H - (FFN_CONV - 1) + k, tm), :] * w_ref[k:k + 1, :]
        return acc

    act = _silu(conv(ug_ref, cwg_ref, cbg_ref)) * conv(uv_ref, cwv_ref, cbv_ref)
    acc_ref[...] += _dot(act.astype(BF16), wd_ref[...])

    @pl.when(j == pl.num_programs(1) - 1)
    def _():
        o_ref[...] = _rms(x_ref[...] + acc_ref[...], fw_ref[...])


def _conv_ffn(x1, norm_w, w_up, conv_w, conv_b, w_down, final_w, tm, tf):
    L = x1.shape[0]
    n_f = D_FF // tf
    hb = tm // FFN_HALO
    return pl.pallas_call(
        functools.partial(_ffn_kernel, tm=tm),
        grid=(L // tm, n_f),
        in_specs=[
            pl.BlockSpec((tm, D_MODEL), lambda i, j: (i, 0)),
            pl.BlockSpec((FFN_HALO, D_MODEL), lambda i, j: (jnp.maximum(i * hb - 1, 0), 0)),
            pl.BlockSpec((1, D_MODEL), lambda i, j: (0, 0)),
            pl.BlockSpec((D_MODEL, tf), lambda i, j: (0, j)),
            pl.BlockSpec((D_MODEL, tf), lambda i, j: (0, n_f + j)),
            pl.BlockSpec((FFN_CONV, tf), lambda i, j: (0, j)),
            pl.BlockSpec((FFN_CONV, tf), lambda i, j: (0, n_f + j)),
            pl.BlockSpec((1, tf), lambda i, j: (0, j)),
            pl.BlockSpec((1, tf), lambda i, j: (0, n_f + j)),
            pl.BlockSpec((tf, D_MODEL), lambda i, j: (j, 0)),
            pl.BlockSpec((1, D_MODEL), lambda i, j: (0, 0)),
        ],
        out_specs=pl.BlockSpec((tm, D_MODEL), lambda i, j: (i, 0)),
        out_shape=jax.ShapeDtypeStruct((L, D_MODEL), F32),
        scratch_shapes=[
            pltpu.VMEM((tm + FFN_HALO, D_MODEL), BF16),
            pltpu.VMEM((tm + FFN_HALO, tf), F32),
            pltpu.VMEM((tm + FFN_HALO, tf), F32),
            pltpu.VMEM((tm, D_MODEL), F32),
        ],
        compiler_params=_params(("arbitrary", "arbitrary"), 56),
        name="conv_ffn",
    )(x1, x1, norm_w, w_up, w_up, conv_w, conv_w, conv_b, conv_b, w_down, final_w)


def _head_expand():
    e = np.zeros((TAIL, D_SSM), np.float32)
    for h in range(SSM_HEADS):
        e[h, h * SSM_HEAD_DIM:(h + 1) * SSM_HEAD_DIM] = 1.0
    return e


def _layer(x2, mix_norm_w, w_in, ssm_conv_w, ssm_conv_b, ssm_dt_bias, ssm_a_log, ssm_d, ssm_norm_w,
           cmp_pos, cmp_w1, cmp_w2_k, cmp_w2_v, attn_norm_w, w_out, ffn_norm_w, w_up, ffn_conv_w, ffn_conv_b,
           w_down, final_norm_w):
    L = x2.shape[0]
    row = lambda v: v.reshape(1, -1).astype(F32)

    bounds = np.cumsum([0, D_SSM, D_SSM + 2 * SSM_GROUPS * SSM_STATE, SSM_HEADS, D_ATTN] + [KV_DIM] * 6
                       + [3 * N_HEADS])
    seg = lambda k: w_in[:, bounds[k]:bounds[k + 1]]
    z_w, xbc_w, dt_w, q_w, kc_w, vc_w, ks_w, vs_w, kw_w, vw_w, gl_w = [seg(k) for k in range(11)]
    w_nat = jnp.concatenate([z_w, xbc_w, kc_w, vc_w, ks_w, kw_w], axis=1).astype(BF16)
    w_tr = jnp.concatenate([q_w, vs_w, vw_w], axis=1).T.astype(BF16)
    w_tail = jnp.concatenate([dt_w, gl_w, jnp.zeros((D_MODEL, TAIL - SSM_HEADS - 3 * N_HEADS), F32)], axis=1)
    w_tail = w_tail.astype(BF16)
    pad_row = lambda v: jnp.concatenate([v.astype(F32), jnp.zeros((TAIL - SSM_HEADS,), F32)]).reshape(1, TAIL)

    p, k_nat, t_out, tail, tail_t = _in_projection(x2, row(mix_norm_w), w_nat, w_tr, w_tail, w_tail.T,
                                                   tm=min(1024, L))
    kc, vct = _compress(p, cmp_pos.astype(F32), cmp_w1.astype(BF16), cmp_w2_k.astype(BF16),
                        cmp_w2_v.T.astype(BF16))

    y_ssm = _ssd(p, tail, tail_t, ssm_conv_w.astype(F32), row(ssm_conv_b), pad_row(ssm_dt_bias),
                 ssm_dt_bias.astype(F32).reshape(SSM_HEADS, 1), pad_row(ssm_a_log),
                 ssm_a_log.astype(F32).reshape(SSM_HEADS, 1),
                 row(jnp.repeat(ssm_d.astype(F32), SSM_HEAD_DIM)), jnp.asarray(_head_expand(), BF16),
                 row(ssm_norm_w))

    n_blk = L // SEL_BLOCK
    slopes = 2.0 ** (-8.0 * jnp.arange(1, N_HEADS + 1, dtype=F32) / N_HEADS)
    ocw, selt, blk_count = _cmp_win_attention(slopes, t_out, kc, vct, k_nat, tail_t, min(N_SEL, n_blk))
    o_attn = _sel_attention(slopes, t_out, k_nat, selt, blk_count[:, :, 0, :], tail_t, ocw)

    x1 = _out_projection(x2, o_attn, y_ssm, row(attn_norm_w), w_out.astype(BF16), tm=512, tn=D_MODEL)
    return _conv_ffn(x1, row(ffn_norm_w), w_up.astype(BF16), ffn_conv_w.astype(F32), row(ffn_conv_b),
                     w_down.astype(BF16), row(final_norm_w), tm=512, tf=512)


def kernel(x, mix_norm_w, w_in, ssm_conv_w, ssm_conv_b, ssm_dt_bias, ssm_a_log, ssm_d, ssm_norm_w, cmp_pos_k,
           cmp_w1_k, cmp_w2_k, cmp_pos_v, cmp_w1_v, cmp_w2_v, attn_norm_w, w_out, ffn_norm_w, w_up, ffn_conv_w,
           ffn_conv_b, w_down, final_norm_w):
    bsz, L, _ = x.shape
    assert bsz == 1 and mix_norm_w.shape[0] == 1, "single sequence, single layer"
    assert L % 1024 == 0 and L >= WINDOW + ATT_TQ
    out = _layer(
        x[0], mix_norm_w[0], w_in[0], ssm_conv_w[0], ssm_conv_b[0], ssm_dt_bias[0], ssm_a_log[0], ssm_d[0],
        ssm_norm_w[0], jnp.stack([cmp_pos_k[0], cmp_pos_v[0]]), jnp.stack([cmp_w1_k[0], cmp_w1_v[0]]),
        cmp_w2_k[0], cmp_w2_v[0], attn_norm_w[0], w_out[0], ffn_norm_w[0], w_up[0], ffn_conv_w[0],
        ffn_conv_b[0], w_down[0], final_norm_w)
    return out[None]
```

```python
import functools

import numpy as np
import jax
import jax.numpy as jnp
from jax import lax
from jax.experimental import pallas as pl
from jax.experimental.pallas import tpu as pltpu

F32 = jnp.float32
BF16 = jnp.bfloat16

D_MODEL = 2048
D_ATTN = 2048
D_SSM = 2048
N_HEADS = 16
HEAD_DIM = 128
N_KV_HEADS = 4
Q_PER_KV = 4
KV_DIM = 512
CMP_LEN = 32
CMP_STRIDE = 16
CMP_HIDDEN = 256
SEL_BLOCK = 64
N_SEL = 16
WINDOW = 512
ATT_TQ = 256
SEL_TQ = 512
SSM_HEAD_DIM = 64
SSM_HEADS = 32
SSM_GROUPS = 4
SSM_STATE = 128
SSM_CONV = 4
SSM_CHUNK = 256
D_FF = 5632
FFN_CONV = 3
NORM_EPS = 1e-6
NEG_INF = -1e30

LOG2E = 1.4426950408889634
M_INIT = -1e20

P_COLS = 6144
K_COLS = 1024
T_ROWS = 3072
TAIL = 128
PROJ_TN = 1024
KC_COL0 = 5120
VC_COL0 = 5632
MIB = 1024 * 1024


def _params(sem, vmem_mib):
    return pltpu.CompilerParams(dimension_semantics=sem, vmem_limit_bytes=vmem_mib * MIB)


def _rms(x, w):
    return x * lax.rsqrt(jnp.mean(x * x, axis=-1, keepdims=True) + NORM_EPS) * w


def _dot(a, b):
    return jnp.dot(a, b, preferred_element_type=F32)


def _dot_nt(a, b):
    return lax.dot_general(a, b, (((1,), (1,)), ((), ())), preferred_element_type=F32)


def _split3(x):
    hi = x.astype(BF16)
    r1 = x - hi.astype(F32)
    mid = r1.astype(BF16)
    lo = (r1 - mid.astype(F32)).astype(BF16)
    return hi, mid, lo


def _dot3(x, m):
    hi, mid, lo = _split3(x)
    return _dot(hi, m) + _dot(mid, m) + _dot(lo, m)


def _dot3_left(m, x):
    hi, mid, lo = _split3(x)
    return _dot(m, hi) + _dot(m, mid) + _dot(m, lo)


def _silu(x):
    return x * (1.0 / (1.0 + jnp.exp(-x)))


def _sigmoid(x):
    return 1.0 / (1.0 + jnp.exp(-x))


def _inproj_kernel(x_ref, nw_ref, w_ref, wtr_ref, wt_ref, wtt_ref, p_ref, k_ref, t_ref, tail_ref, tailt_ref,
                   xn_ref, *, n_p, n_q):
    j = pl.program_id(1)
    hp = PROJ_TN // HEAD_DIM

    @pl.when(j == 0)
    def _():
        xn = _rms(x_ref[...], nw_ref[...]).astype(BF16)
        xn_ref[...] = xn
        tail_ref[...] = _dot(xn, wt_ref[...])
        tailt_ref[...] = _dot_nt(wtt_ref[...], xn)

    @pl.when(j <= n_p)
    def _():
        r = _dot(xn_ref[...], w_ref[...])

        @pl.when(j < n_p)
        def _():
            p_ref[...] = r

        @pl.when(j == n_p)
        def _():
            for c in range(hp):
                k_ref[c] = r[:, c * HEAD_DIM:(c + 1) * HEAD_DIM].astype(BF16)

    @pl.when(j > n_p)
    def _():
        r = _dot_nt(wtr_ref[...], xn_ref[...])
        r = r * jnp.where(j - n_p - 1 < n_q, HEAD_DIM ** -0.5 * LOG2E, 1.0)
        for c in range(hp):
            t_ref[c] = r[c * HEAD_DIM:(c + 1) * HEAD_DIM, :].astype(BF16)


def _in_projection(x2, norm_w, w_nat, w_tr, w_tail, w_tail_t, tm):
    L = x2.shape[0]
    n_p = P_COLS // PROJ_TN
    n_t = T_ROWS // PROJ_TN
    hp = PROJ_TN // HEAD_DIM
    return pl.pallas_call(
        functools.partial(_inproj_kernel, n_p=n_p, n_q=D_ATTN // PROJ_TN),
        grid=(L // tm, n_p + 1 + n_t),
        in_specs=[
            pl.BlockSpec((tm, D_MODEL), lambda i, j: (i, 0), pipeline_mode=pl.Buffered(1)),
            pl.BlockSpec((1, D_MODEL), lambda i, j: (0, 0)),
            pl.BlockSpec((D_MODEL, PROJ_TN), lambda i, j: (0, jnp.minimum(j, n_p))),
            pl.BlockSpec((PROJ_TN, D_MODEL), lambda i, j: (jnp.clip(j - n_p - 1, 0, n_t - 1), 0)),
            pl.BlockSpec((D_MODEL, TAIL), lambda i, j: (0, 0)),
            pl.BlockSpec((TAIL, D_MODEL), lambda i, j: (0, 0)),
        ],
        out_specs=[
            pl.BlockSpec((tm, PROJ_TN), lambda i, j: (i, jnp.minimum(j, n_p - 1))),
            pl.BlockSpec((hp, tm, HEAD_DIM), lambda i, j: (0, i, 0)),
            pl.BlockSpec((hp, HEAD_DIM, tm), lambda i, j: (jnp.clip(j - n_p - 1, 0, n_t - 1), 0, i)),
            pl.BlockSpec((tm, TAIL), lambda i, j: (i, 0)),
            pl.BlockSpec((TAIL, tm), lambda i, j: (0, i)),
        ],
        out_shape=[
            jax.ShapeDtypeStruct((L, P_COLS), F32),
            jax.ShapeDtypeStruct((K_COLS // HEAD_DIM, L, HEAD_DIM), BF16),
            jax.ShapeDtypeStruct((T_ROWS // HEAD_DIM, HEAD_DIM, L), BF16),
            jax.ShapeDtypeStruct((L, TAIL), F32),
            jax.ShapeDtypeStruct((TAIL, L), F32),
        ],
        scratch_shapes=[pltpu.VMEM((tm, D_MODEL), BF16)],
        compiler_params=_params(("arbitrary", "arbitrary"), 56),
        name="in_projection",
    )(x2, norm_w, w_nat, w_tr, w_tail, w_tail_t)


def _compress_hidden(x_ref, pos_ref, w1_ref, n_rows):
    half = CMP_LEN // 2
    acc0 = jnp.zeros((n_rows, CMP_HIDDEN), F32)
    acc1 = jnp.zeros((n_rows, CMP_HIDDEN), F32)
    for i in range(half):
        xi = x_ref[pl.ds(i, n_rows, stride=CMP_STRIDE), :]
        a0 = (xi + pos_ref[i:i + 1, :]).astype(BF16)
        a1 = (xi + pos_ref[half + i:half + i + 1, :]).astype(BF16)
        acc0 = acc0 + _dot(a0, w1_ref[i * HEAD_DIM:(i + 1) * HEAD_DIM, :])
        acc1 = acc1 + _dot(a1, w1_ref[(half + i) * HEAD_DIM:(half + i + 1) * HEAD_DIM, :])
    hid = acc0 + pltpu.roll(acc1, n_rows - 1, 0)
    return jax.nn.gelu(hid).astype(BF16)


def _compress_kernel(xk_ref, xv_ref, pos_ref, w1_ref, w2k_ref, w2vt_ref, kc_ref, vct_ref, *, n_rows):
    hk = _compress_hidden(xk_ref, pos_ref.at[0], w1_ref.at[0], n_rows)
    kc_ref[...] = _dot(hk, w2k_ref[...]).astype(BF16)
    hv = _compress_hidden(xv_ref, pos_ref.at[1], w1_ref.at[1], n_rows)
    vct_ref[...] = _dot_nt(w2vt_ref[...], hv).astype(BF16)


def _compress(p, pos, w1, w2k, w2vt):
    L = p.shape[0]
    n_rows = L // CMP_STRIDE
    full = lambda a: pl.BlockSpec(a.shape, lambda g: (0,) * a.ndim)
    return pl.pallas_call(
        functools.partial(_compress_kernel, n_rows=n_rows),
        grid=(N_KV_HEADS,),
        in_specs=[
            pl.BlockSpec((L, HEAD_DIM), lambda g: (0, KC_COL0 // HEAD_DIM + g)),
            pl.BlockSpec((L, HEAD_DIM), lambda g: (0, VC_COL0 // HEAD_DIM + g)),
            full(pos), full(w1), full(w2k), full(w2vt),
        ],
        out_specs=[
            pl.BlockSpec((None, n_rows, HEAD_DIM), lambda g: (g, 0, 0)),
            pl.BlockSpec((None, HEAD_DIM, n_rows), lambda g: (g, 0, 0)),
        ],
        out_shape=[
            jax.ShapeDtypeStruct((N_KV_HEADS, n_rows, HEAD_DIM), BF16),
            jax.ShapeDtypeStruct((N_KV_HEADS, HEAD_DIM, n_rows), BF16),
        ],
        compiler_params=_params(("arbitrary",), 56),
        name="kv_compress",
    )(p, p, pos, w1, w2k, w2vt)


def _ssd_kernel(xs_ref, bc_ref, xs_halo_ref, bc_halo_ref, z_ref, tail_ref, tailt_ref, cw_ref, cb_ref,
                dtb_row_ref, dtb_col_ref, alog_row_ref, alog_col_ref, dskip_ref, expand_ref, nw_ref,
                o_ref, state_ref, cx_ref, cbc_ref):
    c = pl.program_id(0)
    Q = SSM_CHUNK
    GW = D_SSM // SSM_GROUPS
    halo = 8

    @pl.when(c == 0)
    def _():
        state_ref[...] = jnp.zeros_like(state_ref)
        cx_ref[0:halo, :] = jnp.zeros((halo, D_SSM), F32)
        cbc_ref[0:halo, :] = jnp.zeros((halo, 2 * SSM_GROUPS * SSM_STATE), F32)

    @pl.when(c > 0)
    def _():
        cx_ref[0:halo, :] = xs_halo_ref[...]
        cbc_ref[0:halo, :] = bc_halo_ref[...]

    cx_ref[halo:halo + Q, :] = xs_ref[...]
    cbc_ref[halo:halo + Q, :] = bc_ref[...]

    def conv_silu(ref, w, b):
        acc = b
        for k in range(SSM_CONV):
            acc = acc + ref[pl.ds(halo - (SSM_CONV - 1) + k, Q), :] * w[k:k + 1, :]
        return _silu(acc)

    cw = cw_ref[...]
    cb = cb_ref[...]
    xs = conv_silu(cx_ref, cw[:, :D_SSM], cb[:, :D_SSM])
    bcm = conv_silu(cbc_ref, cw[:, D_SSM:], cb[:, D_SSM:])

    def softplus(v):
        return jnp.maximum(v, 0.0) + jnp.log1p(jnp.exp(-jnp.abs(v)))

    dt = softplus(tail_ref[...] + dtb_row_ref[...])
    a = dt * (-jnp.exp(alog_row_ref[...]))
    rows = lax.broadcasted_iota(jnp.int32, (Q, Q), 0)
    cols = lax.broadcasted_iota(jnp.int32, (Q, Q), 1)
    causal = cols <= rows
    tri = jnp.where(causal, 1.0, 0.0).astype(BF16)
    a_cum = _dot3_left(tri, a)
    dt_t = softplus(tailt_ref[0:SSM_HEADS, :] + dtb_col_ref[...])
    a_t = dt_t * (-jnp.exp(alog_col_ref[...]))
    tri_t = jnp.where(rows <= cols, 1.0, 0.0).astype(BF16)
    a_cum_t = _dot3(a_t, tri_t)

    expand = expand_ref[...]
    dt_e = _dot3(dt, expand)
    ac_e = _dot3(a_cum, expand)
    xdt = xs * dt_e
    decay_to = jnp.exp(ac_e)
    a_last = ac_e[Q - 1:Q, :]
    xdd = xdt * jnp.exp(a_last - ac_e)
    chunk_decay = jnp.exp(a_last)

    lane = lax.broadcasted_iota(jnp.int32, (Q, 2 * SSM_HEAD_DIM), 1)
    first_head = lane < SSM_HEAD_DIM
    y_groups = []
    for g in range(SSM_GROUPS):
        bg = bcm[:, g * SSM_STATE:(g + 1) * SSM_STATE]
        cg = bcm[:, (SSM_GROUPS + g) * SSM_STATE:(SSM_GROUPS + g + 1) * SSM_STATE].astype(BF16)
        gmat = _dot_nt(cg, bg.astype(BF16))
        st = state_ref[g]
        y_off = _dot(cg, st.astype(BF16)) * decay_to[:, g * GW:(g + 1) * GW]
        new = _dot(bg.T.astype(BF16), xdd[:, g * GW:(g + 1) * GW].astype(BF16))
        state_ref[g] = st * chunk_decay[:, g * GW:(g + 1) * GW] + new
        pairs = []
        for pr in range(GW // (2 * SSM_HEAD_DIM)):
            h0 = g * (GW // SSM_HEAD_DIM) + 2 * pr
            xp = xdt[:, h0 * SSM_HEAD_DIM:(h0 + 2) * SSM_HEAD_DIM].astype(BF16)
            ys = []
            for h in (h0, h0 + 1):
                diff = a_cum[:, h:h + 1] - a_cum_t[h:h + 1, :]
                m = (gmat * jnp.exp(jnp.where(causal, diff, NEG_INF))).astype(BF16)
                ys.append(_dot(m, xp))
            pairs.append(jnp.where(first_head, ys[0], ys[1]))
        y_groups.append(jnp.concatenate(pairs, axis=1) + y_off)
    y = jnp.concatenate(y_groups, axis=1) + xs * dskip_ref[...]
    y = y * _silu(z_ref[...])
    o_ref[...] = _rms(y, nw_ref[...]).astype(BF16)


def _ssd(p, tail, tail_t, conv_w, conv_b, dtb_row, dtb_col, alog_row, alog_col, dskip_e, expand, norm_w):
    L = p.shape[0]
    Q = SSM_CHUNK
    bc_w = 2 * SSM_GROUPS * SSM_STATE
    xs_blk = D_SSM // D_SSM
    bc_blk = (2 * D_SSM) // bc_w
    hb = Q // 8
    full = lambda shape: pl.BlockSpec(shape, lambda c: (0,) * len(shape))
    return pl.pallas_call(
        _ssd_kernel,
        grid=(L // Q,),
        in_specs=[
            pl.BlockSpec((Q, D_SSM), lambda c: (c, xs_blk)),
            pl.BlockSpec((Q, bc_w), lambda c: (c, bc_blk)),
            pl.BlockSpec((8, D_SSM), lambda c: (jnp.maximum(c * hb - 1, 0), xs_blk)),
            pl.BlockSpec((8, bc_w), lambda c: (jnp.maximum(c * hb - 1, 0), bc_blk)),
            pl.BlockSpec((Q, D_SSM), lambda c: (c, 0)),
            pl.BlockSpec((Q, TAIL), lambda c: (c, 0)),
            pl.BlockSpec((TAIL, Q), lambda c: (0, c)),
            full(conv_w.shape), full(conv_b.shape), full(dtb_row.shape), full(dtb_col.shape),
            full(alog_row.shape), full(alog_col.shape), full(dskip_e.shape), full(expand.shape),
            full(norm_w.shape),
        ],
        out_specs=pl.BlockSpec((Q, D_SSM), lambda c: (c, 0)),
        out_shape=jax.ShapeDtypeStruct((L, D_SSM), BF16),
        scratch_shapes=[
            pltpu.VMEM((SSM_GROUPS, SSM_STATE, D_SSM // SSM_GROUPS), F32),
            pltpu.VMEM((Q + 8, D_SSM), F32),
            pltpu.VMEM((Q + 8, bc_w), F32),
        ],
        compiler_params=_params(("arbitrary",), 48),
        name="ssd_scan",
    )(p, p, p, p, p, tail, tail_t, conv_w, conv_b, dtb_row, dtb_col, alog_row, alog_col, dskip_e, expand, norm_w)


GATE_ROW0 = SSM_HEADS


def _gate_row(tailt_ref, branch, head):
    return _sigmoid(tailt_ref[pl.ds(GATE_ROW0 + branch * N_HEADS + head, 1), :])


POS_COARSE = 64


def _position_columns(positions, with_block_onehot):
    positions = np.asarray(positions)
    e = np.zeros((positions.shape[0], HEAD_DIM), np.float32)
    e[:, 0:3] = ((positions // POS_COARSE) * POS_COARSE)[:, None]
    e[:, 3:6] = (positions % POS_COARSE)[:, None]
    if with_block_onehot:
        k = np.arange(positions.shape[0])
        e[k, 8 + k // SEL_BLOCK] = 1.0
    return e


def _slope_lane_row(sl, T):
    n_heads = len(sl)
    lane_head = lax.broadcasted_iota(jnp.int32, (1, n_heads * T), 1) // T
    row = jnp.zeros((1, n_heads * T), F32)
    for r in range(n_heads):
        row = jnp.where(lane_head == r, sl[r], row)
    return row


def _slope_rows(sl, T, extra8):
    pieces = [x.astype(F32) for x in _split3(_slope_lane_row(sl, T))]
    top = jnp.concatenate(pieces + pieces + [jnp.zeros((2, len(sl) * T), F32)], axis=0)
    return jnp.concatenate([top, extra8], axis=0).astype(BF16)


def _cmp_win_kernel(slopes_ref, qt_ref, kc_ref, vct_ref, kw_ref, vwt_ref, tailt_ref, posw_ref,
                    ocw_ref, selt_ref, any_ref, s_ref, psum_ref, *, n_cmp_rows, n_blk, n_sel, chunk):
    g = pl.program_id(0)
    qb = pl.program_id(1)
    T = ATT_TQ
    R = Q_PER_KV
    CH = chunk
    t0 = qb * T
    qt4 = jnp.concatenate([qt_ref[r] for r in range(R)], axis=1)
    t_row = t0 + lax.broadcasted_iota(jnp.int32, (1, T), 1)
    sl = [slopes_ref[g * R + r] * LOG2E for r in range(R)]

    n_chunks = (t0 + T - CMP_LEN) // CMP_STRIDE // CH + 1
    end_rel = lax.broadcasted_iota(jnp.int32, (CH, T), 0) * CMP_STRIDE + (CMP_LEN - 1)
    end_rel_f = end_rel.astype(F32)

    def scores(c, m):
        r0 = pl.multiple_of(c * CH, CH)
        s = _dot(kc_ref[pl.ds(r0, CH), :], qt4)
        valid = (t_row - r0 * CMP_STRIDE) >= end_rel
        key_rel = end_rel_f + (r0 * CMP_STRIDE - t0).astype(F32)
        ms = []
        for r in range(R):
            v = jnp.where(valid, s[:, r * T:(r + 1) * T] + sl[r] * key_rel, NEG_INF)
            s_ref[pl.ds(r0, CH), r * T:(r + 1) * T] = v
            ms.append(jnp.max(v, axis=0, keepdims=True))
        return jnp.maximum(m, jnp.concatenate(ms, axis=1))

    m = lax.fori_loop(0, n_chunks, scores, jnp.full((1, R * T), M_INIT, F32))

    def probs(c, carry):
        l, acc = carry
        r0 = pl.multiple_of(c * CH, CH)
        p = jnp.exp2(s_ref[pl.ds(r0, CH), :] - m)
        s_ref[pl.ds(r0, CH), :] = p
        l = l + jnp.sum(p, axis=0, keepdims=True)
        acc = acc + _dot(vct_ref[:, pl.ds(r0, CH)], p.astype(BF16))
        return l, acc

    l, acc = lax.fori_loop(0, n_chunks, probs, (jnp.zeros((1, R * T), F32), jnp.zeros((HEAD_DIM, R * T), F32)))
    inv = jnp.where(l > 0.0, 1.0 / l, 0.0)

    PAD = 8
    LW = 128
    for u in range(T // LW):
        psum_ref[u, 0:PAD, :] = jnp.zeros((PAD, LW), F32)
    for c in range(n_cmp_rows // CH):
        @pl.when(c < n_chunks)
        def _():
            pn = s_ref[c * CH:(c + 1) * CH, :] * inv
            tot = pn[:, 0:T]
            for r in range(1, R):
                tot = tot + pn[:, r * T:(r + 1) * T]
            for u in range(T // LW):
                psum_ref[u, PAD + c * CH:PAD + (c + 1) * CH, :] = tot[:, u * LW:(u + 1) * LW]

        @pl.when(c >= n_chunks)
        def _():
            for u in range(T // LW):
                psum_ref[u, PAD + c * CH:PAD + (c + 1) * CH, :] = jnp.zeros((CH, LW), F32)

    ratio, b_r = SEL_BLOCK // CMP_STRIDE, CMP_LEN // CMP_STRIDE
    imp = jnp.zeros((n_blk, T), F32)
    for shift in range(ratio + b_r - 1):
        mult = sum(1 for mm in range(ratio) for nn in range(b_r) if mm + nn == shift)
        rows = pl.ds(PAD + ratio - 1 - shift, n_blk, stride=ratio)
        imp = imp + float(mult) * jnp.concatenate([psum_ref[u, rows, :] for u in range(T // LW)], axis=1)

    jb = lax.broadcasted_iota(jnp.int32, (n_blk, T), 0)
    jt = t_row // SEL_BLOCK
    forced = (jb == 0) | (jb == jt) | (jb == jt - 1)
    quota = (n_sel - 1 - jnp.minimum(jt, 2)).astype(F32)
    work0 = jnp.where(forced, NEG_INF, jnp.where(jb <= jt, imp, -1.0))

    n_rounds = n_sel - jnp.where(t0 >= 2 * SEL_BLOCK, 3, 1)

    def select_among(rows):
        jbf_v = lax.broadcasted_iota(jnp.int32, (rows, T), 0).astype(F32)

        def pick(i, work):
            top = jnp.max(work, axis=0, keepdims=True)
            first = jnp.min(jnp.where(work == top, jbf_v, float(n_blk)), axis=0, keepdims=True)
            first = jnp.where(i.astype(F32) < quota, first, -1.0)
            return jnp.where(jbf_v == first, NEG_INF, work)

        work = lax.fori_loop(0, n_rounds, pick, work0[:rows])
        selt_ref[0:rows, :] = jnp.where(work == NEG_INF, 1.0, 0.0)
        if rows < n_blk:
            selt_ref[rows:n_blk, :] = jnp.zeros((n_blk - rows, T), F32)

    n_var = min(8, n_blk // 8)
    sec_rows = n_blk // n_var
    section = jnp.minimum(((t0 + T) // SEL_BLOCK - 1) // sec_rows, n_var - 1)
    for var in range(n_var):
        pl.when(section == var)(functools.partial(select_among, (var + 1) * sec_rows))
    any_ref[...] = _dot_nt(jnp.ones((8, T), BF16), selt_ref[...].astype(BF16))

    wlen = WINDOW + T
    start = pl.multiple_of(jnp.maximum(t0 - WINDOW, 0), T)
    q_aug = jnp.concatenate([qt4, _slope_rows(sl, T, jnp.zeros((8, R * T), F32)),
                             jnp.zeros((HEAD_DIM - 16, R * T), BF16)], axis=0)
    s = _dot(jnp.concatenate([kw_ref[pl.ds(start, wlen), :], posw_ref[...]], axis=1), q_aug)
    krow = lax.broadcasted_iota(jnp.int32, (wlen, T), 0)
    dist = (t_row - start) - krow
    valid = (dist >= 0) & (dist < WINDOW)
    vwt = vwt_ref[:, pl.ds(start, wlen)]
    for r in range(R):
        cs = slice(r * T, (r + 1) * T)
        v = jnp.where(valid, s[:, cs], NEG_INF)
        p = jnp.exp2(v - jnp.max(v, axis=0, keepdims=True))
        lw = jnp.sum(p, axis=0, keepdims=True)
        ow = _dot(vwt, p.astype(BF16))
        head = g * R + r
        o_t = acc[:, cs] * (inv[:, cs] * _gate_row(tailt_ref, 0, head)) + ow * (_gate_row(tailt_ref, 2, head) / lw)
        ocw_ref[:, r * HEAD_DIM:(r + 1) * HEAD_DIM] = o_t.T


def _cmp_win_attention(slopes, t_out, kc, vct, k_nat, tail_t, n_sel):
    L = t_out.shape[2]
    n_cmp_rows = L // CMP_STRIDE
    n_blk = L // SEL_BLOCK
    nqb = L // ATT_TQ
    chunk = min(256, n_cmp_rows)
    kern = functools.partial(_cmp_win_kernel, n_cmp_rows=n_cmp_rows, n_blk=n_blk, n_sel=n_sel, chunk=chunk)
    vw_head0 = (D_ATTN + KV_DIM) // HEAD_DIM
    return pl.pallas_call(
        kern,
        grid=(N_KV_HEADS, nqb),
        in_specs=[
            pl.BlockSpec(memory_space=pltpu.SMEM),
            pl.BlockSpec((Q_PER_KV, HEAD_DIM, ATT_TQ), lambda g, i: (g, 0, i)),
            pl.BlockSpec((None, n_cmp_rows, HEAD_DIM), lambda g, i: (g, 0, 0)),
            pl.BlockSpec((None, HEAD_DIM, n_cmp_rows), lambda g, i: (g, 0, 0)),
            pl.BlockSpec((None, L, HEAD_DIM), lambda g, i: (N_KV_HEADS + g, 0, 0)),
            pl.BlockSpec((None, HEAD_DIM, L), lambda g, i: (vw_head0 + g, 0, 0)),
            pl.BlockSpec((TAIL, ATT_TQ), lambda g, i: (0, i)),
            pl.BlockSpec((WINDOW + ATT_TQ, HEAD_DIM), lambda g, i: (0, 0)),
        ],
        out_specs=[
            pl.BlockSpec((ATT_TQ, KV_DIM), lambda g, i: (i, g)),
            pl.BlockSpec((None, None, n_blk, ATT_TQ), lambda g, i: (g, i, 0, 0)),
            pl.BlockSpec((None, None, 8, n_blk), lambda g, i: (g, i, 0, 0)),
        ],
        out_shape=[
            jax.ShapeDtypeStruct((L, D_ATTN), F32),
            jax.ShapeDtypeStruct((N_KV_HEADS, nqb, n_blk, ATT_TQ), F32),
            jax.ShapeDtypeStruct((N_KV_HEADS, nqb, 8, n_blk), F32),
        ],
        scratch_shapes=[
            pltpu.VMEM((n_cmp_rows, Q_PER_KV * ATT_TQ), F32),
            pltpu.VMEM((ATT_TQ // 128, n_cmp_rows + 8, 128), F32),
        ],
        compiler_params=_params(("arbitrary", "arbitrary"), 48),
        name="cmp_win_attention",
    )(slopes, t_out, kc, vct, k_nat, t_out, tail_t,
      jnp.asarray(_position_columns(np.arange(WINDOW + ATT_TQ), False), BF16))


SEL_TK = 512


MASK_BIG = 2.0 ** 100


def _sel_kernel(tiles_ref, counts_ref, slopes_ref, qt_ref, ks_ref, vst_ref, selt_ref, tailt_ref, ocw_ref, posk_ref,
                o_ref, m_ref, l_ref, acc_ref, *, n_tiles_max):
    g = pl.program_id(0)
    qb = pl.program_id(1)
    T = SEL_TQ
    R = Q_PER_KV
    TK = SEL_TK
    bpt = TK // SEL_BLOCK
    t0 = qb * T
    qt4 = jnp.concatenate([qt_ref[r] for r in range(R)], axis=1)
    sl = [slopes_ref[g * R + r] * LOG2E for r in range(R)]
    sl_row = _slope_lane_row(sl, T)
    t_row4 = t0 + lax.broadcasted_iota(jnp.int32, (1, R * T), 1) % T
    zeros_tail = jnp.zeros((HEAD_DIM - 16, R * T), BF16)

    m_ref[...] = jnp.full(m_ref.shape, M_INIT, F32)
    l_ref[...] = jnp.zeros(l_ref.shape, F32)
    acc_ref[...] = jnp.zeros(acc_ref.shape, F32)
    step = g * pl.num_programs(1) + qb

    def visit(n, causal):
        kt = tiles_ref[step * n_tiles_max + n]
        k0 = pl.multiple_of(kt * TK, TK)
        b0 = pl.multiple_of(kt * bpt, bpt)
        sel8 = jnp.concatenate([selt_ref[u, pl.ds(b0, bpt), :] for u in range(T // ATT_TQ)], axis=1)
        mask8 = (sel8 - 1.0) * MASK_BIG
        q_aug = jnp.concatenate([qt4, _slope_rows(sl, T, jnp.concatenate([mask8] * R, axis=1)), zeros_tail], axis=0)
        s = _dot(jnp.concatenate([ks_ref[pl.ds(k0, TK), :], posk_ref[...]], axis=1), q_aug)
        t_rel = t_row4 - k0
        if causal:
            s = jnp.where(lax.broadcasted_iota(jnp.int32, (TK, R * T), 0) <= t_rel, s, NEG_INF)
        col = sl_row * t_rel.astype(F32)
        m_old = m_ref[...]
        m_new = jnp.maximum(m_old, jnp.max(s, axis=0, keepdims=True) - col)
        alpha = jnp.exp2(m_old - m_new)
        p = jnp.exp2(s - (m_new + col))
        l_ref[...] = alpha * l_ref[...] + jnp.sum(p, axis=0, keepdims=True)
        acc_ref[...] = alpha * acc_ref[...] + _dot(vst_ref[:, pl.ds(k0, TK)], p.astype(BF16))
        m_ref[...] = m_new

    count = counts_ref[step]

    def past_tile(n, carry):
        visit(n, causal=False)
        return carry

    lax.fori_loop(0, count - 1, past_tile, 0)
    visit(count - 1, causal=True)
    l = l_ref[...]
    inv = jnp.where(l > 0.0, 1.0 / l, 0.0)
    for r in range(R):
        cs = slice(r * T, (r + 1) * T)
        o_t = acc_ref[:, cs] * (inv[:, cs] * _gate_row(tailt_ref, 1, g * R + r))
        hs = slice(r * HEAD_DIM, (r + 1) * HEAD_DIM)
        o_ref[:, hs] = ocw_ref[:, hs] + o_t.T


def _active_tiles(blk_count, L):
    G = blk_count.shape[0]
    nqb = L // SEL_TQ
    n_t = L // SEL_TK
    flags = blk_count.reshape(G, nqb, SEL_TQ // ATT_TQ, n_t, SEL_TK // SEL_BLOCK).max(axis=(2, 4)) > 0.5
    kt = jnp.arange(n_t, dtype=jnp.int32)
    last = (jnp.arange(nqb, dtype=jnp.int32) * SEL_TQ + SEL_TQ - 1) // SEL_TK
    flags = flags & (kt[None, None, :] <= last[None, :, None])
    order = jnp.sort(jnp.where(flags, kt, n_t + kt), axis=-1)
    tiles = jnp.where(order < n_t, order, 0).astype(jnp.int32)
    return tiles.reshape(-1), flags.sum(axis=-1).astype(jnp.int32).reshape(-1)


def _sel_attention(slopes, t_out, k_nat, selt, blk_count, tail_t, ocw):
    L = t_out.shape[2]
    n_blk = L // SEL_BLOCK
    n_t = L // SEL_TK
    cols = Q_PER_KV * SEL_TQ
    vs_head0 = D_ATTN // HEAD_DIM
    tiles, counts = _active_tiles(blk_count, L)
    grid_spec = pltpu.PrefetchScalarGridSpec(
        num_scalar_prefetch=2,
        grid=(N_KV_HEADS, L // SEL_TQ),
        in_specs=[
            pl.BlockSpec(memory_space=pltpu.SMEM),
            pl.BlockSpec((Q_PER_KV, HEAD_DIM, SEL_TQ), lambda g, i, *_: (g, 0, i)),
            pl.BlockSpec((None, L, HEAD_DIM), lambda g, i, *_: (g, 0, 0)),
            pl.BlockSpec((None, HEAD_DIM, L), lambda g, i, *_: (vs_head0 + g, 0, 0)),
            pl.BlockSpec((None, SEL_TQ // ATT_TQ, n_blk, ATT_TQ), lambda g, i, *_: (g, i, 0, 0)),
            pl.BlockSpec((TAIL, SEL_TQ), lambda g, i, *_: (0, i)),
            pl.BlockSpec((SEL_TQ, KV_DIM), lambda g, i, *_: (i, g)),
            pl.BlockSpec((SEL_TK, HEAD_DIM), lambda g, i, *_: (0, 0)),
        ],
        out_specs=pl.BlockSpec((SEL_TQ, KV_DIM), lambda g, i, *_: (i, g)),
        scratch_shapes=[
            pltpu.VMEM((1, cols), F32),
            pltpu.VMEM((1, cols), F32),
            pltpu.VMEM((HEAD_DIM, cols), F32),
        ],
    )
    return pl.pallas_call(
        functools.partial(_sel_kernel, n_tiles_max=n_t),
        grid_spec=grid_spec,
        out_shape=jax.ShapeDtypeStruct((L, D_ATTN), F32),
        compiler_params=_params(("arbitrary", "arbitrary"), 48),
        name="selected_attention",
    )(tiles, counts, slopes, t_out, k_nat, t_out, selt, tail_t, ocw,
      jnp.asarray(_position_columns(np.arange(SEL_TK), True), BF16))


def _outproj_kernel(x_ref, oa_ref, ys_ref, nw_ref, wa_ref, ws_ref, o_ref, ya_ref):
    @pl.when(pl.program_id(1) == 0)
    def _():
        ya_ref[...] = _rms(oa_ref[...], nw_ref[...]).astype(BF16)

    o_ref[...] = x_ref[...] + _dot(ya_ref[...], wa_ref[...]) + _dot(ys_ref[...], ws_ref[...])


def _out_projection(x2, o_attn, ys, attn_norm_w, w_out, tm, tn):
    L = x2.shape[0]
    n_j = D_MODEL // tn
    return pl.pallas_call(
        _outproj_kernel,
        grid=(L // tm, n_j),
        in_specs=[
            pl.BlockSpec((tm, tn), lambda i, j: (i, j)),
            pl.BlockSpec((tm, D_ATTN), lambda i, j: (i, 0)),
            pl.BlockSpec((tm, D_SSM), lambda i, j: (i, 0)),
            pl.BlockSpec((1, D_ATTN), lambda i, j: (0, 0)),
            pl.BlockSpec((D_ATTN, tn), lambda i, j: (0, j), pipeline_mode=pl.Buffered(1 if n_j == 1 else 2)),
            pl.BlockSpec((D_SSM, tn), lambda i, j: (1, j), pipeline_mode=pl.Buffered(1 if n_j == 1 else 2)),
        ],
        out_specs=pl.BlockSpec((tm, tn), lambda i, j: (i, j)),
        out_shape=jax.ShapeDtypeStruct((L, D_MODEL), F32),
        scratch_shapes=[pltpu.VMEM((tm, D_ATTN), BF16)],
        compiler_params=_params(("arbitrary", "arbitrary"), 56),
        name="out_projection",
    )(x2, o_attn, ys, attn_norm_w, w_out, w_out)


FFN_HALO = 16


def _ffn_kernel(x_ref, halo_ref, nw_ref, wg_ref, wv_ref, cwg_ref, cwv_ref, cbg_ref, cbv_ref, wd_ref, fw_ref,
                o_ref, hn_ref, ug_ref, uv_ref, acc_ref, *, tm):
    i = pl.program_id(0)
    j = pl.program_id(1)
    H = FFN_HALO

    @pl.when(j == 0)
    def _():
        hn_ref[H:H + tm, :] = _rms(x_ref[...], nw_ref[...]).astype(BF16)
        acc_ref[...] = jnp.zeros_like(acc_ref)

    @pl.when((j == 0) & (i == 0))
    def _():
        hn_ref[0:H, :] = jnp.zeros((H, D_MODEL), BF16)

    @pl.when((j == 0) & (i > 0))
    def _():
        hn_ref[0:H, :] = _rms(halo_ref[...], nw_ref[...]).astype(BF16)

    hn = hn_ref[...]
    ug_ref[...] = _dot(hn, wg_ref[...])
    uv_ref[...] = _dot(hn, wv_ref[...])

    def conv(u_ref, w_ref, b_ref):
        acc = b_ref[...]
        for k in range(FFN_CONV):
            acc = acc + u_ref[pl.ds(H - (FFN_CONV - 1) + k, tm), :] * w_ref[k:k + 1, :]
        return acc

    act = _silu(conv(ug_ref, cwg_ref, cbg_ref)) * conv(uv_ref, cwv_ref, cbv_ref)
    acc_ref[...] += _dot(act.astype(BF16), wd_ref[...])

    @pl.when(j == pl.num_programs(1) - 1)
    def _():
        o_ref[...] = _rms(x_ref[...] + acc_ref[...], fw_ref[...])


def _conv_ffn(x1, norm_w, w_up, conv_w, conv_b, w_down, final_w, tm, tf):
    L = x1.shape[0]
    n_f = D_FF // tf
    hb = tm // FFN_HALO
    return pl.pallas_call(
        functools.partial(_ffn_kernel, tm=tm),
        grid=(L // tm, n_f),
        in_specs=[
            pl.BlockSpec((tm, D_MODEL), lambda i, j: (i, 0)),
            pl.BlockSpec((FFN_HALO, D_MODEL), lambda i, j: (jnp.maximum(i * hb - 1, 0), 0)),
            pl.BlockSpec((1, D_MODEL), lambda i, j: (0, 0)),
            pl.BlockSpec((D_MODEL, tf), lambda i, j: (0, j)),
            pl.BlockSpec((D_MODEL, tf), lambda i, j: (0, n_f + j)),
            pl.BlockSpec((FFN_CONV, tf), lambda i, j: (0, j)),
            pl.BlockSpec((FFN_CONV, tf), lambda i, j: (0, n_f + j)),
            pl.BlockSpec((1, tf), lambda i, j: (0, j)),
            pl.BlockSpec((1, tf), lambda i, j: (0, n_f + j)),
            pl.BlockSpec((tf, D_MODEL), lambda i, j: (j, 0)),
            pl.BlockSpec((1, D_MODEL), lambda i, j: (0, 0)),
        ],
        out_specs=pl.BlockSpec((tm, D_MODEL), lambda i, j: (i, 0)),
        out_shape=jax.ShapeDtypeStruct((L, D_MODEL), F32),
        scratch_shapes=[
            pltpu.VMEM((tm + FFN_HALO, D_MODEL), BF16),
            pltpu.VMEM((tm + FFN_HALO, tf), F32),
            pltpu.VMEM((tm + FFN_HALO, tf), F32),
            pltpu.VMEM((tm, D_MODEL), F32),
        ],
        compiler_params=_params(("arbitrary", "arbitrary"), 56),
        name="conv_ffn",
    )(x1, x1, norm_w, w_up, w_up, conv_w, conv_w, conv_b, conv_b, w_down, final_w)


def _head_expand():
    e = np.zeros((TAIL, D_SSM), np.float32)
    for h in range(SSM_HEADS):
        e[h, h * SSM_HEAD_DIM:(h + 1) * SSM_HEAD_DIM] = 1.0
    return e


def _layer(x2, mix_norm_w, w_in, ssm_conv_w, ssm_conv_b, ssm_dt_bias, ssm_a_log, ssm_d, ssm_norm_w,
           cmp_pos, cmp_w1, cmp_w2_k, cmp_w2_v, attn_norm_w, w_out, ffn_norm_w, w_up, ffn_conv_w, ffn_conv_b,
           w_down, final_norm_w):
    L = x2.shape[0]
    row = lambda v: v.reshape(1, -1).astype(F32)

    bounds = np.cumsum([0, D_SSM, D_SSM + 2 * SSM_GROUPS * SSM_STATE, SSM_HEADS, D_ATTN] + [KV_DIM] * 6
                       + [3 * N_HEADS])
    seg = lambda k: w_in[:, bounds[k]:bounds[k + 1]]
    z_w, xbc_w, dt_w, q_w, kc_w, vc_w, ks_w, vs_w, kw_w, vw_w, gl_w = [seg(k) for k in range(11)]
    w_nat = jnp.concatenate([z_w, xbc_w, kc_w, vc_w, ks_w, kw_w], axis=1).astype(BF16)
    w_tr = jnp.concatenate([q_w, vs_w, vw_w], axis=1).astype(BF16).T
    w_tail = jnp.concatenate([dt_w, gl_w, jnp.zeros((D_MODEL, TAIL - SSM_HEADS - 3 * N_HEADS), F32)], axis=1)
    w_tail = w_tail.astype(BF16)
    pad_row = lambda v: jnp.concatenate([v.astype(F32), jnp.zeros((TAIL - SSM_HEADS,), F32)]).reshape(1, TAIL)

    p, k_nat, t_out, tail, tail_t = _in_projection(x2, row(mix_norm_w), w_nat, w_tr, w_tail, w_tail.T,
                                                   tm=min(1024, L))
    kc, vct = _compress(p, cmp_pos.astype(F32), cmp_w1.astype(BF16), cmp_w2_k.astype(BF16),
                        cmp_w2_v.T.astype(BF16))

    y_ssm = _ssd(p, tail, tail_t, ssm_conv_w.astype(F32), row(ssm_conv_b), pad_row(ssm_dt_bias),
                 ssm_dt_bias.astype(F32).reshape(SSM_HEADS, 1), pad_row(ssm_a_log),
                 ssm_a_log.astype(F32).reshape(SSM_HEADS, 1),
                 row(jnp.repeat(ssm_d.astype(F32), SSM_HEAD_DIM)), jnp.asarray(_head_expand(), BF16),
                 row(ssm_norm_w))

    n_blk = L // SEL_BLOCK
    slopes = 2.0 ** (-8.0 * jnp.arange(1, N_HEADS + 1, dtype=F32) / N_HEADS)
    ocw, selt, blk_count = _cmp_win_attention(slopes, t_out, kc, vct, k_nat, tail_t, min(N_SEL, n_blk))
    o_attn = _sel_attention(slopes, t_out, k_nat, selt, blk_count[:, :, 0, :], tail_t, ocw)

    x1 = _out_projection(x2, o_attn, y_ssm, row(attn_norm_w), w_out.astype(BF16), tm=512, tn=D_MODEL)
    return _conv_ffn(x1, row(ffn_norm_w), w_up.astype(BF16), ffn_conv_w.astype(F32), row(ffn_conv_b),
                     w_down.astype(BF16), row(final_norm_w), tm=512, tf=512)


def kernel(x, mix_norm_w, w_in, ssm_conv_w, ssm_conv_b, ssm_dt_bias, ssm_a_log, ssm_d, ssm_norm_w, cmp_pos_k,
           cmp_w1_k, cmp_w2_k, cmp_pos_v, cmp_w1_v, cmp_w2_v, attn_norm_w, w_out, ffn_norm_w, w_up, ffn_conv_w,
           ffn_conv_b, w_down, final_norm_w):
    bsz, L, _ = x.shape
    assert bsz == 1 and mix_norm_w.shape[0] == 1, "single sequence, single layer"
    assert L % 1024 == 0 and L >= WINDOW + ATT_TQ
    out = _layer(
        x[0], mix_norm_w[0], w_in[0], ssm_conv_w[0], ssm_conv_b[0], ssm_dt_bias[0], ssm_a_log[0], ssm_d[0],
        ssm_norm_w[0], jnp.stack([cmp_pos_k[0], cmp_pos_v[0]]), jnp.stack([cmp_w1_k[0], cmp_w1_v[0]]),
        cmp_w2_k[0], cmp_w2_v[0], attn_norm_w[0], w_out[0], ffn_norm_w[0], w_up[0], ffn_conv_w[0],
        ffn_conv_b[0], w_down[0], final_norm_w)
    return out[None]
```

```python
import functools

import numpy as np
import jax
import jax.numpy as jnp
from jax import lax
from jax.experimental import pallas as pl
from jax.experimental.pallas import tpu as pltpu

F32 = jnp.float32
BF16 = jnp.bfloat16

D_MODEL = 2048
D_ATTN = 2048
D_SSM = 2048
N_HEADS = 16
HEAD_DIM = 128
N_KV_HEADS = 4
Q_PER_KV = 4
KV_DIM = 512
CMP_LEN = 32
CMP_STRIDE = 16
CMP_HIDDEN = 256
SEL_BLOCK = 64
N_SEL = 16
WINDOW = 512
ATT_TQ = 256
SEL_TQ = 512
SSM_HEAD_DIM = 64
SSM_HEADS = 32
SSM_GROUPS = 4
SSM_STATE = 128
SSM_CONV = 4
SSM_CHUNK = 256
D_FF = 5632
FFN_CONV = 3
NORM_EPS = 1e-6
NEG_INF = -1e30
FORCE_SCORE = 1e4

LOG2E = 1.4426950408889634
M_INIT = -1e20

P_COLS = 6144
K_COLS = 1024
T_ROWS = 3072
TAIL = 128
PROJ_TN = 1024
KC_COL0 = 5120
VC_COL0 = 5632
MIB = 1024 * 1024


def _params(sem, vmem_mib):
    return pltpu.CompilerParams(dimension_semantics=sem, vmem_limit_bytes=vmem_mib * MIB)


def _rms(x, w):
    return x * lax.rsqrt(jnp.mean(x * x, axis=-1, keepdims=True) + NORM_EPS) * w


def _dot(a, b):
    return jnp.dot(a, b, preferred_element_type=F32)


def _dot_nt(a, b):
    return lax.dot_general(a, b, (((1,), (1,)), ((), ())), preferred_element_type=F32)


def _split3(x):
    hi = x.astype(BF16)
    r1 = x - hi.astype(F32)
    mid = r1.astype(BF16)
    lo = (r1 - mid.astype(F32)).astype(BF16)
    return hi, mid, lo


def _dot3(x, m):
    hi, mid, lo = _split3(x)
    return _dot(hi, m) + _dot(mid, m) + _dot(lo, m)


def _dot3_left(m, x):
    hi, mid, lo = _split3(x)
    return _dot(m, hi) + _dot(m, mid) + _dot(m, lo)


def _silu(x):
    return x * (1.0 / (1.0 + jnp.exp(-x)))


def _sigmoid(x):
    return 1.0 / (1.0 + jnp.exp(-x))


def _inproj_kernel(x_ref, nw_ref, w_ref, wtr_ref, wt_ref, wtt_ref, p_ref, k_ref, t_ref, tail_ref, tailt_ref,
                   xn_ref, *, n_p, n_q):
    j = pl.program_id(1)
    hp = PROJ_TN // HEAD_DIM

    @pl.when(j == 0)
    def _():
        xn = _rms(x_ref[...], nw_ref[...]).astype(BF16)
        xn_ref[...] = xn
        tail_ref[...] = _dot(xn, wt_ref[...])
        tailt_ref[...] = _dot_nt(wtt_ref[...], xn)

    @pl.when(j <= n_p)
    def _():
        r = _dot(xn_ref[...], w_ref[...])

        @pl.when(j < n_p)
        def _():
            p_ref[...] = r

        @pl.when(j == n_p)
        def _():
            for c in range(hp):
                k_ref[c] = r[:, c * HEAD_DIM:(c + 1) * HEAD_DIM].astype(BF16)

    @pl.when(j > n_p)
    def _():
        r = _dot_nt(wtr_ref[...], xn_ref[...])
        r = r * jnp.where(j - n_p - 1 < n_q, HEAD_DIM ** -0.5 * LOG2E, 1.0)
        for c in range(hp):
            t_ref[c] = r[c * HEAD_DIM:(c + 1) * HEAD_DIM, :].astype(BF16)


def _in_projection(x2, norm_w, w_nat, w_tr, w_tail, w_tail_t, tm):
    L = x2.shape[0]
    n_p = P_COLS // PROJ_TN
    n_t = T_ROWS // PROJ_TN
    hp = PROJ_TN // HEAD_DIM
    return pl.pallas_call(
        functools.partial(_inproj_kernel, n_p=n_p, n_q=D_ATTN // PROJ_TN),
        grid=(L // tm, n_p + 1 + n_t),
        in_specs=[
            pl.BlockSpec((tm, D_MODEL), lambda i, j: (i, 0), pipeline_mode=pl.Buffered(1)),
            pl.BlockSpec((1, D_MODEL), lambda i, j: (0, 0)),
            pl.BlockSpec((D_MODEL, PROJ_TN), lambda i, j: (0, jnp.minimum(j, n_p))),
            pl.BlockSpec((PROJ_TN, D_MODEL), lambda i, j: (jnp.clip(j - n_p - 1, 0, n_t - 1), 0)),
            pl.BlockSpec((D_MODEL, TAIL), lambda i, j: (0, 0)),
            pl.BlockSpec((TAIL, D_MODEL), lambda i, j: (0, 0)),
        ],
        out_specs=[
            pl.BlockSpec((tm, PROJ_TN), lambda i, j: (i, jnp.minimum(j, n_p - 1))),
            pl.BlockSpec((hp, tm, HEAD_DIM), lambda i, j: (0, i, 0)),
            pl.BlockSpec((hp, HEAD_DIM, tm), lambda i, j: (jnp.clip(j - n_p - 1, 0, n_t - 1), 0, i)),
            pl.BlockSpec((tm, TAIL), lambda i, j: (i, 0)),
            pl.BlockSpec((TAIL, tm), lambda i, j: (0, i)),
        ],
        out_shape=[
            jax.ShapeDtypeStruct((L, P_COLS), F32),
            jax.ShapeDtypeStruct((K_COLS // HEAD_DIM, L, HEAD_DIM), BF16),
            jax.ShapeDtypeStruct((T_ROWS // HEAD_DIM, HEAD_DIM, L), BF16),
            jax.ShapeDtypeStruct((L, TAIL), F32),
            jax.ShapeDtypeStruct((TAIL, L), F32),
        ],
        scratch_shapes=[pltpu.VMEM((tm, D_MODEL), BF16)],
        compiler_params=_params(("arbitrary", "arbitrary"), 56),
        name="in_projection",
    )(x2, norm_w, w_nat, w_tr, w_tail, w_tail_t)


def _compress_hidden(x_ref, pos_ref, w1_ref, n_rows):
    half = CMP_LEN // 2
    acc0 = jnp.zeros((n_rows, CMP_HIDDEN), F32)
    acc1 = jnp.zeros((n_rows, CMP_HIDDEN), F32)
    for i in range(half):
        xi = x_ref[pl.ds(i, n_rows, stride=CMP_STRIDE), :]
        a0 = (xi + pos_ref[i:i + 1, :]).astype(BF16)
        a1 = (xi + pos_ref[half + i:half + i + 1, :]).astype(BF16)
        acc0 = acc0 + _dot(a0, w1_ref[i * HEAD_DIM:(i + 1) * HEAD_DIM, :])
        acc1 = acc1 + _dot(a1, w1_ref[(half + i) * HEAD_DIM:(half + i + 1) * HEAD_DIM, :])
    hid = acc0 + pltpu.roll(acc1, n_rows - 1, 0)
    return jax.nn.gelu(hid).astype(BF16)


def _compress_kernel(xk_ref, xv_ref, pos_ref, w1_ref, w2k_ref, w2vt_ref, kc_ref, vct_ref, *, n_rows):
    hk = _compress_hidden(xk_ref, pos_ref.at[0], w1_ref.at[0], n_rows)
    kc_ref[...] = _dot(hk, w2k_ref[...]).astype(BF16)
    hv = _compress_hidden(xv_ref, pos_ref.at[1], w1_ref.at[1], n_rows)
    vct_ref[...] = _dot_nt(w2vt_ref[...], hv).astype(BF16)


def _compress(p, pos, w1, w2k, w2vt):
    L = p.shape[0]
    n_rows = L // CMP_STRIDE
    full = lambda a: pl.BlockSpec(a.shape, lambda g: (0,) * a.ndim)
    return pl.pallas_call(
        functools.partial(_compress_kernel, n_rows=n_rows),
        grid=(N_KV_HEADS,),
        in_specs=[
            pl.BlockSpec((L, HEAD_DIM), lambda g: (0, KC_COL0 // HEAD_DIM + g)),
            pl.BlockSpec((L, HEAD_DIM), lambda g: (0, VC_COL0 // HEAD_DIM + g)),
            full(pos), full(w1), full(w2k), full(w2vt),
        ],
        out_specs=[
            pl.BlockSpec((None, n_rows, HEAD_DIM), lambda g: (g, 0, 0)),
            pl.BlockSpec((None, HEAD_DIM, n_rows), lambda g: (g, 0, 0)),
        ],
        out_shape=[
            jax.ShapeDtypeStruct((N_KV_HEADS, n_rows, HEAD_DIM), BF16),
            jax.ShapeDtypeStruct((N_KV_HEADS, HEAD_DIM, n_rows), BF16),
        ],
        compiler_params=_params(("arbitrary",), 56),
        name="kv_compress",
    )(p, p, pos, w1, w2k, w2vt)


def _ssd_kernel(xs_ref, bc_ref, xs_halo_ref, bc_halo_ref, z_ref, tail_ref, tailt_ref, cw_ref, cb_ref,
                dtb_row_ref, dtb_col_ref, alog_row_ref, alog_col_ref, dskip_ref, expand_ref, nw_ref,
                o_ref, state_ref, cx_ref, cbc_ref):
    c = pl.program_id(0)
    Q = SSM_CHUNK
    GW = D_SSM // SSM_GROUPS
    halo = 8

    @pl.when(c == 0)
    def _():
        state_ref[...] = jnp.zeros_like(state_ref)
        cx_ref[0:halo, :] = jnp.zeros((halo, D_SSM), F32)
        cbc_ref[0:halo, :] = jnp.zeros((halo, 2 * SSM_GROUPS * SSM_STATE), F32)

    @pl.when(c > 0)
    def _():
        cx_ref[0:halo, :] = xs_halo_ref[...]
        cbc_ref[0:halo, :] = bc_halo_ref[...]

    cx_ref[halo:halo + Q, :] = xs_ref[...]
    cbc_ref[halo:halo + Q, :] = bc_ref[...]

    def conv_silu(ref, w, b):
        acc = b
        for k in range(SSM_CONV):
            acc = acc + ref[pl.ds(halo - (SSM_CONV - 1) + k, Q), :] * w[k:k + 1, :]
        return _silu(acc)

    cw = cw_ref[...]
    cb = cb_ref[...]
    xs = conv_silu(cx_ref, cw[:, :D_SSM], cb[:, :D_SSM])
    bcm = conv_silu(cbc_ref, cw[:, D_SSM:], cb[:, D_SSM:])

    def softplus(v):
        return jnp.maximum(v, 0.0) + jnp.log1p(jnp.exp(-jnp.abs(v)))

    dt = softplus(tail_ref[...] + dtb_row_ref[...])
    a = dt * (-jnp.exp(alog_row_ref[...]))
    rows = lax.broadcasted_iota(jnp.int32, (Q, Q), 0)
    cols = lax.broadcasted_iota(jnp.int32, (Q, Q), 1)
    causal = cols <= rows
    tri = jnp.where(causal, 1.0, 0.0).astype(BF16)
    a_cum = _dot3_left(tri, a)
    dt_t = softplus(tailt_ref[0:SSM_HEADS, :] + dtb_col_ref[...])
    a_t = dt_t * (-jnp.exp(alog_col_ref[...]))
    tri_t = jnp.where(rows <= cols, 1.0, 0.0).astype(BF16)
    a_cum_t = _dot3(a_t, tri_t)

    expand = expand_ref[...]
    dt_e = _dot3(dt, expand)
    ac_e = _dot3(a_cum, expand)
    xdt = xs * dt_e
    decay_to = jnp.exp(ac_e)
    a_last = ac_e[Q - 1:Q, :]
    xdd = xdt * jnp.exp(a_last - ac_e)
    chunk_decay = jnp.exp(a_last)

    lane = lax.broadcasted_iota(jnp.int32, (Q, 2 * SSM_HEAD_DIM), 1)
    first_head = lane < SSM_HEAD_DIM
    y_groups = []
    for g in range(SSM_GROUPS):
        bg = bcm[:, g * SSM_STATE:(g + 1) * SSM_STATE]
        cg = bcm[:, (SSM_GROUPS + g) * SSM_STATE:(SSM_GROUPS + g + 1) * SSM_STATE].astype(BF16)
        gmat = _dot_nt(cg, bg.astype(BF16))
        st = state_ref[g]
        y_off = _dot(cg, st.astype(BF16)) * decay_to[:, g * GW:(g + 1) * GW]
        new = _dot(bg.T.astype(BF16), xdd[:, g * GW:(g + 1) * GW].astype(BF16))
        state_ref[g] = st * chunk_decay[:, g * GW:(g + 1) * GW] + new
        pairs = []
        for pr in range(GW // (2 * SSM_HEAD_DIM)):
            h0 = g * (GW // SSM_HEAD_DIM) + 2 * pr
            xp = xdt[:, h0 * SSM_HEAD_DIM:(h0 + 2) * SSM_HEAD_DIM].astype(BF16)
            ys = []
            for h in (h0, h0 + 1):
                diff = a_cum[:, h:h + 1] - a_cum_t[h:h + 1, :]
                m = (gmat * jnp.exp(jnp.where(causal, diff, NEG_INF))).astype(BF16)
                ys.append(_dot(m, xp))
            pairs.append(jnp.where(first_head, ys[0], ys[1]))
        y_groups.append(jnp.concatenate(pairs, axis=1) + y_off)
    y = jnp.concatenate(y_groups, axis=1) + xs * dskip_ref[...]
    y = y * _silu(z_ref[...])
    o_ref[...] = _rms(y, nw_ref[...]).astype(BF16)


def _ssd(p, tail, tail_t, conv_w, conv_b, dtb_row, dtb_col, alog_row, alog_col, dskip_e, expand, norm_w):
    L = p.shape[0]
    Q = SSM_CHUNK
    bc_w = 2 * SSM_GROUPS * SSM_STATE
    xs_blk = D_SSM // D_SSM
    bc_blk = (2 * D_SSM) // bc_w
    hb = Q // 8
    full = lambda shape: pl.BlockSpec(shape, lambda c: (0,) * len(shape))
    return pl.pallas_call(
        _ssd_kernel,
        grid=(L // Q,),
        in_specs=[
            pl.BlockSpec((Q, D_SSM), lambda c: (c, xs_blk)),
            pl.BlockSpec((Q, bc_w), lambda c: (c, bc_blk)),
            pl.BlockSpec((8, D_SSM), lambda c: (jnp.maximum(c * hb - 1, 0), xs_blk)),
            pl.BlockSpec((8, bc_w), lambda c: (jnp.maximum(c * hb - 1, 0), bc_blk)),
            pl.BlockSpec((Q, D_SSM), lambda c: (c, 0)),
            pl.BlockSpec((Q, TAIL), lambda c: (c, 0)),
            pl.BlockSpec((TAIL, Q), lambda c: (0, c)),
            full(conv_w.shape), full(conv_b.shape), full(dtb_row.shape), full(dtb_col.shape),
            full(alog_row.shape), full(alog_col.shape), full(dskip_e.shape), full(expand.shape),
            full(norm_w.shape),
        ],
        out_specs=pl.BlockSpec((Q, D_SSM), lambda c: (c, 0)),
        out_shape=jax.ShapeDtypeStruct((L, D_SSM), BF16),
        scratch_shapes=[
            pltpu.VMEM((SSM_GROUPS, SSM_STATE, D_SSM // SSM_GROUPS), F32),
            pltpu.VMEM((Q + 8, D_SSM), F32),
            pltpu.VMEM((Q + 8, bc_w), F32),
        ],
        compiler_params=_params(("arbitrary",), 48),
        name="ssd_scan",
    )(p, p, p, p, p, tail, tail_t, conv_w, conv_b, dtb_row, dtb_col, alog_row, alog_col, dskip_e, expand, norm_w)


GATE_ROW0 = SSM_HEADS


def _gate_row(tailt_ref, branch, head):
    return _sigmoid(tailt_ref[pl.ds(GATE_ROW0 + branch * N_HEADS + head, 1), :])


POS_COARSE = 64


def _position_columns(positions, with_block_onehot):
    positions = np.asarray(positions)
    e = np.zeros((positions.shape[0], HEAD_DIM), np.float32)
    e[:, 0:3] = ((positions // POS_COARSE) * POS_COARSE)[:, None]
    e[:, 3:6] = (positions % POS_COARSE)[:, None]
    if with_block_onehot:
        k = np.arange(positions.shape[0])
        e[k, 8 + k // SEL_BLOCK] = 1.0
    return e


def _slope_lane_row(sl, T):
    n_heads = len(sl)
    lane_head = lax.broadcasted_iota(jnp.int32, (1, n_heads * T), 1) // T
    row = jnp.zeros((1, n_heads * T), F32)
    for r in range(n_heads):
        row = jnp.where(lane_head == r, sl[r], row)
    return row


def _slope_rows(sl, T, extra8):
    pieces = [x.astype(F32) for x in _split3(_slope_lane_row(sl, T))]
    top = jnp.concatenate(pieces + pieces + [jnp.zeros((2, len(sl) * T), F32)], axis=0)
    return jnp.concatenate([top, extra8], axis=0).astype(BF16)


def _cmp_win_kernel(slopes_ref, qt_ref, kc_ref, vct_ref, kw_ref, vwt_ref, tailt_ref, posw_ref,
                    ocw_ref, selt_ref, any_ref, s_ref, psum_ref, *, n_cmp_rows, n_blk, n_sel, chunk):
    g = pl.program_id(0)
    qb = pl.program_id(1)
    T = ATT_TQ
    R = Q_PER_KV
    CH = chunk
    t0 = qb * T
    qt4 = jnp.concatenate([qt_ref[r] for r in range(R)], axis=1)
    t_row = t0 + lax.broadcasted_iota(jnp.int32, (1, T), 1)
    sl = [slopes_ref[g * R + r] * LOG2E for r in range(R)]

    n_chunks = (t0 + T - CMP_LEN) // CMP_STRIDE // CH + 1
    end_rel = lax.broadcasted_iota(jnp.int32, (CH, T), 0) * CMP_STRIDE + (CMP_LEN - 1)
    end_rel_f = end_rel.astype(F32)

    def scores(c, m):
        r0 = pl.multiple_of(c * CH, CH)
        s = _dot(kc_ref[pl.ds(r0, CH), :], qt4)
        valid = (t_row - r0 * CMP_STRIDE) >= end_rel
        key_rel = end_rel_f + (r0 * CMP_STRIDE - t0).astype(F32)
        ms = []
        for r in range(R):
            v = jnp.where(valid, s[:, r * T:(r + 1) * T] + sl[r] * key_rel, NEG_INF)
            s_ref[pl.ds(r0, CH), r * T:(r + 1) * T] = v
            ms.append(jnp.max(v, axis=0, keepdims=True))
        return jnp.maximum(m, jnp.concatenate(ms, axis=1))

    m = lax.fori_loop(0, n_chunks, scores, jnp.full((1, R * T), M_INIT, F32))

    def probs(c, carry):
        l, acc = carry
        r0 = pl.multiple_of(c * CH, CH)
        p = jnp.exp2(s_ref[pl.ds(r0, CH), :] - m)
        s_ref[pl.ds(r0, CH), :] = p
        l = l + jnp.sum(p, axis=0, keepdims=True)
        acc = acc + _dot(vct_ref[:, pl.ds(r0, CH)], p.astype(BF16))
        return l, acc

    l, acc = lax.fori_loop(0, n_chunks, probs, (jnp.zeros((1, R * T), F32), jnp.zeros((HEAD_DIM, R * T), F32)))
    inv = jnp.where(l > 0.0, 1.0 / l, 0.0)

    PAD = 8
    LW = 128
    for u in range(T // LW):
        psum_ref[u, 0:PAD, :] = jnp.zeros((PAD, LW), F32)
    for c in range(n_cmp_rows // CH):
        @pl.when(c < n_chunks)
        def _():
            pn = s_ref[c * CH:(c + 1) * CH, :] * inv
            tot = pn[:, 0:T]
            for r in range(1, R):
                tot = tot + pn[:, r * T:(r + 1) * T]
            for u in range(T // LW):
                psum_ref[u, PAD + c * CH:PAD + (c + 1) * CH, :] = tot[:, u * LW:(u + 1) * LW]

        @pl.when(c >= n_chunks)
        def _():
            for u in range(T // LW):
                psum_ref[u, PAD + c * CH:PAD + (c + 1) * CH, :] = jnp.zeros((CH, LW), F32)

    ratio, b_r = SEL_BLOCK // CMP_STRIDE, CMP_LEN // CMP_STRIDE
    imp = jnp.zeros((n_blk, T), F32)
    for shift in range(ratio + b_r - 1):
        mult = sum(1 for mm in range(ratio) for nn in range(b_r) if mm + nn == shift)
        rows = pl.ds(PAD + ratio - 1 - shift, n_blk, stride=ratio)
        imp = imp + float(mult) * jnp.concatenate([psum_ref[u, rows, :] for u in range(T // LW)], axis=1)

    jb = lax.broadcasted_iota(jnp.int32, (n_blk, T), 0)
    jt = t_row // SEL_BLOCK
    forced = (jb == 0) | (jb == jt) | (jb == jt - 1)
    quota = (n_sel - 1 - jnp.minimum(jt, 2)).astype(F32)
    work0 = jnp.where(forced, NEG_INF, jnp.where(jb <= jt, imp, -1.0))

    n_rounds = n_sel - jnp.where(t0 >= 2 * SEL_BLOCK, 3, 1)

    def select_among(rows):
        jbf_v = lax.broadcasted_iota(jnp.int32, (rows, T), 0).astype(F32)

        def pick(i, work):
            top = jnp.max(work, axis=0, keepdims=True)
            first = jnp.min(jnp.where(work == top, jbf_v, float(n_blk)), axis=0, keepdims=True)
            first = jnp.where(i.astype(F32) < quota, first, -1.0)
            return jnp.where(jbf_v == first, NEG_INF, work)

        work = lax.fori_loop(0, n_rounds, pick, work0[:rows])
        selt_ref[0:rows, :] = jnp.where(work == NEG_INF, 1.0, 0.0)
        if rows < n_blk:
            selt_ref[rows:n_blk, :] = jnp.zeros((n_blk - rows, T), F32)

    n_var = min(8, n_blk // 8)
    sec_rows = n_blk // n_var
    section = jnp.minimum(((t0 + T) // SEL_BLOCK - 1) // sec_rows, n_var - 1)
    for var in range(n_var):
        pl.when(section == var)(functools.partial(select_among, (var + 1) * sec_rows))
    any_ref[...] = _dot_nt(jnp.ones((8, T), BF16), selt_ref[...].astype(BF16))

    wlen = WINDOW + T
    start = pl.multiple_of(jnp.maximum(t0 - WINDOW, 0), T)
    q_aug = jnp.concatenate([qt4, _slope_rows(sl, T, jnp.zeros((8, R * T), F32)),
                             jnp.zeros((HEAD_DIM - 16, R * T), BF16)], axis=0)
    s = _dot(jnp.concatenate([kw_ref[pl.ds(start, wlen), :], posw_ref[...]], axis=1), q_aug)
    krow = lax.broadcasted_iota(jnp.int32, (wlen, T), 0)
    dist = (t_row - start) - krow
    valid = (dist >= 0) & (dist < WINDOW)
    vwt = vwt_ref[:, pl.ds(start, wlen)]
    for r in range(R):
        cs = slice(r * T, (r + 1) * T)
        v = jnp.where(valid, s[:, cs], NEG_INF)
        p = jnp.exp2(v - jnp.max(v, axis=0, keepdims=True))
        lw = jnp.sum(p, axis=0, keepdims=True)
        ow = _dot(vwt, p.astype(BF16))
        head = g * R + r
        o_t = acc[:, cs] * (inv[:, cs] * _gate_row(tailt_ref, 0, head)) + ow * (_gate_row(tailt_ref, 2, head) / lw)
        ocw_ref[:, r * HEAD_DIM:(r + 1) * HEAD_DIM] = o_t.T


def _cmp_win_attention(slopes, t_out, kc, vct, k_nat, tail_t, n_sel):
    L = t_out.shape[2]
    n_cmp_rows = L // CMP_STRIDE
    n_blk = L // SEL_BLOCK
    nqb = L // ATT_TQ
    chunk = min(512, n_cmp_rows)
    kern = functools.partial(_cmp_win_kernel, n_cmp_rows=n_cmp_rows, n_blk=n_blk, n_sel=n_sel, chunk=chunk)
    vw_head0 = (D_ATTN + KV_DIM) // HEAD_DIM
    return pl.pallas_call(
        kern,
        grid=(N_KV_HEADS, nqb),
        in_specs=[
            pl.BlockSpec(memory_space=pltpu.SMEM),
            pl.BlockSpec((Q_PER_KV, HEAD_DIM, ATT_TQ), lambda g, i: (g, 0, i)),
            pl.BlockSpec((None, n_cmp_rows, HEAD_DIM), lambda g, i: (g, 0, 0)),
            pl.BlockSpec((None, HEAD_DIM, n_cmp_rows), lambda g, i: (g, 0, 0)),
            pl.BlockSpec((None, L, HEAD_DIM), lambda g, i: (N_KV_HEADS + g, 0, 0)),
            pl.BlockSpec((None, HEAD_DIM, L), lambda g, i: (vw_head0 + g, 0, 0)),
            pl.BlockSpec((TAIL, ATT_TQ), lambda g, i: (0, i)),
            pl.BlockSpec((WINDOW + ATT_TQ, HEAD_DIM), lambda g, i: (0, 0)),
        ],
        out_specs=[
            pl.BlockSpec((ATT_TQ, KV_DIM), lambda g, i: (i, g)),
            pl.BlockSpec((None, None, n_blk, ATT_TQ), lambda g, i: (g, i, 0, 0)),
            pl.BlockSpec((None, None, 8, n_blk), lambda g, i: (g, i, 0, 0)),
        ],
        out_shape=[
            jax.ShapeDtypeStruct((L, D_ATTN), F32),
            jax.ShapeDtypeStruct((N_KV_HEADS, nqb, n_blk, ATT_TQ), F32),
            jax.ShapeDtypeStruct((N_KV_HEADS, nqb, 8, n_blk), F32),
        ],
        scratch_shapes=[
            pltpu.VMEM((n_cmp_rows, Q_PER_KV * ATT_TQ), F32),
            pltpu.VMEM((ATT_TQ // 128, n_cmp_rows + 8, 128), F32),
        ],
        compiler_params=_params(("arbitrary", "arbitrary"), 48),
        name="cmp_win_attention",
    )(slopes, t_out, kc, vct, k_nat, t_out, tail_t,
      jnp.asarray(_position_columns(np.arange(WINDOW + ATT_TQ), False), BF16))


SEL_TK = 512


MASK_BIG = 2.0 ** 100


def _sel_kernel(tiles_ref, counts_ref, slopes_ref, qt_ref, ks_ref, vst_ref, selt_ref, tailt_ref, ocw_ref, posk_ref,
                o_ref, m_ref, l_ref, acc_ref, *, n_tiles_max):
    g = pl.program_id(0)
    qb = pl.program_id(1)
    T = SEL_TQ
    R = Q_PER_KV
    TK = SEL_TK
    bpt = TK // SEL_BLOCK
    t0 = qb * T
    qt4 = jnp.concatenate([qt_ref[r] for r in range(R)], axis=1)
    sl = [slopes_ref[g * R + r] * LOG2E for r in range(R)]
    sl_row = _slope_lane_row(sl, T)
    t_row4 = t0 + lax.broadcasted_iota(jnp.int32, (1, R * T), 1) % T
    zeros_tail = jnp.zeros((HEAD_DIM - 16, R * T), BF16)

    m_ref[...] = jnp.full(m_ref.shape, M_INIT, F32)
    l_ref[...] = jnp.zeros(l_ref.shape, F32)
    acc_ref[...] = jnp.zeros(acc_ref.shape, F32)
    step = g * pl.num_programs(1) + qb

    def visit(n, causal):
        kt = tiles_ref[step * n_tiles_max + n]
        k0 = pl.multiple_of(kt * TK, TK)
        b0 = pl.multiple_of(kt * bpt, bpt)
        sel8 = jnp.concatenate([selt_ref[u, pl.ds(b0, bpt), :] for u in range(T // ATT_TQ)], axis=1)
        mask8 = (sel8 - 1.0) * MASK_BIG
        q_aug = jnp.concatenate([qt4, _slope_rows(sl, T, jnp.concatenate([mask8] * R, axis=1)), zeros_tail], axis=0)
        s = _dot(jnp.concatenate([ks_ref[pl.ds(k0, TK), :], posk_ref[...]], axis=1), q_aug)
        t_rel = t_row4 - k0
        if causal:
            s = jnp.where(lax.broadcasted_iota(jnp.int32, (TK, R * T), 0) <= t_rel, s, NEG_INF)
        col = sl_row * t_rel.astype(F32)
        m_old = m_ref[...]
        m_new = jnp.maximum(m_old, jnp.max(s, axis=0, keepdims=True) - col)
        alpha = jnp.exp2(m_old - m_new)
        p = jnp.exp2(s - (m_new + col))
        l_ref[...] = alpha * l_ref[...] + jnp.sum(p, axis=0, keepdims=True)
        acc_ref[...] = alpha * acc_ref[...] + _dot(vst_ref[:, pl.ds(k0, TK)], p.astype(BF16))
        m_ref[...] = m_new

    count = counts_ref[step]

    def past_tile(n, carry):
        visit(n, causal=False)
        return carry

    lax.fori_loop(0, count - 1, past_tile, 0)
    visit(count - 1, causal=True)
    l = l_ref[...]
    inv = jnp.where(l > 0.0, 1.0 / l, 0.0)
    for r in range(R):
        cs = slice(r * T, (r + 1) * T)
        o_t = acc_ref[:, cs] * (inv[:, cs] * _gate_row(tailt_ref, 1, g * R + r))
        hs = slice(r * HEAD_DIM, (r + 1) * HEAD_DIM)
        o_ref[:, hs] = ocw_ref[:, hs] + o_t.T


def _active_tiles(blk_count, L):
    G = blk_count.shape[0]
    nqb = L // SEL_TQ
    n_t = L // SEL_TK
    flags = blk_count.reshape(G, nqb, SEL_TQ // ATT_TQ, n_t, SEL_TK // SEL_BLOCK).max(axis=(2, 4)) > 0.5
    kt = jnp.arange(n_t, dtype=jnp.int32)
    last = (jnp.arange(nqb, dtype=jnp.int32) * SEL_TQ + SEL_TQ - 1) // SEL_TK
    flags = flags & (kt[None, None, :] <= last[None, :, None])
    order = jnp.sort(jnp.where(flags, kt, n_t + kt), axis=-1)
    tiles = jnp.where(order < n_t, order, 0).astype(jnp.int32)
    return tiles.reshape(-1), flags.sum(axis=-1).astype(jnp.int32).reshape(-1)


def _sel_attention(slopes, t_out, k_nat, selt, blk_count, tail_t, ocw):
    L = t_out.shape[2]
    n_blk = L // SEL_BLOCK
    n_t = L // SEL_TK
    cols = Q_PER_KV * SEL_TQ
    vs_head0 = D_ATTN // HEAD_DIM
    tiles, counts = _active_tiles(blk_count, L)
    grid_spec = pltpu.PrefetchScalarGridSpec(
        num_scalar_prefetch=2,
        grid=(N_KV_HEADS, L // SEL_TQ),
        in_specs=[
            pl.BlockSpec(memory_space=pltpu.SMEM),
            pl.BlockSpec((Q_PER_KV, HEAD_DIM, SEL_TQ), lambda g, i, *_: (g, 0, i)),
            pl.BlockSpec((None, L, HEAD_DIM), lambda g, i, *_: (g, 0, 0)),
            pl.BlockSpec((None, HEAD_DIM, L), lambda g, i, *_: (vs_head0 + g, 0, 0)),
            pl.BlockSpec((None, SEL_TQ // ATT_TQ, n_blk, ATT_TQ), lambda g, i, *_: (g, i, 0, 0)),
            pl.BlockSpec((TAIL, SEL_TQ), lambda g, i, *_: (0, i)),
            pl.BlockSpec((SEL_TQ, KV_DIM), lambda g, i, *_: (i, g)),
            pl.BlockSpec((SEL_TK, HEAD_DIM), lambda g, i, *_: (0, 0)),
        ],
        out_specs=pl.BlockSpec((SEL_TQ, KV_DIM), lambda g, i, *_: (i, g)),
        scratch_shapes=[
            pltpu.VMEM((1, cols), F32),
            pltpu.VMEM((1, cols), F32),
            pltpu.VMEM((HEAD_DIM, cols), F32),
        ],
    )
    return pl.pallas_call(
        functools.partial(_sel_kernel, n_tiles_max=n_t),
        grid_spec=grid_spec,
        out_shape=jax.ShapeDtypeStruct((L, D_ATTN), F32),
        compiler_params=_params(("arbitrary", "arbitrary"), 48),
        name="selected_attention",
    )(tiles, counts, slopes, t_out, k_nat, t_out, selt, tail_t, ocw,
      jnp.asarray(_position_columns(np.arange(SEL_TK), True), BF16))


def _outproj_kernel(x_ref, oa_ref, ys_ref, nw_ref, wa_ref, ws_ref, o_ref, ya_ref):
    @pl.when(pl.program_id(1) == 0)
    def _():
        ya_ref[...] = _rms(oa_ref[...], nw_ref[...]).astype(BF16)

    o_ref[...] = x_ref[...] + _dot(ya_ref[...], wa_ref[...]) + _dot(ys_ref[...], ws_ref[...])


def _out_projection(x2, o_attn, ys, attn_norm_w, w_out, tm, tn):
    L = x2.shape[0]
    n_j = D_MODEL // tn
    return pl.pallas_call(
        _outproj_kernel,
        grid=(L // tm, n_j),
        in_specs=[
            pl.BlockSpec((tm, tn), lambda i, j: (i, j)),
            pl.BlockSpec((tm, D_ATTN), lambda i, j: (i, 0)),
            pl.BlockSpec((tm, D_SSM), lambda i, j: (i, 0)),
            pl.BlockSpec((1, D_ATTN), lambda i, j: (0, 0)),
            pl.BlockSpec((D_ATTN, tn), lambda i, j: (0, j), pipeline_mode=pl.Buffered(1 if n_j == 1 else 2)),
            pl.BlockSpec((D_SSM, tn), lambda i, j: (1, j), pipeline_mode=pl.Buffered(1 if n_j == 1 else 2)),
        ],
        out_specs=pl.BlockSpec((tm, tn), lambda i, j: (i, j)),
        out_shape=jax.ShapeDtypeStruct((L, D_MODEL), F32),
        scratch_shapes=[pltpu.VMEM((tm, D_ATTN), BF16)],
        compiler_params=_params(("arbitrary", "arbitrary"), 56),
        name="out_projection",
    )(x2, o_attn, ys, attn_norm_w, w_out, w_out)


FFN_HALO = 16


def _ffn_kernel(x_ref, halo_ref, nw_ref, wg_ref, wv_ref, cwg_ref, cwv_ref, cbg_ref, cbv_ref, wd_ref, fw_ref,
                o_ref, hn_ref, ug_ref, uv_ref, acc_ref, *, tm):
    i = pl.program_id(0)
    j = pl.program_id(1)
    H = FFN_HALO

    @pl.when(j == 0)
    def _():
        hn_ref[H:H + tm, :] = _rms(x_ref[...], nw_ref[...]).astype(BF16)
        acc_ref[...] = jnp.zeros_like(acc_ref)

    @pl.when((j == 0) & (i == 0))
    def _():
        hn_ref[0:H, :] = jnp.zeros((H, D_MODEL), BF16)

    @pl.when((j == 0) & (i > 0))
    def _():
        hn_ref[0:H, :] = _rms(halo_ref[...], nw_ref[...]).astype(BF16)

    hn = hn_ref[...]
    ug_ref[...] = _dot(hn, wg_ref[...])
    uv_ref[...] = _dot(hn, wv_ref[...])

    def conv(u_ref, w_ref, b_ref):
        acc = b_ref[...]
        for k in range(FFN_CONV):
            acc = acc + u_ref[pl.ds(H - (FFN_CONV - 1) + k, tm), :] * w_ref[k:k + 1, :]
        return acc

    act = _silu(conv(ug_ref, cwg_ref, cbg_ref)) * conv(uv_ref, cwv_ref, cbv_ref)
    acc_ref[...] += _dot(act.astype(BF16), wd_ref[...])

    @pl.when(j == pl.num_programs(1) - 1)
    def _():
        o_ref[...] = _rms(x_ref[...] + acc_ref[...], fw_ref[...])


def _conv_ffn(x1, norm_w, w_up, conv_w, conv_b, w_down, final_w, tm, tf):
    L = x1.shape[0]
    n_f = D_FF // tf
    hb = tm // FFN_HALO
    return pl.pallas_call(
        functools.partial(_ffn_kernel, tm=tm),
        grid=(L // tm, n_f),
        in_specs=[
            pl.BlockSpec((tm, D_MODEL), lambda i, j: (i, 0)),
            pl.BlockSpec((FFN_HALO, D_MODEL), lambda i, j: (jnp.maximum(i * hb - 1, 0), 0)),
            pl.BlockSpec((1, D_MODEL), lambda i, j: (0, 0)),
            pl.BlockSpec((D_MODEL, tf), lambda i, j: (0, j)),
            pl.BlockSpec((D_MODEL, tf), lambda i, j: (0, n_f + j)),
            pl.BlockSpec((FFN_CONV, tf), lambda i, j: (0, j)),
            pl.BlockSpec((FFN_CONV, tf), lambda i, j: (0, n_f + j)),
            pl.BlockSpec((1, tf), lambda i, j: (0, j)),
            pl.BlockSpec((1, tf), lambda i, j: (0, n_f + j)),
            pl.BlockSpec((tf, D_MODEL), lambda i, j: (j, 0)),
            pl.BlockSpec((1, D_MODEL), lambda i, j: (0, 0)),
        ],
        out_specs=pl.BlockSpec((tm, D_MODEL), lambda i, j: (i, 0)),
        out_shape=jax.ShapeDtypeStruct((L, D_MODEL), F32),
        scratch_shapes=[
            pltpu.VMEM((tm + FFN_HALO, D_MODEL), BF16),
            pltpu.VMEM((tm + FFN_HALO, tf), F32),
            pltpu.VMEM((tm + FFN_HALO, tf), F32),
            pltpu.VMEM((tm, D_MODEL), F32),
        ],
        compiler_params=_params(("arbitrary", "arbitrary"), 56),
        name="conv_ffn",
    )(x1, x1, norm_w, w_up, w_up, conv_w, conv_w, conv_b, conv_b, w_down, final_w)


def _head_expand():
    e = np.zeros((TAIL, D_SSM), np.float32)
    for h in range(SSM_HEADS):
        e[h, h * SSM_HEAD_DIM:(h + 1) * SSM_HEAD_DIM] = 1.0
    return e


def _layer(x2, mix_norm_w, w_in, ssm_conv_w, ssm_conv_b, ssm_dt_bias, ssm_a_log, ssm_d, ssm_norm_w,
           cmp_pos, cmp_w1, cmp_w2_k, cmp_w2_v, attn_norm_w, w_out, ffn_norm_w, w_up, ffn_conv_w, ffn_conv_b,
           w_down, final_norm_w):
    L = x2.shape[0]
    row = lambda v: v.reshape(1, -1).astype(F32)

    bounds = np.cumsum([0, D_SSM, D_SSM + 2 * SSM_GROUPS * SSM_STATE, SSM_HEADS, D_ATTN] + [KV_DIM] * 6
                       + [3 * N_HEADS])
    seg = lambda k: w_in[:, bounds[k]:bounds[k + 1]]
    z_w, xbc_w, dt_w, q_w, kc_w, vc_w, ks_w, vs_w, kw_w, vw_w, gl_w = [seg(k) for k in range(11)]
    w_nat = jnp.concatenate([z_w, xbc_w, kc_w, vc_w, ks_w, kw_w], axis=1).astype(BF16)
    w_tr = jnp.concatenate([q_w, vs_w, vw_w], axis=1).T.astype(BF16)
    w_tail = jnp.concatenate([dt_w, gl_w, jnp.zeros((D_MODEL, TAIL - SSM_HEADS - 3 * N_HEADS), F32)], axis=1)
    w_tail = w_tail.astype(BF16)
    pad_row = lambda v: jnp.concatenate([v.astype(F32), jnp.zeros((TAIL - SSM_HEADS,), F32)]).reshape(1, TAIL)

    p, k_nat, t_out, tail, tail_t = _in_projection(x2, row(mix_norm_w), w_nat, w_tr, w_tail, w_tail.T,
                                                   tm=min(1024, L))
    kc, vct = _compress(p, cmp_pos.astype(F32), cmp_w1.astype(BF16), cmp_w2_k.astype(BF16),
                        cmp_w2_v.T.astype(BF16))

    y_ssm = _ssd(p, tail, tail_t, ssm_conv_w.astype(F32), row(ssm_conv_b), pad_row(ssm_dt_bias),
                 ssm_dt_bias.astype(F32).reshape(SSM_HEADS, 1), pad_row(ssm_a_log),
                 ssm_a_log.astype(F32).reshape(SSM_HEADS, 1),
                 row(jnp.repeat(ssm_d.astype(F32), SSM_HEAD_DIM)), jnp.asarray(_head_expand(), BF16),
                 row(ssm_norm_w))

    n_blk = L // SEL_BLOCK
    slopes = 2.0 ** (-8.0 * jnp.arange(1, N_HEADS + 1, dtype=F32) / N_HEADS)
    ocw, selt, blk_count = _cmp_win_attention(slopes, t_out, kc, vct, k_nat, tail_t, min(N_SEL, n_blk))
    o_attn = _sel_attention(slopes, t_out, k_nat, selt, blk_count[:, :, 0, :], tail_t, ocw)

    x1 = _out_projection(x2, o_attn, y_ssm, row(attn_norm_w), w_out.astype(BF16), tm=512, tn=D_MODEL)
    return _conv_ffn(x1, row(ffn_norm_w), w_up.astype(BF16), ffn_conv_w.astype(F32), row(ffn_conv_b),
                     w_down.astype(BF16), row(final_norm_w), tm=512, tf=512)


def kernel(x, mix_norm_w, w_in, ssm_conv_w, ssm_conv_b, ssm_dt_bias, ssm_a_log, ssm_d, ssm_norm_w, cmp_pos_k,
           cmp_w1_k, cmp_w2_k, cmp_pos_v, cmp_w1_v, cmp_w2_v, attn_norm_w, w_out, ffn_norm_w, w_up, ffn_conv_w,
           ffn_conv_b, w_down, final_norm_w):
    bsz, L, _ = x.shape
    assert bsz == 1 and mix_norm_w.shape[0] == 1, "single sequence, single layer"
    assert L % 1024 == 0 and L >= WINDOW + ATT_TQ
    out = _layer(
        x[0], mix_norm_w[0], w_in[0], ssm_conv_w[0], ssm_conv_b[0], ssm_dt_bias[0], ssm_a_log[0], ssm_d[0],
        ssm_norm_w[0], jnp.stack([cmp_pos_k[0], cmp_pos_v[0]]), jnp.stack([cmp_w1_k[0], cmp_w1_v[0]]),
        cmp_w2_k[0], cmp_w2_v[0], attn_norm_w[0], w_out[0], ffn_norm_w[0], w_up[0], ffn_conv_w[0],
        ffn_conv_b[0], w_down[0], final_norm_w)
    return out[None]
```

```python
import functools

import numpy as np
import jax
import jax.numpy as jnp
from jax import lax
from jax.experimental import pallas as pl
from jax.experimental.pallas import tpu as pltpu

F32 = jnp.float32
BF16 = jnp.bfloat16

D_MODEL = 2048
D_ATTN = 2048
D_SSM = 2048
N_HEADS = 16
HEAD_DIM = 128
N_KV_HEADS = 4
Q_PER_KV = 4
KV_DIM = 512
CMP_LEN = 32
CMP_STRIDE = 16
CMP_HIDDEN = 256
SEL_BLOCK = 64
N_SEL = 16
WINDOW = 512
ATT_TQ = 256
SEL_TQ = 512
SSM_HEAD_DIM = 64
SSM_HEADS = 32
SSM_GROUPS = 4
SSM_STATE = 128
SSM_CONV = 4
SSM_CHUNK = 256
D_FF = 5632
FFN_CONV = 3
NORM_EPS = 1e-6
NEG_INF = -1e30
FORCE_SCORE = 1e4

LOG2E = 1.4426950408889634
M_INIT = -1e20

P_COLS = 6144
K_COLS = 1024
T_ROWS = 3072
TAIL = 128
PROJ_TN = 1024
KC_COL0 = 5120
VC_COL0 = 5632
MIB = 1024 * 1024


def _params(sem, vmem_mib):
    return pltpu.CompilerParams(dimension_semantics=sem, vmem_limit_bytes=vmem_mib * MIB)


def _rms(x, w):
    return x * lax.rsqrt(jnp.mean(x * x, axis=-1, keepdims=True) + NORM_EPS) * w


def _dot(a, b):
    return jnp.dot(a, b, preferred_element_type=F32)


def _dot_nt(a, b):
    return lax.dot_general(a, b, (((1,), (1,)), ((), ())), preferred_element_type=F32)


def _split3(x):
    hi = x.astype(BF16)
    r1 = x - hi.astype(F32)
    mid = r1.astype(BF16)
    lo = (r1 - mid.astype(F32)).astype(BF16)
    return hi, mid, lo


def _dot3(x, m):
    hi, mid, lo = _split3(x)
    return _dot(hi, m) + _dot(mid, m) + _dot(lo, m)


def _dot3_left(m, x):
    hi, mid, lo = _split3(x)
    return _dot(m, hi) + _dot(m, mid) + _dot(m, lo)


def _silu(x):
    return x * (1.0 / (1.0 + jnp.exp(-x)))


def _sigmoid(x):
    return 1.0 / (1.0 + jnp.exp(-x))


def _inproj_kernel(x_ref, nw_ref, w_ref, wtr_ref, wt_ref, wtt_ref, p_ref, k_ref, t_ref, tail_ref, tailt_ref,
                   xn_ref, *, n_p, n_q):
    j = pl.program_id(1)
    hp = PROJ_TN // HEAD_DIM

    @pl.when(j == 0)
    def _():
        xn = _rms(x_ref[...], nw_ref[...]).astype(BF16)
        xn_ref[...] = xn
        tail_ref[...] = _dot(xn, wt_ref[...])
        tailt_ref[...] = _dot_nt(wtt_ref[...], xn)

    @pl.when(j <= n_p)
    def _():
        r = _dot(xn_ref[...], w_ref[...])

        @pl.when(j < n_p)
        def _():
            p_ref[...] = r

        @pl.when(j == n_p)
        def _():
            for c in range(hp):
                k_ref[c] = r[:, c * HEAD_DIM:(c + 1) * HEAD_DIM].astype(BF16)

    @pl.when(j > n_p)
    def _():
        r = _dot_nt(wtr_ref[...], xn_ref[...])
        r = r * jnp.where(j - n_p - 1 < n_q, HEAD_DIM ** -0.5 * LOG2E, 1.0)
        for c in range(hp):
            t_ref[c] = r[c * HEAD_DIM:(c + 1) * HEAD_DIM, :].astype(BF16)


def _in_projection(x2, norm_w, w_nat, w_tr, w_tail, w_tail_t, tm):
    L = x2.shape[0]
    n_p = P_COLS // PROJ_TN
    n_t = T_ROWS // PROJ_TN
    hp = PROJ_TN // HEAD_DIM
    return pl.pallas_call(
        functools.partial(_inproj_kernel, n_p=n_p, n_q=D_ATTN // PROJ_TN),
        grid=(L // tm, n_p + 1 + n_t),
        in_specs=[
            pl.BlockSpec((tm, D_MODEL), lambda i, j: (i, 0), pipeline_mode=pl.Buffered(1)),
            pl.BlockSpec((1, D_MODEL), lambda i, j: (0, 0)),
            pl.BlockSpec((D_MODEL, PROJ_TN), lambda i, j: (0, jnp.minimum(j, n_p))),
            pl.BlockSpec((PROJ_TN, D_MODEL), lambda i, j: (jnp.clip(j - n_p - 1, 0, n_t - 1), 0)),
            pl.BlockSpec((D_MODEL, TAIL), lambda i, j: (0, 0)),
            pl.BlockSpec((TAIL, D_MODEL), lambda i, j: (0, 0)),
        ],
        out_specs=[
            pl.BlockSpec((tm, PROJ_TN), lambda i, j: (i, jnp.minimum(j, n_p - 1))),
            pl.BlockSpec((hp, tm, HEAD_DIM), lambda i, j: (0, i, 0)),
            pl.BlockSpec((hp, HEAD_DIM, tm), lambda i, j: (jnp.clip(j - n_p - 1, 0, n_t - 1), 0, i)),
            pl.BlockSpec((tm, TAIL), lambda i, j: (i, 0)),
            pl.BlockSpec((TAIL, tm), lambda i, j: (0, i)),
        ],
        out_shape=[
            jax.ShapeDtypeStruct((L, P_COLS), F32),
            jax.ShapeDtypeStruct((K_COLS // HEAD_DIM, L, HEAD_DIM), BF16),
            jax.ShapeDtypeStruct((T_ROWS // HEAD_DIM, HEAD_DIM, L), BF16),
            jax.ShapeDtypeStruct((L, TAIL), F32),
            jax.ShapeDtypeStruct((TAIL, L), F32),
        ],
        scratch_shapes=[pltpu.VMEM((tm, D_MODEL), BF16)],
        compiler_params=_params(("arbitrary", "arbitrary"), 56),
        name="in_projection",
    )(x2, norm_w, w_nat, w_tr, w_tail, w_tail_t)


def _compress_hidden(x_ref, pos_ref, w1_ref, n_rows):
    half = CMP_LEN // 2
    acc0 = jnp.zeros((n_rows, CMP_HIDDEN), F32)
    acc1 = jnp.zeros((n_rows, CMP_HIDDEN), F32)
    for i in range(half):
        xi = x_ref[pl.ds(i, n_rows, stride=CMP_STRIDE), :]
        a0 = (xi + pos_ref[i:i + 1, :]).astype(BF16)
        a1 = (xi + pos_ref[half + i:half + i + 1, :]).astype(BF16)
        acc0 = acc0 + _dot(a0, w1_ref[i * HEAD_DIM:(i + 1) * HEAD_DIM, :])
        acc1 = acc1 + _dot(a1, w1_ref[(half + i) * HEAD_DIM:(half + i + 1) * HEAD_DIM, :])
    hid = acc0 + pltpu.roll(acc1, n_rows - 1, 0)
    return jax.nn.gelu(hid).astype(BF16)


def _compress_kernel(xk_ref, xv_ref, pos_ref, w1_ref, w2k_ref, w2vt_ref, kc_ref, vct_ref, *, n_rows):
    hk = _compress_hidden(xk_ref, pos_ref.at[0], w1_ref.at[0], n_rows)
    kc_ref[...] = _dot(hk, w2k_ref[...]).astype(BF16)
    hv = _compress_hidden(xv_ref, pos_ref.at[1], w1_ref.at[1], n_rows)
    vct_ref[...] = _dot_nt(w2vt_ref[...], hv).astype(BF16)


def _compress(p, pos, w1, w2k, w2vt):
    L = p.shape[0]
    n_rows = L // CMP_STRIDE
    full = lambda a: pl.BlockSpec(a.shape, lambda g: (0,) * a.ndim)
    return pl.pallas_call(
        functools.partial(_compress_kernel, n_rows=n_rows),
        grid=(N_KV_HEADS,),
        in_specs=[
            pl.BlockSpec((L, HEAD_DIM), lambda g: (0, KC_COL0 // HEAD_DIM + g)),
            pl.BlockSpec((L, HEAD_DIM), lambda g: (0, VC_COL0 // HEAD_DIM + g)),
            full(pos), full(w1), full(w2k), full(w2vt),
        ],
        out_specs=[
            pl.BlockSpec((None, n_rows, HEAD_DIM), lambda g: (g, 0, 0)),
            pl.BlockSpec((None, HEAD_DIM, n_rows), lambda g: (g, 0, 0)),
        ],
        out_shape=[
            jax.ShapeDtypeStruct((N_KV_HEADS, n_rows, HEAD_DIM), BF16),
            jax.ShapeDtypeStruct((N_KV_HEADS, HEAD_DIM, n_rows), BF16),
        ],
        compiler_params=_params(("arbitrary",), 56),
        name="kv_compress",
    )(p, p, pos, w1, w2k, w2vt)


def _ssd_kernel(xs_ref, bc_ref, xs_halo_ref, bc_halo_ref, z_ref, tail_ref, tailt_ref, cw_ref, cb_ref,
                dtb_row_ref, dtb_col_ref, alog_row_ref, alog_col_ref, dskip_ref, expand_ref, nw_ref,
                o_ref, state_ref, cx_ref, cbc_ref):
    c = pl.program_id(0)
    Q = SSM_CHUNK
    GW = D_SSM // SSM_GROUPS
    halo = 8

    @pl.when(c == 0)
    def _():
        state_ref[...] = jnp.zeros_like(state_ref)
        cx_ref[0:halo, :] = jnp.zeros((halo, D_SSM), F32)
        cbc_ref[0:halo, :] = jnp.zeros((halo, 2 * SSM_GROUPS * SSM_STATE), F32)

    @pl.when(c > 0)
    def _():
        cx_ref[0:halo, :] = xs_halo_ref[...]
        cbc_ref[0:halo, :] = bc_halo_ref[...]

    cx_ref[halo:halo + Q, :] = xs_ref[...]
    cbc_ref[halo:halo + Q, :] = bc_ref[...]

    def conv_silu(ref, w, b):
        acc = b
        for k in range(SSM_CONV):
            acc = acc + ref[pl.ds(halo - (SSM_CONV - 1) + k, Q), :] * w[k:k + 1, :]
        return _silu(acc)

    cw = cw_ref[...]
    cb = cb_ref[...]
    xs = conv_silu(cx_ref, cw[:, :D_SSM], cb[:, :D_SSM])
    bcm = conv_silu(cbc_ref, cw[:, D_SSM:], cb[:, D_SSM:])

    def softplus(v):
        return jnp.maximum(v, 0.0) + jnp.log1p(jnp.exp(-jnp.abs(v)))

    dt = softplus(tail_ref[...] + dtb_row_ref[...])
    a = dt * (-jnp.exp(alog_row_ref[...]))
    rows = lax.broadcasted_iota(jnp.int32, (Q, Q), 0)
    cols = lax.broadcasted_iota(jnp.int32, (Q, Q), 1)
    causal = cols <= rows
    tri = jnp.where(causal, 1.0, 0.0).astype(BF16)
    a_cum = _dot3_left(tri, a)
    dt_t = softplus(tailt_ref[0:SSM_HEADS, :] + dtb_col_ref[...])
    a_t = dt_t * (-jnp.exp(alog_col_ref[...]))
    tri_t = jnp.where(rows <= cols, 1.0, 0.0).astype(BF16)
    a_cum_t = _dot3(a_t, tri_t)

    expand = expand_ref[...]
    dt_e = _dot3(dt, expand)
    ac_e = _dot3(a_cum, expand)
    xdt = xs * dt_e
    decay_to = jnp.exp(ac_e)
    a_last = ac_e[Q - 1:Q, :]
    xdd = xdt * jnp.exp(a_last - ac_e)
    chunk_decay = jnp.exp(a_last)

    lane = lax.broadcasted_iota(jnp.int32, (Q, 2 * SSM_HEAD_DIM), 1)
    first_head = lane < SSM_HEAD_DIM
    y_groups = []
    for g in range(SSM_GROUPS):
        bg = bcm[:, g * SSM_STATE:(g + 1) * SSM_STATE]
        cg = bcm[:, (SSM_GROUPS + g) * SSM_STATE:(SSM_GROUPS + g + 1) * SSM_STATE].astype(BF16)
        gmat = _dot_nt(cg, bg.astype(BF16))
        st = state_ref[g]
        y_off = _dot(cg, st.astype(BF16)) * decay_to[:, g * GW:(g + 1) * GW]
        new = _dot(bg.T.astype(BF16), xdd[:, g * GW:(g + 1) * GW].astype(BF16))
        state_ref[g] = st * chunk_decay[:, g * GW:(g + 1) * GW] + new
        pairs = []
        for pr in range(GW // (2 * SSM_HEAD_DIM)):
            h0 = g * (GW // SSM_HEAD_DIM) + 2 * pr
            xp = xdt[:, h0 * SSM_HEAD_DIM:(h0 + 2) * SSM_HEAD_DIM].astype(BF16)
            ys = []
            for h in (h0, h0 + 1):
                diff = a_cum[:, h:h + 1] - a_cum_t[h:h + 1, :]
                m = (gmat * jnp.exp(jnp.where(causal, diff, NEG_INF))).astype(BF16)
                ys.append(_dot(m, xp))
            pairs.append(jnp.where(first_head, ys[0], ys[1]))
        y_groups.append(jnp.concatenate(pairs, axis=1) + y_off)
    y = jnp.concatenate(y_groups, axis=1) + xs * dskip_ref[...]
    y = y * _silu(z_ref[...])
    o_ref[...] = _rms(y, nw_ref[...]).astype(BF16)


def _ssd(p, tail, tail_t, conv_w, conv_b, dtb_row, dtb_col, alog_row, alog_col, dskip_e, expand, norm_w):
    L = p.shape[0]
    Q = SSM_CHUNK
    bc_w = 2 * SSM_GROUPS * SSM_STATE
    xs_blk = D_SSM // D_SSM
    bc_blk = (2 * D_SSM) // bc_w
    hb = Q // 8
    full = lambda shape: pl.BlockSpec(shape, lambda c: (0,) * len(shape))
    return pl.pallas_call(
        _ssd_kernel,
        grid=(L // Q,),
        in_specs=[
            pl.BlockSpec((Q, D_SSM), lambda c: (c, xs_blk)),
            pl.BlockSpec((Q, bc_w), lambda c: (c, bc_blk)),
            pl.BlockSpec((8, D_SSM), lambda c: (jnp.maximum(c * hb - 1, 0), xs_blk)),
            pl.BlockSpec((8, bc_w), lambda c: (jnp.maximum(c * hb - 1, 0), bc_blk)),
            pl.BlockSpec((Q, D_SSM), lambda c: (c, 0)),
            pl.BlockSpec((Q, TAIL), lambda c: (c, 0)),
            pl.BlockSpec((TAIL, Q), lambda c: (0, c)),
            full(conv_w.shape), full(conv_b.shape), full(dtb_row.shape), full(dtb_col.shape),
            full(alog_row.shape), full(alog_col.shape), full(dskip_e.shape), full(expand.shape),
            full(norm_w.shape),
        ],
        out_specs=pl.BlockSpec((Q, D_SSM), lambda c: (c, 0)),
        out_shape=jax.ShapeDtypeStruct((L, D_SSM), BF16),
        scratch_shapes=[
            pltpu.VMEM((SSM_GROUPS, SSM_STATE, D_SSM // SSM_GROUPS), F32),
            pltpu.VMEM((Q + 8, D_SSM), F32),
            pltpu.VMEM((Q + 8, bc_w), F32),
        ],
        compiler_params=_params(("arbitrary",), 48),
        name="ssd_scan",
    )(p, p, p, p, p, tail, tail_t, conv_w, conv_b, dtb_row, dtb_col, alog_row, alog_col, dskip_e, expand, norm_w)


GATE_ROW0 = SSM_HEADS


def _gate_row(tailt_ref, branch, head):
    return _sigmoid(tailt_ref[pl.ds(GATE_ROW0 + branch * N_HEADS + head, 1), :])


POS_COARSE = 64


def _position_columns(positions, with_block_onehot):
    positions = np.asarray(positions)
    e = np.zeros((positions.shape[0], HEAD_DIM), np.float32)
    e[:, 0:3] = ((positions // POS_COARSE) * POS_COARSE)[:, None]
    e[:, 3:6] = (positions % POS_COARSE)[:, None]
    if with_block_onehot:
        k = np.arange(positions.shape[0])
        e[k, 8 + k // SEL_BLOCK] = 1.0
    return e


def _slope_lane_row(sl, T):
    n_heads = len(sl)
    lane_head = lax.broadcasted_iota(jnp.int32, (1, n_heads * T), 1) // T
    row = jnp.zeros((1, n_heads * T), F32)
    for r in range(n_heads):
        row = jnp.where(lane_head == r, sl[r], row)
    return row


def _slope_rows(sl, T, extra8):
    pieces = [x.astype(F32) for x in _split3(_slope_lane_row(sl, T))]
    top = jnp.concatenate(pieces + pieces + [jnp.zeros((2, len(sl) * T), F32)], axis=0)
    return jnp.concatenate([top, extra8], axis=0).astype(BF16)


def _cmp_win_kernel(slopes_ref, qt_ref, kc_ref, vct_ref, kw_ref, vwt_ref, tailt_ref, posw_ref,
                    ocw_ref, selt_ref, any_ref, s_ref, psum_ref, *, n_cmp_rows, n_blk, n_sel, chunk):
    g = pl.program_id(0)
    qb = pl.program_id(1)
    T = ATT_TQ
    R = Q_PER_KV
    CH = chunk
    t0 = qb * T
    qt4 = jnp.concatenate([qt_ref[r] for r in range(R)], axis=1)
    t_row = t0 + lax.broadcasted_iota(jnp.int32, (1, T), 1)
    sl = [slopes_ref[g * R + r] * LOG2E for r in range(R)]

    n_chunks = (t0 + T - CMP_LEN) // CMP_STRIDE // CH + 1
    end_rel = lax.broadcasted_iota(jnp.int32, (CH, T), 0) * CMP_STRIDE + (CMP_LEN - 1)
    end_rel_f = end_rel.astype(F32)

    def scores(c, m):
        r0 = pl.multiple_of(c * CH, CH)
        s = _dot(kc_ref[pl.ds(r0, CH), :], qt4)
        valid = (t_row - r0 * CMP_STRIDE) >= end_rel
        key_rel = end_rel_f + (r0 * CMP_STRIDE - t0).astype(F32)
        ms = []
        for r in range(R):
            v = jnp.where(valid, s[:, r * T:(r + 1) * T] + sl[r] * key_rel, NEG_INF)
            s_ref[pl.ds(r0, CH), r * T:(r + 1) * T] = v
            ms.append(jnp.max(v, axis=0, keepdims=True))
        return jnp.maximum(m, jnp.concatenate(ms, axis=1))

    m = lax.fori_loop(0, n_chunks, scores, jnp.full((1, R * T), M_INIT, F32))

    def probs(c, carry):
        l, acc = carry
        r0 = pl.multiple_of(c * CH, CH)
        p = jnp.exp2(s_ref[pl.ds(r0, CH), :] - m)
        s_ref[pl.ds(r0, CH), :] = p
        l = l + jnp.sum(p, axis=0, keepdims=True)
        acc = acc + _dot(vct_ref[:, pl.ds(r0, CH)], p.astype(BF16))
        return l, acc

    l, acc = lax.fori_loop(0, n_chunks, probs, (jnp.zeros((1, R * T), F32), jnp.zeros((HEAD_DIM, R * T), F32)))
    inv = jnp.where(l > 0.0, 1.0 / l, 0.0)

    PAD = 8
    LW = 128
    for u in range(T // LW):
        psum_ref[u, 0:PAD, :] = jnp.zeros((PAD, LW), F32)
    for c in range(n_cmp_rows // CH):
        @pl.when(c < n_chunks)
        def _():
            pn = s_ref[c * CH:(c + 1) * CH, :] * inv
            tot = pn[:, 0:T]
            for r in range(1, R):
                tot = tot + pn[:, r * T:(r + 1) * T]
            for u in range(T // LW):
                psum_ref[u, PAD + c * CH:PAD + (c + 1) * CH, :] = tot[:, u * LW:(u + 1) * LW]

        @pl.when(c >= n_chunks)
        def _():
            for u in range(T // LW):
                psum_ref[u, PAD + c * CH:PAD + (c + 1) * CH, :] = jnp.zeros((CH, LW), F32)

    ratio, b_r = SEL_BLOCK // CMP_STRIDE, CMP_LEN // CMP_STRIDE
    imp = jnp.zeros((n_blk, T), F32)
    for shift in range(ratio + b_r - 1):
        mult = sum(1 for mm in range(ratio) for nn in range(b_r) if mm + nn == shift)
        rows = pl.ds(PAD + ratio - 1 - shift, n_blk, stride=ratio)
        imp = imp + float(mult) * jnp.concatenate([psum_ref[u, rows, :] for u in range(T // LW)], axis=1)

    jb = lax.broadcasted_iota(jnp.int32, (n_blk, T), 0)
    jt = t_row // SEL_BLOCK
    forced = (jb == 0) | (jb == jt) | (jb == jt - 1)
    quota = (n_sel - 1 - jnp.minimum(jt, 2)).astype(F32)
    work0 = jnp.where(forced, NEG_INF, jnp.where(jb <= jt, imp, -1.0))

    n_rounds = n_sel - jnp.where(t0 >= 2 * SEL_BLOCK, 3, 1)

    def select_among(rows):
        jbf_v = lax.broadcasted_iota(jnp.int32, (rows, T), 0).astype(F32)

        def pick(i, work):
            top = jnp.max(work, axis=0, keepdims=True)
            first = jnp.min(jnp.where(work == top, jbf_v, float(n_blk)), axis=0, keepdims=True)
            first = jnp.where(i.astype(F32) < quota, first, -1.0)
            return jnp.where(jbf_v == first, NEG_INF, work)

        work = lax.fori_loop(0, n_rounds, pick, work0[:rows])
        selt_ref[0:rows, :] = jnp.where(work == NEG_INF, 1.0, 0.0)
        if rows < n_blk:
            selt_ref[rows:n_blk, :] = jnp.zeros((n_blk - rows, T), F32)

    n_var = min(8, n_blk // 8)
    sec_rows = n_blk // n_var
    section = jnp.minimum(((t0 + T) // SEL_BLOCK - 1) // sec_rows, n_var - 1)
    for var in range(n_var):
        pl.when(section == var)(functools.partial(select_among, (var + 1) * sec_rows))
    any_ref[...] = _dot_nt(jnp.ones((8, T), BF16), selt_ref[...].astype(BF16))

    wlen = WINDOW + T
    start = pl.multiple_of(jnp.maximum(t0 - WINDOW, 0), T)
    q_aug = jnp.concatenate([qt4, _slope_rows(sl, T, jnp.zeros((8, R * T), F32)),
                             jnp.zeros((HEAD_DIM - 16, R * T), BF16)], axis=0)
    s = _dot(jnp.concatenate([kw_ref[pl.ds(start, wlen), :], posw_ref[...]], axis=1), q_aug)
    krow = lax.broadcasted_iota(jnp.int32, (wlen, T), 0)
    dist = (t_row - start) - krow
    valid = (dist >= 0) & (dist < WINDOW)
    vwt = vwt_ref[:, pl.ds(start, wlen)]
    for r in range(R):
        cs = slice(r * T, (r + 1) * T)
        v = jnp.where(valid, s[:, cs], NEG_INF)
        p = jnp.exp2(v - jnp.max(v, axis=0, keepdims=True))
        lw = jnp.sum(p, axis=0, keepdims=True)
        ow = _dot(vwt, p.astype(BF16))
        head = g * R + r
        o_t = acc[:, cs] * (inv[:, cs] * _gate_row(tailt_ref, 0, head)) + ow * (_gate_row(tailt_ref, 2, head) / lw)
        ocw_ref[:, r * HEAD_DIM:(r + 1) * HEAD_DIM] = o_t.T


def _cmp_win_attention(slopes, t_out, kc, vct, k_nat, tail_t, n_sel):
    L = t_out.shape[2]
    n_cmp_rows = L // CMP_STRIDE
    n_blk = L // SEL_BLOCK
    nqb = L // ATT_TQ
    chunk = min(512, n_cmp_rows)
    kern = functools.partial(_cmp_win_kernel, n_cmp_rows=n_cmp_rows, n_blk=n_blk, n_sel=n_sel, chunk=chunk)
    vw_head0 = (D_ATTN + KV_DIM) // HEAD_DIM
    return pl.pallas_call(
        kern,
        grid=(N_KV_HEADS, nqb),
        in_specs=[
            pl.BlockSpec(memory_space=pltpu.SMEM),
            pl.BlockSpec((Q_PER_KV, HEAD_DIM, ATT_TQ), lambda g, i: (g, 0, i)),
            pl.BlockSpec((None, n_cmp_rows, HEAD_DIM), lambda g, i: (g, 0, 0)),
            pl.BlockSpec((None, HEAD_DIM, n_cmp_rows), lambda g, i: (g, 0, 0)),
            pl.BlockSpec((None, L, HEAD_DIM), lambda g, i: (N_KV_HEADS + g, 0, 0)),
            pl.BlockSpec((None, HEAD_DIM, L), lambda g, i: (vw_head0 + g, 0, 0)),
            pl.BlockSpec((TAIL, ATT_TQ), lambda g, i: (0, i)),
            pl.BlockSpec((WINDOW + ATT_TQ, HEAD_DIM), lambda g, i: (0, 0)),
        ],
        out_specs=[
            pl.BlockSpec((ATT_TQ, KV_DIM), lambda g, i: (i, g)),
            pl.BlockSpec((None, None, n_blk, ATT_TQ), lambda g, i: (g, i, 0, 0)),
            pl.BlockSpec((None, None, 8, n_blk), lambda g, i: (g, i, 0, 0)),
        ],
        out_shape=[
            jax.ShapeDtypeStruct((L, D_ATTN), F32),
            jax.ShapeDtypeStruct((N_KV_HEADS, nqb, n_blk, ATT_TQ), F32),
            jax.ShapeDtypeStruct((N_KV_HEADS, nqb, 8, n_blk), F32),
        ],
        scratch_shapes=[
            pltpu.VMEM((n_cmp_rows, Q_PER_KV * ATT_TQ), F32),
            pltpu.VMEM((ATT_TQ // 128, n_cmp_rows + 8, 128), F32),
        ],
        compiler_params=_params(("arbitrary", "arbitrary"), 48),
        name="cmp_win_attention",
    )(slopes, t_out, kc, vct, k_nat, t_out, tail_t,
      jnp.asarray(_position_columns(np.arange(WINDOW + ATT_TQ), False), BF16))


SEL_TK = 512


MASK_BIG = 2.0 ** 100


def _sel_kernel(tiles_ref, counts_ref, slopes_ref, qt_ref, ks_ref, vst_ref, selt_ref, tailt_ref, ocw_ref, posk_ref,
                o_ref, m_ref, l_ref, acc_ref, *, n_tiles_max):
    g = pl.program_id(0)
    qb = pl.program_id(1)
    T = SEL_TQ
    R = Q_PER_KV
    TK = SEL_TK
    bpt = TK // SEL_BLOCK
    t0 = qb * T
    qt4 = jnp.concatenate([qt_ref[r] for r in range(R)], axis=1)
    sl = [slopes_ref[g * R + r] * LOG2E for r in range(R)]
    sl_row = _slope_lane_row(sl, T)
    t_row4 = t0 + lax.broadcasted_iota(jnp.int32, (1, R * T), 1) % T
    zeros_tail = jnp.zeros((HEAD_DIM - 16, R * T), BF16)

    m_ref[...] = jnp.full(m_ref.shape, M_INIT, F32)
    l_ref[...] = jnp.zeros(l_ref.shape, F32)
    acc_ref[...] = jnp.zeros(acc_ref.shape, F32)
    step = g * pl.num_programs(1) + qb

    def visit(n, causal):
        kt = tiles_ref[step * n_tiles_max + n]
        k0 = pl.multiple_of(kt * TK, TK)
        b0 = pl.multiple_of(kt * bpt, bpt)
        sel8 = jnp.concatenate([selt_ref[u, pl.ds(b0, bpt), :] for u in range(T // ATT_TQ)], axis=1)
        mask8 = (sel8 - 1.0) * MASK_BIG
        q_aug = jnp.concatenate([qt4, _slope_rows(sl, T, jnp.concatenate([mask8] * R, axis=1)), zeros_tail], axis=0)
        s = _dot(jnp.concatenate([ks_ref[pl.ds(k0, TK), :], posk_ref[...]], axis=1), q_aug)
        t_rel = t_row4 - k0
        if causal:
            s = jnp.where(lax.broadcasted_iota(jnp.int32, (TK, R * T), 0) <= t_rel, s, NEG_INF)
        col = sl_row * t_rel.astype(F32)
        m_old = m_ref[...]
        m_new = jnp.maximum(m_old, jnp.max(s, axis=0, keepdims=True) - col)
        alpha = jnp.exp2(m_old - m_new)
        p = jnp.exp2(s - (m_new + col))
        l_ref[...] = alpha * l_ref[...] + jnp.sum(p, axis=0, keepdims=True)
        acc_ref[...] = alpha * acc_ref[...] + _dot(vst_ref[:, pl.ds(k0, TK)], p.astype(BF16))
        m_ref[...] = m_new

    count = counts_ref[step]

    def past_tile(n, carry):
        visit(n, causal=False)
        return carry

    lax.fori_loop(0, count - 1, past_tile, 0)
    visit(count - 1, causal=True)
    l = l_ref[...]
    inv = jnp.where(l > 0.0, 1.0 / l, 0.0)
    for r in range(R):
        cs = slice(r * T, (r + 1) * T)
        o_t = acc_ref[:, cs] * (inv[:, cs] * _gate_row(tailt_ref, 1, g * R + r))
        hs = slice(r * HEAD_DIM, (r + 1) * HEAD_DIM)
        o_ref[:, hs] = ocw_ref[:, hs] + o_t.T


def _active_tiles(blk_count, L):
    G = blk_count.shape[0]
    nqb = L // SEL_TQ
    n_t = L // SEL_TK
    flags = blk_count.reshape(G, nqb, SEL_TQ // ATT_TQ, n_t, SEL_TK // SEL_BLOCK).max(axis=(2, 4)) > 0.5
    kt = jnp.arange(n_t, dtype=jnp.int32)
    last = (jnp.arange(nqb, dtype=jnp.int32) * SEL_TQ + SEL_TQ - 1) // SEL_TK
    flags = flags & (kt[None, None, :] <= last[None, :, None])
    order = jnp.sort(jnp.where(flags, kt, n_t + kt), axis=-1)
    tiles = jnp.where(order < n_t, order, 0).astype(jnp.int32)
    return tiles.reshape(-1), flags.sum(axis=-1).astype(jnp.int32).reshape(-1)


def _sel_attention(slopes, t_out, k_nat, selt, blk_count, tail_t, ocw):
    L = t_out.shape[2]
    n_blk = L // SEL_BLOCK
    n_t = L // SEL_TK
    cols = Q_PER_KV * SEL_TQ
    vs_head0 = D_ATTN // HEAD_DIM
    tiles, counts = _active_tiles(blk_count, L)
    grid_spec = pltpu.PrefetchScalarGridSpec(
        num_scalar_prefetch=2,
        grid=(N_KV_HEADS, L // SEL_TQ),
        in_specs=[
            pl.BlockSpec(memory_space=pltpu.SMEM),
            pl.BlockSpec((Q_PER_KV, HEAD_DIM, SEL_TQ), lambda g, i, *_: (g, 0, i)),
            pl.BlockSpec((None, L, HEAD_DIM), lambda g, i, *_: (g, 0, 0)),
            pl.BlockSpec((None, HEAD_DIM, L), lambda g, i, *_: (vs_head0 + g, 0, 0)),
            pl.BlockSpec((None, SEL_TQ // ATT_TQ, n_blk, ATT_TQ), lambda g, i, *_: (g, i, 0, 0)),
            pl.BlockSpec((TAIL, SEL_TQ), lambda g, i, *_: (0, i)),
            pl.BlockSpec((SEL_TQ, KV_DIM), lambda g, i, *_: (i, g)),
            pl.BlockSpec((SEL_TK, HEAD_DIM), lambda g, i, *_: (0, 0)),
        ],
        out_specs=pl.BlockSpec((SEL_TQ, KV_DIM), lambda g, i, *_: (i, g)),
        scratch_shapes=[
            pltpu.VMEM((1, cols), F32),
            pltpu.VMEM((1, cols), F32),
            pltpu.VMEM((HEAD_DIM, cols), F32),
        ],
    )
    return pl.pallas_call(
        functools.partial(_sel_kernel, n_tiles_max=n_t),
        grid_spec=grid_spec,
        out_shape=jax.ShapeDtypeStruct((L, D_ATTN), F32),
        compiler_params=_params(("arbitrary", "arbitrary"), 48),
        name="selected_attention",
    )(tiles, counts, slopes, t_out, k_nat, t_out, selt, tail_t, ocw,
      jnp.asarray(_position_columns(np.arange(SEL_TK), True), BF16))


def _outproj_kernel(x_ref, oa_ref, ys_ref, nw_ref, wa_ref, ws_ref, o_ref, ya_ref):
    @pl.when(pl.program_id(1) == 0)
    def _():
        ya_ref[...] = _rms(oa_ref[...], nw_ref[...]).astype(BF16)

    o_ref[...] = x_ref[...] + _dot(ya_ref[...], wa_ref[...]) + _dot(ys_ref[...], ws_ref[...])


def _out_projection(x2, o_attn, ys, attn_norm_w, w_out, tm, tn):
    L = x2.shape[0]
    n_j = D_MODEL // tn
    return pl.pallas_call(
        _outproj_kernel,
        grid=(L // tm, n_j),
        in_specs=[
            pl.BlockSpec((tm, tn), lambda i, j: (i, j)),
            pl.BlockSpec((tm, D_ATTN), lambda i, j: (i, 0)),
            pl.BlockSpec((tm, D_SSM), lambda i, j: (i, 0)),
            pl.BlockSpec((1, D_ATTN), lambda i, j: (0, 0)),
            pl.BlockSpec((D_ATTN, tn), lambda i, j: (0, j), pipeline_mode=pl.Buffered(1 if n_j == 1 else 2)),
            pl.BlockSpec((D_SSM, tn), lambda i, j: (1, j), pipeline_mode=pl.Buffered(1 if n_j == 1 else 2)),
        ],
        out_specs=pl.BlockSpec((tm, tn), lambda i, j: (i, j)),
        out_shape=jax.ShapeDtypeStruct((L, D_MODEL), F32),
        scratch_shapes=[pltpu.VMEM((tm, D_ATTN), BF16)],
        compiler_params=_params(("arbitrary", "arbitrary"), 56),
        name="out_projection",
    )(x2, o_attn, ys, attn_norm_w, w_out, w_out)


FFN_HALO = 16


def _ffn_kernel(x_ref, halo_ref, nw_ref, wg_ref, wv_ref, cwg_ref, cwv_ref, cbg_ref, cbv_ref, wd_ref, fw_ref,
                o_ref, hn_ref, ug_ref, uv_ref, acc_ref, *, tm):
    i = pl.program_id(0)
    j = pl.program_id(1)
    H = FFN_HALO

    @pl.when(j == 0)
    def _():
        hn_ref[H:H + tm, :] = _rms(x_ref[...], nw_ref[...]).astype(BF16)
        acc_ref[...] = jnp.zeros_like(acc_ref)

    @pl.when((j == 0) & (i == 0))
    def _():
        hn_ref[0:H, :] = jnp.zeros((H, D_MODEL), BF16)

    @pl.when((j == 0) & (i > 0))
    def _():
        hn_ref[0:H, :] = _rms(halo_ref[...], nw_ref[...]).astype(BF16)

    hn = hn_ref[...]
    ug_ref[...] = _dot(hn, wg_ref[...])
    uv_ref[...] = _dot(hn, wv_ref[...])

    def conv(u_ref, w_ref, b_ref):
        acc = b_ref[...]
        for k in range(FFN_CONV):
            acc = acc + u_ref[pl.ds(H - (FFN_CONV - 1) + k, tm), :] * w_ref[k:k + 1, :]
        return acc

    act = _silu(conv(ug_ref, cwg_ref, cbg_ref)) * conv(uv_ref, cwv_ref, cbv_ref)
    acc_ref[...] += _dot(act.astype(BF16), wd_ref[...])

    @pl.when(j == pl.num_programs(1) - 1)
    def _():
        o_ref[...] = _rms(x_ref[...] + acc_ref[...], fw_ref[...])


def _conv_ffn(x1, norm_w, w_up, conv_w, conv_b, w_down, final_w, tm, tf):
    L = x1.shape[0]
    n_f = D_FF // tf
    hb = tm // FFN_HALO
    return pl.pallas_call(
        functools.partial(_ffn_kernel, tm=tm),
        grid=(L // tm, n_f),
        in_specs=[
            pl.BlockSpec((tm, D_MODEL), lambda i, j: (i, 0), pipeline_mode=pl.Buffered(1)),
            pl.BlockSpec((FFN_HALO, D_MODEL), lambda i, j: (jnp.maximum(i * hb - 1, 0), 0)),
            pl.BlockSpec((1, D_MODEL), lambda i, j: (0, 0)),
            pl.BlockSpec((D_MODEL, tf), lambda i, j: (0, j)),
            pl.BlockSpec((D_MODEL, tf), lambda i, j: (0, n_f + j)),
            pl.BlockSpec((FFN_CONV, tf), lambda i, j: (0, j)),
            pl.BlockSpec((FFN_CONV, tf), lambda i, j: (0, n_f + j)),
            pl.BlockSpec((1, tf), lambda i, j: (0, j)),
            pl.BlockSpec((1, tf), lambda i, j: (0, n_f + j)),
            pl.BlockSpec((tf, D_MODEL), lambda i, j: (j, 0)),
            pl.BlockSpec((1, D_MODEL), lambda i, j: (0, 0)),
        ],
        out_specs=pl.BlockSpec((tm, D_MODEL), lambda i, j: (i, 0), pipeline_mode=pl.Buffered(1)),
        out_shape=jax.ShapeDtypeStruct((L, D_MODEL), F32),
        scratch_shapes=[
            pltpu.VMEM((tm + FFN_HALO, D_MODEL), BF16),
            pltpu.VMEM((tm + FFN_HALO, tf), F32),
            pltpu.VMEM((tm + FFN_HALO, tf), F32),
            pltpu.VMEM((tm, D_MODEL), F32),
        ],
        compiler_params=_params(("arbitrary", "arbitrary"), 60),
        name="conv_ffn",
    )(x1, x1, norm_w, w_up, w_up, conv_w, conv_w, conv_b, conv_b, w_down, final_w)


def _head_expand():
    e = np.zeros((TAIL, D_SSM), np.float32)
    for h in range(SSM_HEADS):
        e[h, h * SSM_HEAD_DIM:(h + 1) * SSM_HEAD_DIM] = 1.0
    return e


def _layer(x2, mix_norm_w, w_in, ssm_conv_w, ssm_conv_b, ssm_dt_bias, ssm_a_log, ssm_d, ssm_norm_w,
           cmp_pos, cmp_w1, cmp_w2_k, cmp_w2_v, attn_norm_w, w_out, ffn_norm_w, w_up, ffn_conv_w, ffn_conv_b,
           w_down, final_norm_w):
    L = x2.shape[0]
    row = lambda v: v.reshape(1, -1).astype(F32)

    bounds = np.cumsum([0, D_SSM, D_SSM + 2 * SSM_GROUPS * SSM_STATE, SSM_HEADS, D_ATTN] + [KV_DIM] * 6
                       + [3 * N_HEADS])
    seg = lambda k: w_in[:, bounds[k]:bounds[k + 1]]
    z_w, xbc_w, dt_w, q_w, kc_w, vc_w, ks_w, vs_w, kw_w, vw_w, gl_w = [seg(k) for k in range(11)]
    w_nat = jnp.concatenate([z_w, xbc_w, kc_w, vc_w, ks_w, kw_w], axis=1).astype(BF16)
    w_tr = jnp.concatenate([q_w, vs_w, vw_w], axis=1).T.astype(BF16)
    w_tail = jnp.concatenate([dt_w, gl_w, jnp.zeros((D_MODEL, TAIL - SSM_HEADS - 3 * N_HEADS), F32)], axis=1)
    w_tail = w_tail.astype(BF16)
    pad_row = lambda v: jnp.concatenate([v.astype(F32), jnp.zeros((TAIL - SSM_HEADS,), F32)]).reshape(1, TAIL)

    p, k_nat, t_out, tail, tail_t = _in_projection(x2, row(mix_norm_w), w_nat, w_tr, w_tail, w_tail.T,
                                                   tm=min(1024, L))
    kc, vct = _compress(p, cmp_pos.astype(F32), cmp_w1.astype(BF16), cmp_w2_k.astype(BF16),
                        cmp_w2_v.T.astype(BF16))

    y_ssm = _ssd(p, tail, tail_t, ssm_conv_w.astype(F32), row(ssm_conv_b), pad_row(ssm_dt_bias),
                 ssm_dt_bias.astype(F32).reshape(SSM_HEADS, 1), pad_row(ssm_a_log),
                 ssm_a_log.astype(F32).reshape(SSM_HEADS, 1),
                 row(jnp.repeat(ssm_d.astype(F32), SSM_HEAD_DIM)), jnp.asarray(_head_expand(), BF16),
                 row(ssm_norm_w))

    n_blk = L // SEL_BLOCK
    slopes = 2.0 ** (-8.0 * jnp.arange(1, N_HEADS + 1, dtype=F32) / N_HEADS)
    ocw, selt, blk_count = _cmp_win_attention(slopes, t_out, kc, vct, k_nat, tail_t, min(N_SEL, n_blk))
    o_attn = _sel_attention(slopes, t_out, k_nat, selt, blk_count[:, :, 0, :], tail_t, ocw)

    x1 = _out_projection(x2, o_attn, y_ssm, row(attn_norm_w), w_out.astype(BF16), tm=512, tn=D_MODEL)
    return _conv_ffn(x1, row(ffn_norm_w), w_up.astype(BF16), ffn_conv_w.astype(F32), row(ffn_conv_b),
                     w_down.astype(BF16), row(final_norm_w), tm=min(1024, L), tf=512)


def kernel(x, mix_norm_w, w_in, ssm_conv_w, ssm_conv_b, ssm_dt_bias, ssm_a_log, ssm_d, ssm_norm_w, cmp_pos_k,
           cmp_w1_k, cmp_w2_k, cmp_pos_v, cmp_w1_v, cmp_w2_v, attn_norm_w, w_out, ffn_norm_w, w_up, ffn_conv_w,
           ffn_conv_b, w_down, final_norm_w):
    bsz, L, _ = x.shape
    assert bsz == 1 and mix_norm_w.shape[0] == 1, "single sequence, single layer"
    assert L % 1024 == 0 and L >= WINDOW + ATT_TQ
    out = _layer(
        x[0], mix_norm_w[0], w_in[0], ssm_conv_w[0], ssm_conv_b[0], ssm_dt_bias[0], ssm_a_log[0], ssm_d[0],
        ssm_norm_w[0], jnp.stack([cmp_pos_k[0], cmp_pos_v[0]]), jnp.stack([cmp_w1_k[0], cmp_w1_v[0]]),
        cmp_w2_k[0], cmp_w2_v[0], attn_norm_w[0], w_out[0], ffn_norm_w[0], w_up[0], ffn_conv_w[0],
        ffn_conv_b[0], w_down[0], final_norm_w)
    return out[None]
```
